```python
import jax, jax.numpy as jnp
from jax import lax
import numpy as np

D_MODEL = 2048
BATCH = 2
SEQ = 4096
DEPTH = 4

N_MIXERS = 3
EPS = 1e-6
NEG_INF = -1e30

_FF_RAW = -(-8 * D_MODEL // 3)
D_FF = -(-_FF_RAW // 256) * 256

HD_A = 64
N_Q_A = D_MODEL // HD_A
N_KV_A = N_Q_A // 8
WINDOW = 128
BLOCK_A = 128

N_HEADS_B = 4
DK_B = D_MODEL // 2 // N_HEADS_B
DV_B = D_MODEL // N_HEADS_B
GATE_RANK = 16
GATE_TAU = 16.0
CHUNK = 64

N_HEADS_C = 16
QK_HEAD_C = 128
V_HEAD_C = 128
Q_RANK_C = 512
KV_RANK_C = 256
N_IDX_HEADS = 16
IDX_DIM = 64
TOPK_MAX = 256
C_BLOCK = 128
ATTN_SCALE_C = KV_RANK_C ** -0.5
IDX_W_SCALE = N_IDX_HEADS ** -0.5 * IDX_DIM ** -0.5

kernel_name = "hybrid_swa_gla_dsa_trunk"


def rms_norm(x, g):
    xf = x.astype(jnp.float32)
    y = xf * lax.rsqrt(jnp.mean(xf * xf, axis=-1, keepdims=True) + EPS)
    return (y * g.astype(jnp.float32)).astype(x.dtype)


def layer_norm(x, g, b):
    xf = x.astype(jnp.float32)
    mu = jnp.mean(xf, axis=-1, keepdims=True)
    xc = xf - mu
    y = xc * lax.rsqrt(jnp.mean(xc * xc, axis=-1, keepdims=True) + EPS)
    return (y * g.astype(jnp.float32) + b.astype(jnp.float32)).astype(x.dtype)


def swiglu(h, w_gate_up, w_down):
    g, u = jnp.split(h @ w_gate_up, 2, axis=-1)
    return (jax.nn.silu(g) * u) @ w_down


def _band(t):
    prev = jnp.concatenate([jnp.zeros_like(t[:, :1]), t[:, :-1]], axis=1)
    return jnp.concatenate([prev, t], axis=2)


def swa_sink_attention(h, w_in, q_norm, k_norm, sinks, wo):
    B, L, _ = h.shape
    nb = L // BLOCK_A
    G = N_Q_A // N_KV_A
    q, k, v = jnp.split(h @ w_in, [N_Q_A * HD_A, (N_Q_A + N_KV_A) * HD_A], axis=-1)
    q = rms_norm(q.reshape(B, nb, BLOCK_A, N_KV_A, G, HD_A), q_norm)
    k = rms_norm(k.reshape(B, nb, BLOCK_A, N_KV_A, HD_A), k_norm)
    v = v.reshape(B, nb, BLOCK_A, N_KV_A, HD_A)
    k_band, v_band = _band(k), _band(v)
    logits = jnp.einsum('bnqhgd,bnshd->bnhgqs', q, k_band).astype(jnp.float32) * HD_A ** -0.5
    i = jnp.arange(BLOCK_A)[:, None]
    j = jnp.arange(2 * BLOCK_A)[None, :]
    diff = i + BLOCK_A - j
    in_window = (diff >= 0) & (diff < WINDOW)
    not_pad = (jnp.arange(nb)[:, None, None] > 0) | (j[None] >= BLOCK_A)
    mask = in_window[None] & not_pad
    logits = jnp.where(mask[None, :, None, None], logits, NEG_INF)
    sink = jnp.broadcast_to(sinks.astype(jnp.float32).reshape(1, 1, N_KV_A, G, 1, 1),
                            logits.shape[:-1] + (1,))
    p = jax.nn.softmax(jnp.concatenate([logits, sink], axis=-1), axis=-1)[..., :-1]
    o = jnp.einsum('bnhgqs,bnshd->bnqhgd', p.astype(v.dtype), v_band)
    return o.reshape(B, L, N_Q_A * HD_A) @ wo


def gla_attention(h, w_in, w_gate_b, gate_bias, o_norm, wo):
    B, L, _ = h.shape
    nc = L // CHUNK
    f32 = jnp.float32
    dk_all, dv_all = N_HEADS_B * DK_B, N_HEADS_B * DV_B
    q, k, v, r, g_low = jnp.split(
        h @ w_in, [dk_all, 2 * dk_all, 2 * dk_all + dv_all, 2 * dk_all + 2 * dv_all], axis=-1)

    def heads(t, d):
        return t.astype(f32).reshape(B, nc, CHUNK, N_HEADS_B, d)

    q = heads(q, DK_B) * DK_B ** -0.5
    k = heads(k, DK_B)
    v = heads(v, DV_B)
    log_a = jax.nn.log_sigmoid((g_low @ w_gate_b + gate_bias).astype(f32)) / GATE_TAU
    bcum = jnp.cumsum(heads(log_a, DK_B), axis=2)
    b_last = bcum[:, :, -1:]
    q_dec = q * jnp.exp(bcum)
    k_inv = k * jnp.exp(-bcum)
    k_end = k * jnp.exp(b_last - bcum)
    causal = jnp.tril(jnp.ones((CHUNK, CHUNK), dtype=bool))
    a_intra = jnp.where(causal, jnp.einsum('bnthk,bnshk->bnhts', q_dec, k_inv), 0.0)
    o_intra = jnp.einsum('bnhts,bnshv->bnthv', a_intra, v)
    chunk_decay = jnp.exp(b_last[:, :, 0])

    def step(S, xs):
        qc, kc, vc, dc = xs
        o = jnp.einsum('bthk,bhkv->bthv', qc, S)
        S = S * dc[..., None] + jnp.einsum('bshk,bshv->bhkv', kc, vc)
        return S, o

    S0 = jnp.zeros((B, N_HEADS_B, DK_B, DV_B), f32)
    front = lambda t: jnp.moveaxis(t, 1, 0)
    _, o_inter = lax.scan(step, S0, (front(q_dec), front(k_end), front(v), front(chunk_decay)))
    o = o_intra + jnp.moveaxis(o_inter, 0, 1)
    o = rms_norm(o.reshape(B, L, N_HEADS_B, DV_B), o_norm).reshape(B, L, dv_all)
    o = o.astype(h.dtype) * jax.nn.silu(r)
    return o @ wo


def dsa_attention(h, w_in, q_lat_norm, kv_norm, w_uq, w_uk, q_abs_norm, w_iq,
                  idx_k_norm, idx_k_bias, w_uv, wo):
    B, L, _ = h.shape
    nb = L // C_BLOCK
    topk = min(TOPK_MAX, L // 4)
    f32 = jnp.float32
    c_q, c_kv, k_idx, w_idx = jnp.split(
        h @ w_in, [Q_RANK_C, Q_RANK_C + KV_RANK_C, Q_RANK_C + KV_RANK_C + IDX_DIM], axis=-1)
    c_q = rms_norm(c_q, q_lat_norm)
    c_kv = rms_norm(c_kv, kv_norm)
    q_nope = (c_q @ w_uq).reshape(B, L, N_HEADS_C, QK_HEAD_C)
    q_abs = rms_norm(jnp.einsum('blhd,hdc->blhc', q_nope, w_uk), q_abs_norm)
    q_idx = (c_q @ w_iq).reshape(B, L, N_IDX_HEADS, IDX_DIM)
    k_idx = layer_norm(k_idx, idx_k_norm, idx_k_bias)
    w_idx = w_idx * IDX_W_SCALE
    key_pos = jnp.arange(L, dtype=jnp.int32)

    def blocks(t):
        return jnp.moveaxis(t.reshape(B, nb, C_BLOCK, *t.shape[2:]), 1, 0)

    def attend_block(xs):
        qa, qi, wi, t0 = xs
        q_pos = t0 + jnp.arange(C_BLOCK, dtype=jnp.int32)
        rel = jax.nn.relu(jnp.einsum('bqhd,bsd->bqhs', qi, k_idx).astype(f32))
        score = jnp.einsum('bqhs,bqh->bqs', rel, wi.astype(f32))
        score = jnp.where((key_pos[None, :] <= q_pos[:, None])[None], score, NEG_INF)
        _, idx = lax.top_k(score, topk)
        kv_sel = jax.vmap(lambda c, i: c[i])(c_kv, idx)
        logits = jnp.einsum('bqhc,bqjc->bhqj', qa, kv_sel).astype(f32) * ATTN_SCALE_C
        valid = idx <= q_pos[None, :, None]
        logits = jnp.where(valid[:, None], logits, NEG_INF)
        p = jax.nn.softmax(logits, axis=-1).astype(kv_sel.dtype)
        return jnp.einsum('bhqj,bqjc->bqhc', p, kv_sel)

    starts = jnp.arange(nb, dtype=jnp.int32) * C_BLOCK
    o_lat = lax.map(attend_block, (blocks(q_abs), blocks(q_idx), blocks(w_idx), starts))
    o_lat = jnp.moveaxis(o_lat, 0, 1).reshape(B, L, N_HEADS_C, KV_RANK_C)
    o = jnp.einsum('blhc,hcv->blhv', o_lat, w_uv).reshape(B, L, N_HEADS_C * V_HEAD_C)
    return o @ wo


def setup_inputs(seed: int = 0) -> dict:
    key = jax.random.key(seed)
    keys = iter(jax.random.split(key, 128))

    def normal(shape, fan_in):
        return jax.random.normal(next(keys), shape, jnp.float32) * fan_in ** -0.5

    def gain(n):
        return 1.0 + 0.02 * jax.random.normal(next(keys), (n,), jnp.float32)

    def small(shape, scale):
        return scale * jax.random.normal(next(keys), shape, jnp.float32)

    p = {"x": jax.random.normal(next(keys), (BATCH, SEQ, D_MODEL), jnp.float32)}
    dk_all, dv_all = N_HEADS_B * DK_B, N_HEADS_B * DV_B
    for i in range(DEPTH):
        pre = f"l{i}_"
        kind = i % N_MIXERS
        p[pre + "norm_mix"] = gain(D_MODEL)
        if kind == 0:
            p[pre + "w_in"] = normal((D_MODEL, (N_Q_A + 2 * N_KV_A) * HD_A), D_MODEL)
            p[pre + "q_norm"] = gain(HD_A)
            p[pre + "k_norm"] = gain(HD_A)
            p[pre + "sinks"] = small((N_Q_A,), 0.5)
            p[pre + "wo"] = normal((N_Q_A * HD_A, D_MODEL), N_Q_A * HD_A)
        elif kind == 1:
            p[pre + "w_in"] = normal((D_MODEL, 2 * dk_all + 2 * dv_all + GATE_RANK), D_MODEL)
            p[pre + "w_gate_b"] = normal((GATE_RANK, dk_all), GATE_RANK)
            p[pre + "gate_bias"] = small((dk_all,), 0.1)
            p[pre + "o_norm"] = gain(DV_B)
            p[pre + "wo"] = normal((dv_all, D_MODEL), dv_all)
        else:
            p[pre + "w_in"] = normal((D_MODEL, Q_RANK_C + KV_RANK_C + IDX_DIM + N_IDX_HEADS), D_MODEL)
            p[pre + "q_lat_norm"] = gain(Q_RANK_C)
            p[pre + "kv_norm"] = gain(KV_RANK_C)
            p[pre + "w_uq"] = normal((Q_RANK_C, N_HEADS_C * QK_HEAD_C), Q_RANK_C)
            p[pre + "w_uk"] = normal((N_HEADS_C, QK_HEAD_C, KV_RANK_C), QK_HEAD_C)
            p[pre + "q_abs_norm"] = gain(KV_RANK_C)
            p[pre + "w_iq"] = normal((Q_RANK_C, N_IDX_HEADS * IDX_DIM), Q_RANK_C)
            p[pre + "idx_k_norm"] = gain(IDX_DIM)
            p[pre + "idx_k_bias"] = small((IDX_DIM,), 0.02)
            p[pre + "w_uv"] = normal((N_HEADS_C, KV_RANK_C, V_HEAD_C), KV_RANK_C)
            p[pre + "wo"] = normal((N_HEADS_C * V_HEAD_C, D_MODEL), N_HEADS_C * V_HEAD_C)
        p[pre + "norm_ffn"] = gain(D_MODEL)
        p[pre + "w_gate_up"] = normal((D_MODEL, 2 * D_FF), D_MODEL)
        p[pre + "w_down"] = normal((D_FF, D_MODEL), D_FF)
    return p


def reference(x,
              l0_norm_mix, l0_w_in, l0_q_norm, l0_k_norm, l0_sinks, l0_wo,
              l0_norm_ffn, l0_w_gate_up, l0_w_down,
              l1_norm_mix, l1_w_in, l1_w_gate_b, l1_gate_bias, l1_o_norm, l1_wo,
              l1_norm_ffn, l1_w_gate_up, l1_w_down,
              l2_norm_mix, l2_w_in, l2_q_lat_norm, l2_kv_norm, l2_w_uq, l2_w_uk, l2_q_abs_norm,
              l2_w_iq, l2_idx_k_norm, l2_idx_k_bias, l2_w_uv, l2_wo,
              l2_norm_ffn, l2_w_gate_up, l2_w_down,
              l3_norm_mix, l3_w_in, l3_q_norm, l3_k_norm, l3_sinks, l3_wo,
              l3_norm_ffn, l3_w_gate_up, l3_w_down):
    mixers = (swa_sink_attention, gla_attention, dsa_attention)
    layers = (
        (l0_norm_mix, (l0_w_in, l0_q_norm, l0_k_norm, l0_sinks, l0_wo),
         l0_norm_ffn, l0_w_gate_up, l0_w_down),
        (l1_norm_mix, (l1_w_in, l1_w_gate_b, l1_gate_bias, l1_o_norm, l1_wo),
         l1_norm_ffn, l1_w_gate_up, l1_w_down),
        (l2_norm_mix, (l2_w_in, l2_q_lat_norm, l2_kv_norm, l2_w_uq, l2_w_uk, l2_q_abs_norm,
                       l2_w_iq, l2_idx_k_norm, l2_idx_k_bias, l2_w_uv, l2_wo),
         l2_norm_ffn, l2_w_gate_up, l2_w_down),
        (l3_norm_mix, (l3_w_in, l3_q_norm, l3_k_norm, l3_sinks, l3_wo),
         l3_norm_ffn, l3_w_gate_up, l3_w_down),
    )
    h = x
    for i in range(DEPTH):
        norm_mix, mix_params, norm_ffn, w_gate_up, w_down = layers[i]
        h = h + mixers[i % N_MIXERS](rms_norm(h, norm_mix), *mix_params)
        h = h + swiglu(rms_norm(h, norm_ffn), w_gate_up, w_down)
    return h
```

```python
import functools

import jax
import jax.numpy as jnp
from jax import lax
from jax.experimental import pallas as pl
from jax.experimental.pallas import tpu as pltpu

F32 = jnp.float32
BF16 = jnp.bfloat16
EPS = 1e-6
NEG_INF = -1e30
LANES = 128

HD_A = 64
N_KV_A = 4
GROUP_A = 8
BLOCK_A = 128
N_HEADS_B = 4
GATE_RANK = 16
GATE_TAU = 16.0
CHUNK = 64
N_HEADS_C = 16
QK_HEAD_C = 128
V_HEAD_C = 128
Q_RANK_C = 512
KV_RANK_C = 256
N_IDX_HEADS = 16
IDX_DIM = 64
TOPK_MAX = 256
C_BLOCK = 128
KEY_TILE_C = 512
INT_MIN = -2 ** 31


def _nt_dot(a, b):
    return lax.dot_general(a, b, (((1,), (1,)), ((), ())), preferred_element_type=F32)


def _tn_dot(a, b):
    return lax.dot_general(a, b, (((0,), (0,)), ((), ())), preferred_element_type=F32)


def _dot(a, b):
    return jnp.dot(a, b, preferred_element_type=F32)


def _rms(x, g):
    return x * lax.rsqrt(jnp.mean(x * x, axis=-1, keepdims=True) + EPS) * g


def _params(*sem):
    return pltpu.CompilerParams(dimension_semantics=sem)


def _norm_matmul_kernel(x_ref, g_ref, w_ref, o_ref, xn_ref):
    @pl.when(pl.program_id(1) == 0)
    def _():
        xn_ref[...] = _rms(x_ref[...], g_ref[...]).astype(BF16)

    o_ref[...] = _dot(xn_ref[...], w_ref[...]).astype(o_ref.dtype)


def norm_matmul(x, g, w, *, tm, tn, out_dtype=F32):
    m, d = x.shape
    n = w.shape[1]
    return pl.pallas_call(
        _norm_matmul_kernel,
        grid=(m // tm, n // tn),
        in_specs=[
            pl.BlockSpec((tm, d), lambda i, j: (i, 0)),
            pl.BlockSpec((1, d), lambda i, j: (0, 0)),
            pl.BlockSpec((d, tn), lambda i, j: (0, j)),
        ],
        out_specs=pl.BlockSpec((tm, tn), lambda i, j: (i, j)),
        out_shape=jax.ShapeDtypeStruct((m, n), out_dtype),
        scratch_shapes=[pltpu.VMEM((tm, d), BF16)],
        compiler_params=_params("parallel", "arbitrary"),
        name="norm_matmul",
    )(x, g.reshape(1, d), w)


def _matmul_res_kernel(a_ref, w_ref, r_ref, o_ref):
    o_ref[...] = r_ref[...] + _dot(a_ref[...], w_ref[...])


def matmul_res(a, w, r, *, tm, tn):
    m, k = a.shape
    n = w.shape[1]
    return pl.pallas_call(
        _matmul_res_kernel,
        grid=(m // tm, n // tn),
        in_specs=[
            pl.BlockSpec((tm, k), lambda i, j: (i, 0)),
            pl.BlockSpec((k, tn), lambda i, j: (0, j)),
            pl.BlockSpec((tm, tn), lambda i, j: (i, j)),
        ],
        out_specs=pl.BlockSpec((tm, tn), lambda i, j: (i, j)),
        out_shape=jax.ShapeDtypeStruct((m, n), F32),
        compiler_params=_params("parallel", "arbitrary"),
        name="matmul_res",
    )(a, w, r)


def _ffn_kernel(x_ref, g_ref, wg_ref, wu_ref, wd_ref, o_ref, xn_ref):
    @pl.when(pl.program_id(1) == 0)
    def _():
        x = x_ref[...]
        xn_ref[...] = _rms(x, g_ref[...]).astype(BF16)
        o_ref[...] = x

    xn = xn_ref[...]
    gate = _dot(xn, wg_ref[...])
    up = _dot(xn, wu_ref[...])
    act = (gate * (1.0 / (1.0 + jnp.exp(-gate))) * up).astype(BF16)
    o_ref[...] += _dot(act, wd_ref[...])


def ffn(x, g, w_gate_up, w_down, *, tm, tf):
    m, d = x.shape
    ff = w_down.shape[0]
    nf = ff // tf
    return pl.pallas_call(
        _ffn_kernel,
        grid=(m // tm, nf),
        in_specs=[
            pl.BlockSpec((tm, d), lambda i, j: (i, 0)),
            pl.BlockSpec((1, d), lambda i, j: (0, 0)),
            pl.BlockSpec((d, tf), lambda i, j: (0, j)),
            pl.BlockSpec((d, tf), lambda i, j: (0, j + nf)),
            pl.BlockSpec((tf, d), lambda i, j: (j, 0)),
        ],
        out_specs=pl.BlockSpec((tm, d), lambda i, j: (i, 0)),
        out_shape=jax.ShapeDtypeStruct((m, d), F32),
        scratch_shapes=[pltpu.VMEM((tm, d), BF16)],
        compiler_params=_params("parallel", "arbitrary"),
        name="ffn",
    )(x, g.reshape(1, d), w_gate_up, w_gate_up, w_down)


def _swa_kernel(sink_ref, cur_ref, prev_ref, qn_ref, kn_ref, o_ref):
    n = pl.program_id(1)
    nq = N_KV_A * GROUP_A * HD_A
    nkv = N_KV_A * HD_A
    rows = GROUP_A * BLOCK_A
    qi = lax.broadcasted_iota(jnp.int32, (rows, 2 * BLOCK_A), 0) & (BLOCK_A - 1)
    kj = lax.broadcasted_iota(jnp.int32, (rows, 2 * BLOCK_A), 1)
    mask = (kj > qi) & (kj <= qi + BLOCK_A) & ((n > 0) | (kj >= BLOCK_A))
    qg = qn_ref[...] * HD_A ** -0.5
    kg = kn_ref[...]
    for h in range(N_KV_A):
        k = jnp.concatenate(
            [prev_ref[0, :, h * HD_A:(h + 1) * HD_A],
             cur_ref[0, :, nq + h * HD_A:nq + (h + 1) * HD_A]], axis=0)
        v = jnp.concatenate(
            [prev_ref[0, :, nkv + h * HD_A:nkv + (h + 1) * HD_A],
             cur_ref[0, :, nq + nkv + h * HD_A:nq + nkv + (h + 1) * HD_A]], axis=0)
        q = jnp.concatenate(
            [cur_ref[0, :, (h * GROUP_A + g) * HD_A:(h * GROUP_A + g + 1) * HD_A]
             for g in range(GROUP_A)], axis=0)
        sink = jnp.concatenate(
            [jnp.full((BLOCK_A, 1), sink_ref[h * GROUP_A + g], F32) for g in range(GROUP_A)],
            axis=0)
        s = _nt_dot(_rms(q, qg).astype(BF16), _rms(k, kg).astype(BF16))
        s = jnp.where(mask, s, NEG_INF)
        mx = jnp.maximum(jnp.max(s, axis=-1, keepdims=True), sink)
        p = jnp.exp(s - mx)
        den = jnp.sum(p, axis=-1, keepdims=True) + jnp.exp(sink - mx)
        o = _dot(p.astype(BF16), v.astype(BF16)) / den
        for g in range(GROUP_A):
            c0 = (h * GROUP_A + g) * HD_A
            o_ref[0, :, c0:c0 + HD_A] = o[g * BLOCK_A:(g + 1) * BLOCK_A].astype(o_ref.dtype)


def swa_attention(qkv, q_norm, k_norm, sinks):
    b, l, n = qkv.shape
    nq = N_KV_A * GROUP_A * HD_A
    kv_w = 2 * N_KV_A * HD_A
    return pl.pallas_call(
        _swa_kernel,
        grid=(b, l // BLOCK_A),
        in_specs=[
            pl.BlockSpec(memory_space=pltpu.SMEM),
            pl.BlockSpec((1, BLOCK_A, n), lambda i, j: (i, j, 0)),
            pl.BlockSpec((1, BLOCK_A, kv_w), lambda i, j: (i, jnp.maximum(j - 1, 0), nq // kv_w)),
            pl.BlockSpec((1, HD_A), lambda i, j: (0, 0)),
            pl.BlockSpec((1, HD_A), lambda i, j: (0, 0)),
        ],
        out_specs=pl.BlockSpec((1, BLOCK_A, nq), lambda i, j: (i, j, 0)),
        out_shape=jax.ShapeDtypeStruct((b, l, nq), BF16),
        compiler_params=_params("parallel", "parallel"),
        name="swa_attention",
    )(sinks, qkv, qkv, q_norm.reshape(1, HD_A), k_norm.reshape(1, HD_A))


def _split3(x):
    hi = x.astype(BF16)
    r1 = x - hi.astype(F32)
    mid = r1.astype(BF16)
    lo = (r1 - mid.astype(F32)).astype(BF16)
    return hi, mid, lo


def _gla_kernel(q_ref, k_ref, v_ref, r_ref, gl_ref, wgb_ref, gb_ref, on_ref, o_ref, st_ref,
                *, tb, dk, dv):
    @pl.when(pl.program_id(1) == 0)
    def _():
        st_ref[...] = jnp.zeros_like(st_ref)

    ti = lax.broadcasted_iota(jnp.int32, (CHUNK, CHUNK), 0)
    si = lax.broadcasted_iota(jnp.int32, (CHUNK, CHUNK), 1)
    causal = si <= ti
    tri = causal.astype(BF16)
    wgb = wgb_ref[...]
    gb = gb_ref[...]
    on = on_ref[...]

    def chunk(c, carry):
        rs = pl.ds(pl.multiple_of(c * CHUNK, CHUNK), CHUNK)
        gate = _dot(gl_ref[0, rs, :].astype(BF16), wgb) + gb
        log_a = (jnp.minimum(gate, 0.0) - jnp.log1p(jnp.exp(-jnp.abs(gate)))) / GATE_TAU
        hi, mid, lo = _split3(log_a)
        bcum = _dot(tri, hi) + _dot(tri, mid) + _dot(tri, lo)
        b_last = bcum[CHUNK - 1:CHUNK, :]
        q = q_ref[0, rs, :]
        k = k_ref[0, rs, :]
        q_dec = (q * dk ** -0.5 * jnp.exp(bcum)).astype(BF16)
        k_inv = (k * jnp.exp(-bcum)).astype(BF16)
        k_end = (k * jnp.exp(b_last - bcum)).astype(BF16)
        decay = jnp.exp(b_last)
        for h in range(N_HEADS_B):
            ks = slice(h * dk, (h + 1) * dk)
            vs = slice(h * dv, (h + 1) * dv)
            v = v_ref[0, rs, vs].astype(BF16)
            a = jnp.where(causal, _nt_dot(q_dec[:, ks], k_inv[:, ks]), 0.0)
            st = st_ref[h]
            o = _dot(a.astype(BF16), v) + _nt_dot(q_dec[:, ks], st.astype(BF16))
            st_ref[h] = st * decay[:, ks] + _tn_dot(v, k_end[:, ks])
            o = _rms(o, on)
            r = r_ref[0, rs, vs]
            o_ref[0, rs, vs] = (o * (r * (1.0 / (1.0 + jnp.exp(-r))))).astype(o_ref.dtype)
        return carry

    lax.fori_loop(0, tb // CHUNK, chunk, 0)


def gla_attention(proj, w_gate_b, gate_bias, o_norm, *, dk, dv, tb):
    b, l, _ = proj.shape
    dk_all, dv_all = N_HEADS_B * dk, N_HEADS_B * dv
    return pl.pallas_call(
        functools.partial(_gla_kernel, tb=tb, dk=dk, dv=dv),
        grid=(b, l // tb),
        in_specs=[
            pl.BlockSpec((1, tb, dk_all), lambda i, j: (i, j, 0)),
            pl.BlockSpec((1, tb, dk_all), lambda i, j: (i, j, 1)),
            pl.BlockSpec((1, tb, dv_all), lambda i, j: (i, j, 2 * dk_all // dv_all)),
            pl.BlockSpec((1, tb, dv_all), lambda i, j: (i, j, 2 * dk_all // dv_all + 1)),
            pl.BlockSpec((1, tb, LANES), lambda i, j: (i, j, (2 * dk_all + 2 * dv_all) // LANES)),
            pl.BlockSpec((LANES, dk_all), lambda i, j: (0, 0)),
            pl.BlockSpec((1, dk_all), lambda i, j: (0, 0)),
            pl.BlockSpec((1, dv), lambda i, j: (0, 0)),
        ],
        out_specs=pl.BlockSpec((1, tb, dv_all), lambda i, j: (i, j, 0)),
        out_shape=jax.ShapeDtypeStruct((b, l, dv_all), BF16),
        scratch_shapes=[pltpu.VMEM((N_HEADS_B, dv, dk), F32)],
        compiler_params=_params("parallel", "arbitrary"),
        name="gla_attention",
    )(proj, proj, proj, proj, proj, w_gate_b, gate_bias.reshape(1, dk_all), o_norm.reshape(1, dv))


def _dsa_prep_kernel(x_ref, g_ref, win_ref, qln_ref, kvn_ref, wuq_ref, wuk_ref, qan_ref, wiq_ref,
                     ikn_ref, ikb_ref, qabs_ref, qidx_ref, widx_ref, ckv_ref, kidx_ref):
    xn = _rms(x_ref[...], g_ref[...]).astype(BF16)
    proj = _dot(xn, win_ref[...])
    o_kv = Q_RANK_C
    o_ki = Q_RANK_C + KV_RANK_C
    o_wi = o_ki + IDX_DIM
    c_q = _rms(proj[:, :o_kv], qln_ref[...]).astype(BF16)
    ckv_ref[...] = _rms(proj[:, o_kv:o_ki], kvn_ref[...]).astype(BF16)
    k_idx = proj[:, o_ki:o_wi]
    mu = jnp.mean(k_idx, axis=-1, keepdims=True)
    kc = k_idx - mu
    k_idx = kc * lax.rsqrt(jnp.mean(kc * kc, axis=-1, keepdims=True) + EPS)
    kidx_ref[...] = (k_idx * ikn_ref[...] + ikb_ref[...]).astype(BF16)
    widx_ref[...] = proj[:, o_wi:o_wi + N_IDX_HEADS] * (N_IDX_HEADS ** -0.5 * IDX_DIM ** -0.5)
    q_nope = _dot(c_q, wuq_ref[...])
    q_idx = _dot(c_q, wiq_ref[...])
    qan = qan_ref[...] * KV_RANK_C ** -0.5
    for h in range(N_HEADS_C):
        qh = q_nope[:, h * QK_HEAD_C:(h + 1) * QK_HEAD_C].astype(BF16)
        qa = _rms(_dot(qh, wuk_ref[h]), qan).astype(BF16)
        qi = q_idx[:, h * IDX_DIM:(h + 1) * IDX_DIM].astype(BF16)
        for j in range(qabs_ref.shape[0]):
            qabs_ref[j, h] = qa[j * C_BLOCK:(j + 1) * C_BLOCK]
            qidx_ref[j, h] = qi[j * C_BLOCK:(j + 1) * C_BLOCK]


def dsa_prep(x, g, w_in, q_lat_norm, kv_norm, w_uq, w_uk, q_abs_norm, w_iq, idx_k_norm, idx_k_bias,
             *, tm):
    m, d = x.shape
    nblk = m // C_BLOCK
    bpt = tm // C_BLOCK
    n_in = w_in.shape[1]
    const2 = lambda i: (0, 0)
    return pl.pallas_call(
        _dsa_prep_kernel,
        grid=(m // tm,),
        in_specs=[
            pl.BlockSpec((tm, d), lambda i: (i, 0)),
            pl.BlockSpec((1, d), const2),
            pl.BlockSpec((d, n_in), const2),
            pl.BlockSpec((1, Q_RANK_C), const2),
            pl.BlockSpec((1, KV_RANK_C), const2),
            pl.BlockSpec(w_uq.shape, const2),
            pl.BlockSpec(w_uk.shape, lambda i: (0, 0, 0)),
            pl.BlockSpec((1, KV_RANK_C), const2),
            pl.BlockSpec(w_iq.shape, const2),
            pl.BlockSpec((1, IDX_DIM), const2),
            pl.BlockSpec((1, IDX_DIM), const2),
        ],
        out_specs=[
            pl.BlockSpec((bpt, N_HEADS_C, C_BLOCK, KV_RANK_C), lambda i: (i, 0, 0, 0)),
            pl.BlockSpec((bpt, N_IDX_HEADS, C_BLOCK, IDX_DIM), lambda i: (i, 0, 0, 0)),
            pl.BlockSpec((tm, N_IDX_HEADS), lambda i: (i, 0)),
            pl.BlockSpec((tm, KV_RANK_C), lambda i: (i, 0)),
            pl.BlockSpec((tm, IDX_DIM), lambda i: (i, 0)),
        ],
        out_shape=[
            jax.ShapeDtypeStruct((nblk, N_HEADS_C, C_BLOCK, KV_RANK_C), BF16),
            jax.ShapeDtypeStruct((nblk, N_IDX_HEADS, C_BLOCK, IDX_DIM), BF16),
            jax.ShapeDtypeStruct((m, N_IDX_HEADS), F32),
            jax.ShapeDtypeStruct((m, KV_RANK_C), BF16),
            jax.ShapeDtypeStruct((m, IDX_DIM), BF16),
        ],
        compiler_params=_params("parallel"),
        name="dsa_prep",
    )(x, g.reshape(1, d), w_in, q_lat_norm.reshape(1, -1), kv_norm.reshape(1, -1), w_uq, w_uk,
      q_abs_norm.reshape(1, -1), w_iq, idx_k_norm.reshape(1, -1), idx_k_bias.reshape(1, -1))


def _sortable_key(s):
    bits = pltpu.bitcast(s, jnp.int32)
    return bits ^ ((bits >> 31) & 0x7FFFFFFF)


def _dsa_attn_kernel(qabs_ref, qidx_ref, widx_ref, ckv_ref, kidx_ref, wuv_ref, o_ref,
                     key_ref, m_ref, l_ref, acc_ref, *, topk):
    n = pl.program_id(1)
    tk = KEY_TILE_C
    n_tiles = (n * C_BLOCK + C_BLOCK + tk - 1) // tk
    q_pos = n * C_BLOCK + lax.broadcasted_iota(jnp.int32, (C_BLOCK, tk), 0)
    col = lax.broadcasted_iota(jnp.int32, (C_BLOCK, tk), 1)
    neg_key = _sortable_key(jnp.full((C_BLOCK, tk), NEG_INF, F32))
    widx = widx_ref[0, 0]

    def score_tile(t, carry):
        cs = pl.ds(pl.multiple_of(t * tk, tk), tk)
        kt = kidx_ref[0, cs, :]
        score = jnp.zeros((C_BLOCK, tk), F32)
        for h in range(N_IDX_HEADS):
            rel = jnp.maximum(_nt_dot(qidx_ref[0, h], kt), 0.0)
            score = score + rel * widx[:, h:h + 1]
        key = jnp.where(t * tk + col <= q_pos, _sortable_key(score), neg_key)
        key_ref[:, cs] = key
        return carry

    lax.fori_loop(0, n_tiles, score_tile, 0)

    def count_ge(cand):
        def body(t, cnt):
            cs = pl.ds(pl.multiple_of(t * tk, tk), tk)
            ge = (key_ref[:, cs] >= cand).astype(jnp.int32)
            for c in range(tk // LANES):
                cnt = cnt + ge[:, c * LANES:(c + 1) * LANES]
            return cnt
        cnt = lax.fori_loop(0, n_tiles, body, jnp.zeros((C_BLOCK, LANES), jnp.int32))
        return jnp.sum(cnt, axis=-1, keepdims=True)

    def bisect(i, lo):
        cand = lo + lax.shift_left(jnp.int32(1), 31 - i)
        return jnp.where(count_ge(cand) >= topk, cand, lo)

    thresh = lax.fori_loop(0, 32, bisect, jnp.full((C_BLOCK, 1), INT_MIN, jnp.int32))

    rows = N_HEADS_C * C_BLOCK
    m_ref[...] = jnp.full(m_ref.shape, NEG_INF, F32)
    l_ref[...] = jnp.zeros_like(l_ref)
    acc_ref[...] = jnp.zeros_like(acc_ref)
    q = qabs_ref[0].reshape(rows, KV_RANK_C)

    def attn_tile(t, carry):
        cs = pl.ds(pl.multiple_of(t * tk, tk), tk)
        kv = ckv_ref[0, cs, :]
        sel = (key_ref[:, cs] >= thresh) & (t * tk + col <= q_pos)
        s = _nt_dot(q, kv).reshape(N_HEADS_C, C_BLOCK, tk)
        s = jnp.where(sel[None], s, NEG_INF)
        m_old = m_ref[...]
        m_new = jnp.maximum(m_old, jnp.max(s, axis=-1, keepdims=True))
        p = jnp.where(sel[None], jnp.exp(s - m_new), 0.0)
        alpha = jnp.exp(m_old - m_new)
        l_ref[...] = alpha * l_ref[...] + jnp.sum(p, axis=-1, keepdims=True)
        pv = _dot(p.reshape(rows, tk).astype(BF16), kv).reshape(N_HEADS_C, C_BLOCK, KV_RANK_C)
        acc_ref[...] = alpha * acc_ref[...] + pv
        m_ref[...] = m_new
        return carry

    lax.fori_loop(0, n_tiles, attn_tile, 0)

    o_lat = (acc_ref[...] / l_ref[...]).astype(BF16)
    for h in range(N_HEADS_C):
        o_ref[0, :, h * V_HEAD_C:(h + 1) * V_HEAD_C] = _dot(o_lat[h], wuv_ref[h]).astype(o_ref.dtype)


def dsa_attention(qabs, qidx, widx, ckv, kidx, w_uv, *, batch, topk):
    nblk = qabs.shape[0]
    nb = nblk // batch
    l = nb * C_BLOCK
    l_pad = -(-l // KEY_TILE_C) * KEY_TILE_C
    assert l_pad == l and l >= topk
    ckv = ckv.reshape(batch, l, KV_RANK_C)
    kidx = kidx.reshape(batch, l, IDX_DIM)
    widx = widx.reshape(batch, nb, C_BLOCK, N_IDX_HEADS)
    return pl.pallas_call(
        functools.partial(_dsa_attn_kernel, topk=topk),
        grid=(batch, nb),
        in_specs=[
            pl.BlockSpec((1, N_HEADS_C, C_BLOCK, KV_RANK_C), lambda b, j: (b * nb + j, 0, 0, 0)),
            pl.BlockSpec((1, N_IDX_HEADS, C_BLOCK, IDX_DIM), lambda b, j: (b * nb + j, 0, 0, 0)),
            pl.BlockSpec((1, 1, C_BLOCK, N_IDX_HEADS), lambda b, j: (b, j, 0, 0)),
            pl.BlockSpec((1, l, KV_RANK_C), lambda b, j: (b, 0, 0)),
            pl.BlockSpec((1, l, IDX_DIM), lambda b, j: (b, 0, 0)),
            pl.BlockSpec(w_uv.shape, lambda b, j: (0, 0, 0)),
        ],
        out_specs=pl.BlockSpec((1, C_BLOCK, N_HEADS_C * V_HEAD_C), lambda b, j: (b, j, 0)),
        out_shape=jax.ShapeDtypeStruct((batch, l, N_HEADS_C * V_HEAD_C), BF16),
        scratch_shapes=[
            pltpu.VMEM((C_BLOCK, l), jnp.int32),
            pltpu.VMEM((N_HEADS_C, C_BLOCK, 1), F32),
            pltpu.VMEM((N_HEADS_C, C_BLOCK, 1), F32),
            pltpu.VMEM((N_HEADS_C, C_BLOCK, KV_RANK_C), F32),
        ],
        compiler_params=_params("parallel", "arbitrary"),
        name="dsa_attention",
    )(qabs, qidx, widx, ckv, kidx, w_uv)


TM = 512
TF = 512


def _swa_layer(h, batch, norm_mix, w_in, q_norm, k_norm, sinks, wo):
    m, d = h.shape
    qkv = norm_matmul(h, norm_mix, w_in.astype(BF16), tm=TM, tn=512)
    o = swa_attention(qkv.reshape(batch, m // batch, -1), q_norm, k_norm, sinks)
    return matmul_res(o.reshape(m, -1), wo.astype(BF16), h, tm=TM, tn=512)


def _gla_layer(h, batch, norm_mix, w_in, w_gate_b, gate_bias, o_norm, wo):
    m, d = h.shape
    dk_all = w_gate_b.shape[1]
    dv_all = wo.shape[0]
    n_main = 2 * dk_all + 2 * dv_all
    w_pad = jnp.pad(w_in.astype(BF16), ((0, 0), (0, LANES - GATE_RANK)))
    wgb_pad = jnp.pad(w_gate_b.astype(BF16), ((0, LANES - GATE_RANK), (0, 0)))
    proj = norm_matmul(h, norm_mix, w_pad, tm=TM, tn=(n_main + LANES) // 7)
    o = gla_attention(proj.reshape(batch, m // batch, -1), wgb_pad, gate_bias, o_norm,
                      dk=dk_all // N_HEADS_B, dv=dv_all // N_HEADS_B, tb=256)
    return matmul_res(o.reshape(m, -1), wo.astype(BF16), h, tm=TM, tn=512)


def _dsa_layer(h, batch, norm_mix, w_in, q_lat_norm, kv_norm, w_uq, w_uk, q_abs_norm, w_iq,
               idx_k_norm, idx_k_bias, w_uv, wo):
    m, d = h.shape
    l = m // batch
    qabs, qidx, widx, ckv, kidx = dsa_prep(
        h, norm_mix, w_in.astype(BF16), q_lat_norm, kv_norm, w_uq.astype(BF16), w_uk.astype(BF16),
        q_abs_norm, w_iq.astype(BF16), idx_k_norm, idx_k_bias, tm=TM)
    o = dsa_attention(qabs, qidx, widx, ckv, kidx, w_uv.astype(BF16), batch=batch,
                      topk=min(TOPK_MAX, l // 4))
    return matmul_res(o.reshape(m, -1), wo.astype(BF16), h, tm=TM, tn=512)


def kernel(x, l0_norm_mix, l0_w_in, l0_q_norm, l0_k_norm, l0_sinks, l0_wo, l0_norm_ffn, l0_w_gate_up, l0_w_down, l1_norm_mix, l1_w_in, l1_w_gate_b, l1_gate_bias, l1_o_norm, l1_wo, l1_norm_ffn, l1_w_gate_up, l1_w_down, l2_norm_mix, l2_w_in, l2_q_lat_norm, l2_kv_norm, l2_w_uq, l2_w_uk, l2_q_abs_norm, l2_w_iq, l2_idx_k_norm, l2_idx_k_bias, l2_w_uv, l2_wo, l2_norm_ffn, l2_w_gate_up, l2_w_down, l3_norm_mix, l3_w_in, l3_q_norm, l3_k_norm, l3_sinks, l3_wo, l3_norm_ffn, l3_w_gate_up, l3_w_down):
    batch, seq, d = x.shape
    h = x.reshape(batch * seq, d)

    def channel_mix(h, norm_ffn, w_gate_up, w_down):
        return ffn(h, norm_ffn, w_gate_up.astype(BF16), w_down.astype(BF16), tm=TM, tf=TF)

    h = _swa_layer(h, batch, l0_norm_mix, l0_w_in, l0_q_norm, l0_k_norm, l0_sinks, l0_wo)
    h = channel_mix(h, l0_norm_ffn, l0_w_gate_up, l0_w_down)
    h = _gla_layer(h, batch, l1_norm_mix, l1_w_in, l1_w_gate_b, l1_gate_bias, l1_o_norm, l1_wo)
    h = channel_mix(h, l1_norm_ffn, l1_w_gate_up, l1_w_down)
    h = _dsa_layer(h, batch, l2_norm_mix, l2_w_in, l2_q_lat_norm, l2_kv_norm, l2_w_uq, l2_w_uk,
                   l2_q_abs_norm, l2_w_iq, l2_idx_k_norm, l2_idx_k_bias, l2_w_uv, l2_wo)
    h = channel_mix(h, l2_norm_ffn, l2_w_gate_up, l2_w_down)
    h = _swa_layer(h, batch, l3_norm_mix, l3_w_in, l3_q_norm, l3_k_norm, l3_sinks, l3_wo)
    h = channel_mix(h, l3_norm_ffn, l3_w_gate_up, l3_w_down)
    return h.reshape(batch, seq, d)
```

```python
import functools

import jax
import jax.numpy as jnp
import numpy as np
from jax import lax
from jax.experimental import pallas as pl
from jax.experimental.pallas import tpu as pltpu

F32 = jnp.float32
BF16 = jnp.bfloat16
EPS = 1e-6
NEG_INF = -1e30
LANES = 128

HD_A = 64
N_KV_A = 4
GROUP_A = 8
BLOCK_A = 128
N_HEADS_B = 4
GATE_RANK = 16
GATE_TAU = 16.0
CHUNK = 64
N_HEADS_C = 16
QK_HEAD_C = 128
V_HEAD_C = 128
Q_RANK_C = 512
KV_RANK_C = 256
N_IDX_HEADS = 16
IDX_DIM = 64
TOPK_MAX = 256
C_BLOCK = 128
KEY_TILE_C = 512
INT_MIN = -2 ** 31
_NEG_BITS = int(np.float32(NEG_INF).view(np.int32))
NEG_KEY = _NEG_BITS ^ ((_NEG_BITS >> 31) & 0x7FFFFFFF)


def _nt_dot(a, b):
    return lax.dot_general(a, b, (((1,), (1,)), ((), ())), preferred_element_type=F32)


def _tn_dot(a, b):
    return lax.dot_general(a, b, (((0,), (0,)), ((), ())), preferred_element_type=F32)


def _dot(a, b):
    return jnp.dot(a, b, preferred_element_type=F32)


def _rms(x, g):
    return x * lax.rsqrt(jnp.mean(x * x, axis=-1, keepdims=True) + EPS) * g


def _params(*sem):
    return pltpu.CompilerParams(dimension_semantics=sem)


def _norm_matmul_kernel(x_ref, g_ref, w_ref, o_ref, xn_ref):
    @pl.when(pl.program_id(1) == 0)
    def _():
        xn_ref[...] = _rms(x_ref[...], g_ref[...]).astype(BF16)

    o_ref[...] = _dot(xn_ref[...], w_ref[...]).astype(o_ref.dtype)


def norm_matmul(x, g, w, *, tm, tn, out_dtype=F32):
    m, d = x.shape
    n = w.shape[1]
    return pl.pallas_call(
        _norm_matmul_kernel,
        grid=(m // tm, n // tn),
        in_specs=[
            pl.BlockSpec((tm, d), lambda i, j: (i, 0)),
            pl.BlockSpec((1, d), lambda i, j: (0, 0)),
            pl.BlockSpec((d, tn), lambda i, j: (0, j)),
        ],
        out_specs=pl.BlockSpec((tm, tn), lambda i, j: (i, j)),
        out_shape=jax.ShapeDtypeStruct((m, n), out_dtype),
        scratch_shapes=[pltpu.VMEM((tm, d), BF16)],
        compiler_params=_params("parallel", "arbitrary"),
        name="norm_matmul",
    )(x, g.reshape(1, d), w)


def _matmul_res_kernel(a_ref, w_ref, r_ref, o_ref):
    o_ref[...] = r_ref[...] + _dot(a_ref[...], w_ref[...])


def matmul_res(a, w, r, *, tm, tn):
    m, k = a.shape
    n = w.shape[1]
    return pl.pallas_call(
        _matmul_res_kernel,
        grid=(m // tm, n // tn),
        in_specs=[
            pl.BlockSpec((tm, k), lambda i, j: (i, 0)),
            pl.BlockSpec((k, tn), lambda i, j: (0, j)),
            pl.BlockSpec((tm, tn), lambda i, j: (i, j)),
        ],
        out_specs=pl.BlockSpec((tm, tn), lambda i, j: (i, j)),
        out_shape=jax.ShapeDtypeStruct((m, n), F32),
        compiler_params=_params("parallel", "arbitrary"),
        name="matmul_res",
    )(a, w, r)


def _ffn_kernel(x_ref, g_ref, wg_ref, wu_ref, wd_ref, o_ref, xn_ref):
    @pl.when(pl.program_id(1) == 0)
    def _():
        x = x_ref[...]
        xn_ref[...] = _rms(x, g_ref[...]).astype(BF16)
        o_ref[...] = x

    xn = xn_ref[...]
    gate = _dot(xn, wg_ref[...])
    up = _dot(xn, wu_ref[...])
    act = (gate * (1.0 / (1.0 + jnp.exp(-gate))) * up).astype(BF16)
    o_ref[...] += _dot(act, wd_ref[...])


def ffn(x, g, w_gate_up, w_down, *, tm, tf):
    m, d = x.shape
    ff = w_down.shape[0]
    nf = ff // tf
    return pl.pallas_call(
        _ffn_kernel,
        grid=(m // tm, nf),
        in_specs=[
            pl.BlockSpec((tm, d), lambda i, j: (i, 0)),
            pl.BlockSpec((1, d), lambda i, j: (0, 0)),
            pl.BlockSpec((d, tf), lambda i, j: (0, j)),
            pl.BlockSpec((d, tf), lambda i, j: (0, j + nf)),
            pl.BlockSpec((tf, d), lambda i, j: (j, 0)),
        ],
        out_specs=pl.BlockSpec((tm, d), lambda i, j: (i, 0)),
        out_shape=jax.ShapeDtypeStruct((m, d), F32),
        scratch_shapes=[pltpu.VMEM((tm, d), BF16)],
        compiler_params=_params("parallel", "arbitrary"),
        name="ffn",
    )(x, g.reshape(1, d), w_gate_up, w_gate_up, w_down)


def _swa_kernel(sink_ref, cur_ref, prev_ref, qn_ref, kn_ref, sel_ref, o_ref):
    n = pl.program_id(1)
    nq = N_KV_A * GROUP_A * HD_A
    nkv = N_KV_A * HD_A
    kj = lax.broadcasted_iota(jnp.int32, (2 * BLOCK_A, BLOCK_A), 0)
    qi = lax.broadcasted_iota(jnp.int32, (2 * BLOCK_A, BLOCK_A), 1)
    mask = (kj > qi) & (kj <= qi + BLOCK_A) & ((n > 0) | (kj >= BLOCK_A))
    xq = cur_ref[0, :, :nq]
    sq = xq * xq
    sq_hi = sq.astype(BF16)
    sq_lo = (sq - sq_hi.astype(F32)).astype(BF16)
    ssq = _dot(sq_hi, sel_ref[...]) + _dot(sq_lo, sel_ref[...])
    rq_t = lax.rsqrt(ssq * (1.0 / HD_A) + EPS).T
    q_bf = xq.astype(BF16)
    kscale = qn_ref[...] * kn_ref[...] * HD_A ** -0.5
    zeros = jnp.zeros((2 * BLOCK_A, HD_A), F32)
    v_all = jnp.concatenate([prev_ref[0, :, nkv:2 * nkv], cur_ref[0, :, nq + nkv:nq + 2 * nkv]], axis=0)
    v_t = v_all.T.astype(BF16)
    for h in range(N_KV_A):
        k = jnp.concatenate([prev_ref[0, :, h * HD_A:(h + 1) * HD_A],
                             cur_ref[0, :, nq + h * HD_A:nq + (h + 1) * HD_A]], axis=0)
        kn = k * lax.rsqrt(jnp.mean(k * k, axis=-1, keepdims=True) + EPS) * kscale
        k_pad = (jnp.concatenate([kn, zeros], axis=1).astype(BF16),
                 jnp.concatenate([zeros, kn], axis=1).astype(BF16))
        vt = v_t[h * HD_A:(h + 1) * HD_A, :]
        for pair in range(GROUP_A // 2):
            c0 = (h * GROUP_A + 2 * pair) * HD_A
            q_tile = q_bf[:, c0:c0 + 2 * HD_A]
            outs = []
            for par in range(2):
                j = h * GROUP_A + 2 * pair + par
                s = _nt_dot(k_pad[par], q_tile) * rq_t[j:j + 1, :]
                s = jnp.where(mask, s, NEG_INF)
                sink = sink_ref[j]
                mx = jnp.maximum(jnp.max(s, axis=0, keepdims=True), sink)
                p = jnp.exp(s - mx)
                den = jnp.sum(p, axis=0, keepdims=True) + jnp.exp(sink - mx)
                outs.append(_dot(vt, p.astype(BF16)) / den)
            o_ref[0, :, c0:c0 + 2 * HD_A] = jnp.concatenate(outs, axis=0).T.astype(o_ref.dtype)


def swa_attention(qkv, q_norm, k_norm, sinks):
    b, l, n = qkv.shape
    nq = N_KV_A * GROUP_A * HD_A
    kv_w = 2 * N_KV_A * HD_A
    head_sel = (jnp.arange(nq)[:, None] // HD_A == jnp.arange(LANES)[None, :]).astype(BF16)
    return pl.pallas_call(
        _swa_kernel,
        grid=(b, l // BLOCK_A),
        in_specs=[
            pl.BlockSpec(memory_space=pltpu.SMEM),
            pl.BlockSpec((1, BLOCK_A, n), lambda i, j: (i, j, 0)),
            pl.BlockSpec((1, BLOCK_A, kv_w), lambda i, j: (i, jnp.maximum(j - 1, 0), nq // kv_w)),
            pl.BlockSpec((1, HD_A), lambda i, j: (0, 0)),
            pl.BlockSpec((1, HD_A), lambda i, j: (0, 0)),
            pl.BlockSpec((nq, LANES), lambda i, j: (0, 0)),
        ],
        out_specs=pl.BlockSpec((1, BLOCK_A, nq), lambda i, j: (i, j, 0)),
        out_shape=jax.ShapeDtypeStruct((b, l, nq), BF16),
        compiler_params=_params("parallel", "parallel"),
        name="swa_attention",
    )(sinks, qkv, qkv, q_norm.reshape(1, HD_A), k_norm.reshape(1, HD_A), head_sel)


def _split3(x):
    hi = x.astype(BF16)
    r1 = x - hi.astype(F32)
    mid = r1.astype(BF16)
    lo = (r1 - mid.astype(F32)).astype(BF16)
    return hi, mid, lo


def _gla_kernel(q_ref, k_ref, v_ref, r_ref, gl_ref, wgb_ref, gb_ref, on_ref, o_ref, st_ref,
                *, tb, dk, dv):
    @pl.when(pl.program_id(1) == 0)
    def _():
        st_ref[...] = jnp.zeros_like(st_ref)

    ti = lax.broadcasted_iota(jnp.int32, (CHUNK, CHUNK), 0)
    si = lax.broadcasted_iota(jnp.int32, (CHUNK, CHUNK), 1)
    causal = si <= ti
    tri = causal.astype(BF16)
    wgb = wgb_ref[...]
    gb = gb_ref[...]
    on = on_ref[...]

    def chunk(c, carry):
        rs = pl.ds(pl.multiple_of(c * CHUNK, CHUNK), CHUNK)
        gate = _dot(gl_ref[0, rs, :].astype(BF16), wgb) + gb
        log_a = (jnp.minimum(gate, 0.0) - jnp.log1p(jnp.exp(-jnp.abs(gate)))) / GATE_TAU
        hi, mid, lo = _split3(log_a)
        bcum = _dot(tri, hi) + _dot(tri, mid) + _dot(tri, lo)
        b_last = bcum[CHUNK - 1:CHUNK, :]
        q = q_ref[0, rs, :]
        k = k_ref[0, rs, :]
        q_dec = (q * dk ** -0.5 * jnp.exp(bcum)).astype(BF16)
        k_inv = (k * jnp.exp(-bcum)).astype(BF16)
        k_end = (k * jnp.exp(b_last - bcum)).astype(BF16)
        decay = jnp.exp(b_last)
        for h in range(N_HEADS_B):
            ks = slice(h * dk, (h + 1) * dk)
            vs = slice(h * dv, (h + 1) * dv)
            v = v_ref[0, rs, vs].astype(BF16)
            a = jnp.where(causal, _nt_dot(q_dec[:, ks], k_inv[:, ks]), 0.0)
            st = st_ref[h]
            o = _dot(a.astype(BF16), v) + _nt_dot(q_dec[:, ks], st.astype(BF16))
            st_ref[h] = st * decay[:, ks] + _tn_dot(v, k_end[:, ks])
            o = _rms(o, on)
            r = r_ref[0, rs, vs]
            o_ref[0, rs, vs] = (o * (r * (1.0 / (1.0 + jnp.exp(-r))))).astype(o_ref.dtype)
        return carry

    lax.fori_loop(0, tb // CHUNK, chunk, 0)


def gla_attention(proj, w_gate_b, gate_bias, o_norm, *, dk, dv, tb):
    b, l, _ = proj.shape
    dk_all, dv_all = N_HEADS_B * dk, N_HEADS_B * dv
    return pl.pallas_call(
        functools.partial(_gla_kernel, tb=tb, dk=dk, dv=dv),
        grid=(b, l // tb),
        in_specs=[
            pl.BlockSpec((1, tb, dk_all), lambda i, j: (i, j, 0)),
            pl.BlockSpec((1, tb, dk_all), lambda i, j: (i, j, 1)),
            pl.BlockSpec((1, tb, dv_all), lambda i, j: (i, j, 2 * dk_all // dv_all)),
            pl.BlockSpec((1, tb, dv_all), lambda i, j: (i, j, 2 * dk_all // dv_all + 1)),
            pl.BlockSpec((1, tb, LANES), lambda i, j: (i, j, (2 * dk_all + 2 * dv_all) // LANES)),
            pl.BlockSpec((LANES, dk_all), lambda i, j: (0, 0)),
            pl.BlockSpec((1, dk_all), lambda i, j: (0, 0)),
            pl.BlockSpec((1, dv), lambda i, j: (0, 0)),
        ],
        out_specs=pl.BlockSpec((1, tb, dv_all), lambda i, j: (i, j, 0)),
        out_shape=jax.ShapeDtypeStruct((b, l, dv_all), BF16),
        scratch_shapes=[pltpu.VMEM((N_HEADS_B, dv, dk), F32)],
        compiler_params=_params("parallel", "arbitrary"),
        name="gla_attention",
    )(proj, proj, proj, proj, proj, w_gate_b, gate_bias.reshape(1, dk_all), o_norm.reshape(1, dv))


def _dsa_prep_kernel(x_ref, g_ref, win_ref, qln_ref, kvn_ref, wuq_ref, wuk_ref, qan_ref, wiq_ref,
                     ikn_ref, ikb_ref, qabs_ref, qidx_ref, widx_ref, ckv_ref, ckvt_ref, kidx_ref):
    xn = _rms(x_ref[...], g_ref[...]).astype(BF16)
    proj = _dot(xn, win_ref[...])
    o_kv = Q_RANK_C
    o_ki = Q_RANK_C + KV_RANK_C
    o_wi = o_ki + IDX_DIM
    c_q = _rms(proj[:, :o_kv], qln_ref[...]).astype(BF16)
    c_kv = _rms(proj[:, o_kv:o_ki], kvn_ref[...])
    ckv_ref[...] = c_kv.astype(BF16)
    ckvt_ref[...] = c_kv.T.astype(BF16)
    k_idx = proj[:, o_ki:o_wi]
    mu = jnp.mean(k_idx, axis=-1, keepdims=True)
    kc = k_idx - mu
    k_idx = kc * lax.rsqrt(jnp.mean(kc * kc, axis=-1, keepdims=True) + EPS)
    kidx_ref[...] = (k_idx * ikn_ref[...] + ikb_ref[...]).astype(BF16)
    widx_ref[...] = proj[:, o_wi:o_wi + N_IDX_HEADS] * (N_IDX_HEADS ** -0.5 * IDX_DIM ** -0.5)
    q_nope = _dot(c_q, wuq_ref[...])
    q_idx = _dot(c_q, wiq_ref[...])
    qan = qan_ref[...] * KV_RANK_C ** -0.5
    for h in range(N_HEADS_C):
        qh = q_nope[:, h * QK_HEAD_C:(h + 1) * QK_HEAD_C].astype(BF16)
        qa = _rms(_dot(qh, wuk_ref[h]), qan).astype(BF16)
        qi = q_idx[:, h * IDX_DIM:(h + 1) * IDX_DIM].astype(BF16)
        for j in range(qabs_ref.shape[0]):
            qabs_ref[j, h] = qa[j * C_BLOCK:(j + 1) * C_BLOCK]
            qidx_ref[j, h] = qi[j * C_BLOCK:(j + 1) * C_BLOCK]


def dsa_prep(x, g, w_in, q_lat_norm, kv_norm, w_uq, w_uk, q_abs_norm, w_iq, idx_k_norm, idx_k_bias,
             *, tm):
    m, d = x.shape
    nblk = m // C_BLOCK
    bpt = tm // C_BLOCK
    n_in = w_in.shape[1]
    const2 = lambda i: (0, 0)
    return pl.pallas_call(
        _dsa_prep_kernel,
        grid=(m // tm,),
        in_specs=[
            pl.BlockSpec((tm, d), lambda i: (i, 0)),
            pl.BlockSpec((1, d), const2),
            pl.BlockSpec((d, n_in), const2),
            pl.BlockSpec((1, Q_RANK_C), const2),
            pl.BlockSpec((1, KV_RANK_C), const2),
            pl.BlockSpec(w_uq.shape, const2),
            pl.BlockSpec(w_uk.shape, lambda i: (0, 0, 0)),
            pl.BlockSpec((1, KV_RANK_C), const2),
            pl.BlockSpec(w_iq.shape, const2),
            pl.BlockSpec((1, IDX_DIM), const2),
            pl.BlockSpec((1, IDX_DIM), const2),
        ],
        out_specs=[
            pl.BlockSpec((bpt, N_HEADS_C, C_BLOCK, KV_RANK_C), lambda i: (i, 0, 0, 0)),
            pl.BlockSpec((bpt, N_IDX_HEADS, C_BLOCK, IDX_DIM), lambda i: (i, 0, 0, 0)),
            pl.BlockSpec((tm, N_IDX_HEADS), lambda i: (i, 0)),
            pl.BlockSpec((tm, KV_RANK_C), lambda i: (i, 0)),
            pl.BlockSpec((KV_RANK_C, tm), lambda i: (0, i)),
            pl.BlockSpec((tm, IDX_DIM), lambda i: (i, 0)),
        ],
        out_shape=[
            jax.ShapeDtypeStruct((nblk, N_HEADS_C, C_BLOCK, KV_RANK_C), BF16),
            jax.ShapeDtypeStruct((nblk, N_IDX_HEADS, C_BLOCK, IDX_DIM), BF16),
            jax.ShapeDtypeStruct((m, N_IDX_HEADS), F32),
            jax.ShapeDtypeStruct((m, KV_RANK_C), BF16),
            jax.ShapeDtypeStruct((KV_RANK_C, m), BF16),
            jax.ShapeDtypeStruct((m, IDX_DIM), BF16),
        ],
        compiler_params=_params("parallel"),
        name="dsa_prep",
    )(x, g.reshape(1, d), w_in, q_lat_norm.reshape(1, -1), kv_norm.reshape(1, -1), w_uq, w_uk,
      q_abs_norm.reshape(1, -1), w_iq, idx_k_norm.reshape(1, -1), idx_k_bias.reshape(1, -1))


def _sortable_key(s):
    bits = pltpu.bitcast(s, jnp.int32)
    return bits ^ ((bits >> 31) & 0x7FFFFFFF)


def _dsa_attn_kernel(qabs_ref, qidx_ref, widx_ref, ckv_ref, ckvt_ref, kidx_ref, wuv_ref, o_ref,
                     key_ref, m_ref, l_ref, acc_ref, *, topk, idx_bits):
    n = pl.program_id(1)
    tk = KEY_TILE_C
    n_tiles = (n * C_BLOCK + C_BLOCK + tk - 1) // tk
    q_row = n * C_BLOCK + lax.broadcasted_iota(jnp.int32, (C_BLOCK, tk), 0)
    k_col = lax.broadcasted_iota(jnp.int32, (C_BLOCK, tk), 1)
    k_row = lax.broadcasted_iota(jnp.int32, (tk, C_BLOCK), 0)
    widx = widx_ref[0, 0]

    def tile(t):
        return pl.ds(pl.multiple_of(t * tk, tk), tk)

    def score_tile(t, carry):
        kt = kidx_ref[0, tile(t), :]
        score = jnp.zeros((C_BLOCK, tk), F32)
        for h in range(N_IDX_HEADS):
            rel = jnp.maximum(_nt_dot(qidx_ref[0, h], kt), 0.0)
            score = score + rel * widx[:, h:h + 1]
        score = jnp.where(t * tk + k_col <= q_row, score, NEG_INF)
        key_ref[tile(t), :] = _sortable_key(score.T)
        return carry

    lax.fori_loop(0, n_tiles, score_tile, 0)

    def count(pred):
        def body(t, cnt):
            hit = pred(key_ref[tile(t), :], t).astype(jnp.int32)
            return cnt + jnp.sum(hit.reshape(tk // 8, 8, C_BLOCK), axis=0)
        cnt = lax.fori_loop(0, n_tiles, body, jnp.zeros((8, C_BLOCK), jnp.int32))
        return jnp.sum(cnt, axis=0, keepdims=True)

    def bisect(i, lo):
        cand = lo + lax.shift_left(jnp.int32(1), 31 - i)
        return jnp.where(count(lambda k, t: k >= cand) >= topk, cand, lo)

    thresh = lax.fori_loop(0, 32, bisect, jnp.full((1, C_BLOCK), INT_MIN, jnp.int32))

    n_ge = count(lambda k, t: k >= thresh)
    excess = jnp.max(jnp.where((n_ge > topk) & (thresh > NEG_KEY), 1, 0))

    @pl.when(excess > 0)
    def _():
        want = topk - count(lambda k, t: k > thresh)

        def index_bisect(i, j0):
            cand = j0 + lax.shift_left(jnp.int32(1), idx_bits - 1 - i)
            before = count(lambda k, t: (k == thresh) & (t * tk + k_row < cand))
            return jnp.where(before < want, cand, j0)

        j0 = lax.fori_loop(0, idx_bits, index_bisect, jnp.zeros((1, C_BLOCK), jnp.int32))

        def demote(t, carry):
            keys = key_ref[tile(t), :]
            drop = (keys == thresh) & (t * tk + k_row > j0)
            key_ref[tile(t), :] = jnp.where(drop, thresh - 1, keys)
            return carry

        lax.fori_loop(0, n_tiles, demote, 0)

    thr = jnp.maximum(thresh, NEG_KEY + 1)
    thr2 = jnp.concatenate([thr, thr], axis=1)
    m_ref[...] = jnp.full(m_ref.shape, NEG_INF, F32)
    l_ref[...] = jnp.zeros_like(l_ref)
    acc_ref[...] = jnp.zeros_like(acc_ref)

    def attn_tile(t, carry):
        kv = ckv_ref[0, tile(t), :]
        kvt = ckvt_ref[:, tile(t)]
        keys = key_ref[tile(t), :]
        sel = jnp.concatenate([keys, keys], axis=1) >= thr2
        for p in range(N_HEADS_C // 2):
            q = qabs_ref[0, 2 * p:2 * p + 2].reshape(2 * C_BLOCK, KV_RANK_C)
            s = jnp.where(sel, _nt_dot(kv, q), NEG_INF)
            m_old = m_ref[p]
            m_new = jnp.maximum(m_old, jnp.max(s, axis=0, keepdims=True))
            pe = jnp.exp(s - m_new)
            alpha = jnp.exp(m_old - m_new)
            l_ref[p] = alpha * l_ref[p] + jnp.sum(pe, axis=0, keepdims=True)
            acc_ref[p] = alpha * acc_ref[p] + _dot(kvt, pe.astype(BF16))
            m_ref[p] = m_new
        return carry

    lax.fori_loop(0, n_tiles, attn_tile, 0)

    for p in range(N_HEADS_C // 2):
        o_lat_t = (acc_ref[p] / l_ref[p]).astype(BF16)
        for e in range(2):
            h = 2 * p + e
            o_ref[0, :, h * V_HEAD_C:(h + 1) * V_HEAD_C] = _tn_dot(
                o_lat_t[:, e * C_BLOCK:(e + 1) * C_BLOCK], wuv_ref[h]).astype(o_ref.dtype)


def dsa_attention(qabs, qidx, widx, ckv, ckvt, kidx, w_uv, *, batch, topk):
    nblk = qabs.shape[0]
    nb = nblk // batch
    l = nb * C_BLOCK
    assert l % KEY_TILE_C == 0 and l >= topk
    ckv = ckv.reshape(batch, l, KV_RANK_C)
    kidx = kidx.reshape(batch, l, IDX_DIM)
    widx = widx.reshape(batch, nb, C_BLOCK, N_IDX_HEADS)
    return pl.pallas_call(
        functools.partial(_dsa_attn_kernel, topk=topk, idx_bits=(l - 1).bit_length()),
        grid=(batch, nb),
        in_specs=[
            pl.BlockSpec((1, N_HEADS_C, C_BLOCK, KV_RANK_C), lambda b, j: (b * nb + j, 0, 0, 0)),
            pl.BlockSpec((1, N_IDX_HEADS, C_BLOCK, IDX_DIM), lambda b, j: (b * nb + j, 0, 0, 0)),
            pl.BlockSpec((1, 1, C_BLOCK, N_IDX_HEADS), lambda b, j: (b, j, 0, 0)),
            pl.BlockSpec((1, l, KV_RANK_C), lambda b, j: (b, 0, 0)),
            pl.BlockSpec((KV_RANK_C, l), lambda b, j: (0, b)),
            pl.BlockSpec((1, l, IDX_DIM), lambda b, j: (b, 0, 0)),
            pl.BlockSpec(w_uv.shape, lambda b, j: (0, 0, 0)),
        ],
        out_specs=pl.BlockSpec((1, C_BLOCK, N_HEADS_C * V_HEAD_C), lambda b, j: (b, j, 0)),
        out_shape=jax.ShapeDtypeStruct((batch, l, N_HEADS_C * V_HEAD_C), BF16),
        scratch_shapes=[
            pltpu.VMEM((l, C_BLOCK), jnp.int32),
            pltpu.VMEM((N_HEADS_C // 2, 1, 2 * C_BLOCK), F32),
            pltpu.VMEM((N_HEADS_C // 2, 1, 2 * C_BLOCK), F32),
            pltpu.VMEM((N_HEADS_C // 2, KV_RANK_C, 2 * C_BLOCK), F32),
        ],
        compiler_params=_params("parallel", "arbitrary"),
        name="dsa_attention",
    )(qabs, qidx, widx, ckv, ckvt, kidx, w_uv)


TM = 512
TF = 512


def _swa_layer(h, batch, norm_mix, w_in, q_norm, k_norm, sinks, wo):
    m, d = h.shape
    qkv = norm_matmul(h, norm_mix, w_in.astype(BF16), tm=TM, tn=512)
    o = swa_attention(qkv.reshape(batch, m // batch, -1), q_norm, k_norm, sinks)
    return matmul_res(o.reshape(m, -1), wo.astype(BF16), h, tm=TM, tn=512)


def _gla_layer(h, batch, norm_mix, w_in, w_gate_b, gate_bias, o_norm, wo):
    m, d = h.shape
    dk_all = w_gate_b.shape[1]
    dv_all = wo.shape[0]
    n_main = 2 * dk_all + 2 * dv_all
    w_pad = jnp.pad(w_in.astype(BF16), ((0, 0), (0, LANES - GATE_RANK)))
    wgb_pad = jnp.pad(w_gate_b.astype(BF16), ((0, LANES - GATE_RANK), (0, 0)))
    proj = norm_matmul(h, norm_mix, w_pad, tm=TM, tn=(n_main + LANES) // 7)
    o = gla_attention(proj.reshape(batch, m // batch, -1), wgb_pad, gate_bias, o_norm,
                      dk=dk_all // N_HEADS_B, dv=dv_all // N_HEADS_B, tb=256)
    return matmul_res(o.reshape(m, -1), wo.astype(BF16), h, tm=TM, tn=512)


def _dsa_layer(h, batch, norm_mix, w_in, q_lat_norm, kv_norm, w_uq, w_uk, q_abs_norm, w_iq,
               idx_k_norm, idx_k_bias, w_uv, wo):
    m, d = h.shape
    l = m // batch
    qabs, qidx, widx, ckv, ckvt, kidx = dsa_prep(
        h, norm_mix, w_in.astype(BF16), q_lat_norm, kv_norm, w_uq.astype(BF16), w_uk.astype(BF16),
        q_abs_norm, w_iq.astype(BF16), idx_k_norm, idx_k_bias, tm=TM)
    o = dsa_attention(qabs, qidx, widx, ckv, ckvt, kidx, w_uv.astype(BF16), batch=batch,
                      topk=min(TOPK_MAX, l // 4))
    return matmul_res(o.reshape(m, -1), wo.astype(BF16), h, tm=TM, tn=512)


def kernel(x, l0_norm_mix, l0_w_in, l0_q_norm, l0_k_norm, l0_sinks, l0_wo, l0_norm_ffn, l0_w_gate_up, l0_w_down, l1_norm_mix, l1_w_in, l1_w_gate_b, l1_gate_bias, l1_o_norm, l1_wo, l1_norm_ffn, l1_w_gate_up, l1_w_down, l2_norm_mix, l2_w_in, l2_q_lat_norm, l2_kv_norm, l2_w_uq, l2_w_uk, l2_q_abs_norm, l2_w_iq, l2_idx_k_norm, l2_idx_k_bias, l2_w_uv, l2_wo, l2_norm_ffn, l2_w_gate_up, l2_w_down, l3_norm_mix, l3_w_in, l3_q_norm, l3_k_norm, l3_sinks, l3_wo, l3_norm_ffn, l3_w_gate_up, l3_w_down):
    batch, seq, d = x.shape
    h = x.reshape(batch * seq, d)

    def channel_mix(h, norm_ffn, w_gate_up, w_down):
        return ffn(h, norm_ffn, w_gate_up.astype(BF16), w_down.astype(BF16), tm=TM, tf=TF)

    h = _swa_layer(h, batch, l0_norm_mix, l0_w_in, l0_q_norm, l0_k_norm, l0_sinks, l0_wo)
    h = channel_mix(h, l0_norm_ffn, l0_w_gate_up, l0_w_down)
    h = _gla_layer(h, batch, l1_norm_mix, l1_w_in, l1_w_gate_b, l1_gate_bias, l1_o_norm, l1_wo)
    h = channel_mix(h, l1_norm_ffn, l1_w_gate_up, l1_w_down)
    h = _dsa_layer(h, batch, l2_norm_mix, l2_w_in, l2_q_lat_norm, l2_kv_norm, l2_w_uq, l2_w_uk,
                   l2_q_abs_norm, l2_w_iq, l2_idx_k_norm, l2_idx_k_bias, l2_w_uv, l2_wo)
    h = channel_mix(h, l2_norm_ffn, l2_w_gate_up, l2_w_down)
    h = _swa_layer(h, batch, l3_norm_mix, l3_w_in, l3_q_norm, l3_k_norm, l3_sinks, l3_wo)
    h = channel_mix(h, l3_norm_ffn, l3_w_gate_up, l3_w_down)
    return h.reshape(batch, seq, d)
```

```python
import functools

import jax
import jax.numpy as jnp
import numpy as np
from jax import lax
from jax.experimental import pallas as pl
from jax.experimental.pallas import tpu as pltpu

F32 = jnp.float32
BF16 = jnp.bfloat16
EPS = 1e-6
NEG_INF = -1e30
LANES = 128

HD_A = 64
N_KV_A = 4
GROUP_A = 8
BLOCK_A = 128
N_HEADS_B = 4
GATE_RANK = 16
GATE_TAU = 16.0
CHUNK = 64
N_HEADS_C = 16
QK_HEAD_C = 128
V_HEAD_C = 128
Q_RANK_C = 512
KV_RANK_C = 256
N_IDX_HEADS = 16
IDX_DIM = 64
TOPK_MAX = 256
C_BLOCK = 128
KEY_TILE_C = 512
ATTN_AHEAD_C = 2
ONES_ROWS_C = 16
INT_MIN = -2 ** 31
LOG2E = 1.4426950408889634
_NEG_BITS = int(np.float32(NEG_INF).view(np.int32))
NEG_KEY = _NEG_BITS ^ ((_NEG_BITS >> 31) & 0x7FFFFFFF)


def _nt_dot(a, b):
    return lax.dot_general(a, b, (((1,), (1,)), ((), ())), preferred_element_type=F32)


def _tn_dot(a, b):
    return lax.dot_general(a, b, (((0,), (0,)), ((), ())), preferred_element_type=F32)


def _dot(a, b):
    return jnp.dot(a, b, preferred_element_type=F32)


def _rms(x, g):
    return x * lax.rsqrt(jnp.mean(x * x, axis=-1, keepdims=True) + EPS) * g


def _params(*sem):
    return pltpu.CompilerParams(dimension_semantics=sem)


def _norm_matmul_kernel(x_ref, g_ref, w_ref, o_ref, xn_ref):
    @pl.when(pl.program_id(1) == 0)
    def _():
        xn_ref[...] = _rms(x_ref[...], g_ref[...]).astype(BF16)

    o_ref[...] = _dot(xn_ref[...], w_ref[...]).astype(o_ref.dtype)


def norm_matmul(x, g, w, *, tm, tn, out_dtype=F32):
    m, d = x.shape
    n = w.shape[1]
    return pl.pallas_call(
        _norm_matmul_kernel,
        grid=(m // tm, n // tn),
        in_specs=[
            pl.BlockSpec((tm, d), lambda i, j: (i, 0)),
            pl.BlockSpec((1, d), lambda i, j: (0, 0)),
            pl.BlockSpec((d, tn), lambda i, j: (0, j)),
        ],
        out_specs=pl.BlockSpec((tm, tn), lambda i, j: (i, j)),
        out_shape=jax.ShapeDtypeStruct((m, n), out_dtype),
        scratch_shapes=[pltpu.VMEM((tm, d), BF16)],
        compiler_params=_params("parallel", "arbitrary"),
        name="norm_matmul",
    )(x, g.reshape(1, d), w)


def _matmul_res_kernel(a_ref, w_ref, r_ref, o_ref):
    o_ref[...] = r_ref[...] + _dot(a_ref[...], w_ref[...])


def matmul_res(a, w, r, *, tm, tn):
    m, k = a.shape
    n = w.shape[1]
    return pl.pallas_call(
        _matmul_res_kernel,
        grid=(m // tm, n // tn),
        in_specs=[
            pl.BlockSpec((tm, k), lambda i, j: (i, 0)),
            pl.BlockSpec((k, tn), lambda i, j: (0, j)),
            pl.BlockSpec((tm, tn), lambda i, j: (i, j)),
        ],
        out_specs=pl.BlockSpec((tm, tn), lambda i, j: (i, j)),
        out_shape=jax.ShapeDtypeStruct((m, n), F32),
        compiler_params=_params("parallel", "arbitrary"),
        name="matmul_res",
    )(a, w, r)


def _ffn_kernel(x_ref, g_ref, wg_ref, wu_ref, wd_ref, o_ref, xn_ref):
    @pl.when(pl.program_id(1) == 0)
    def _():
        x = x_ref[...]
        xn_ref[...] = _rms(x, g_ref[...]).astype(BF16)
        o_ref[...] = x

    xn = xn_ref[...]
    gate = _dot(xn, wg_ref[...])
    up = _dot(xn, wu_ref[...])
    act = (gate * (1.0 / (1.0 + jnp.exp(-gate))) * up).astype(BF16)
    o_ref[...] += _dot(act, wd_ref[...])


def ffn(x, g, w_gate_up, w_down, *, tm, tf):
    m, d = x.shape
    ff = w_down.shape[0]
    nf = ff // tf
    return pl.pallas_call(
        _ffn_kernel,
        grid=(m // tm, nf),
        in_specs=[
            pl.BlockSpec((tm, d), lambda i, j: (i, 0)),
            pl.BlockSpec((1, d), lambda i, j: (0, 0)),
            pl.BlockSpec((d, tf), lambda i, j: (0, j)),
            pl.BlockSpec((d, tf), lambda i, j: (0, j + nf)),
            pl.BlockSpec((tf, d), lambda i, j: (j, 0)),
        ],
        out_specs=pl.BlockSpec((tm, d), lambda i, j: (i, 0)),
        out_shape=jax.ShapeDtypeStruct((m, d), F32),
        scratch_shapes=[pltpu.VMEM((tm, d), BF16)],
        compiler_params=_params("parallel", "arbitrary"),
        name="ffn",
    )(x, g.reshape(1, d), w_gate_up, w_gate_up, w_down)


def _swa_kernel(sink_ref, cur_ref, prev_ref, qn_ref, kn_ref, sel_ref, o_ref):
    n = pl.program_id(1)
    nq = N_KV_A * GROUP_A * HD_A
    nkv = N_KV_A * HD_A
    kj = lax.broadcasted_iota(jnp.int32, (2 * BLOCK_A, 2 * BLOCK_A), 0)
    qi = lax.broadcasted_iota(jnp.int32, (2 * BLOCK_A, 2 * BLOCK_A), 1) & (BLOCK_A - 1)
    mask2 = (kj > qi) & (kj <= qi + BLOCK_A) & ((n > 0) | (kj >= BLOCK_A))
    first_half = lax.broadcasted_iota(jnp.int32, (1, 2 * BLOCK_A), 1) < BLOCK_A
    xq = cur_ref[0, :, :nq]
    sq = xq * xq
    sq_hi = sq.astype(BF16)
    sq_lo = (sq - sq_hi.astype(F32)).astype(BF16)
    ssq = _dot(sq_hi, sel_ref[...]) + _dot(sq_lo, sel_ref[...])
    rq_t = (lax.rsqrt(ssq * (1.0 / HD_A) + EPS) * LOG2E).T
    q_bf = xq.astype(BF16)
    kscale = qn_ref[...] * kn_ref[...] * HD_A ** -0.5
    zeros = jnp.zeros((2 * BLOCK_A, HD_A), F32)
    v_all = jnp.concatenate([prev_ref[0, :, nkv:2 * nkv], cur_ref[0, :, nq + nkv:nq + 2 * nkv]], axis=0)
    v_t = v_all.T.astype(BF16)
    k_pads = []
    for h in range(N_KV_A):
        k = jnp.concatenate([prev_ref[0, :, h * HD_A:(h + 1) * HD_A],
                             cur_ref[0, :, nq + h * HD_A:nq + (h + 1) * HD_A]], axis=0)
        kn = k * lax.rsqrt(jnp.mean(k * k, axis=-1, keepdims=True) + EPS) * kscale
        k_pads.append((jnp.concatenate([kn, zeros], axis=1).astype(BF16),
                       jnp.concatenate([zeros, kn], axis=1).astype(BF16)))

    def logits(i):
        j = 2 * i
        k_pad = k_pads[j // GROUP_A]
        q_tile = q_bf[:, j * HD_A:(j + 2) * HD_A]
        rq = jnp.concatenate([rq_t[j:j + 1, :], rq_t[j + 1:j + 2, :]], axis=1)
        s = jnp.concatenate([_nt_dot(k_pad[0], q_tile), _nt_dot(k_pad[1], q_tile)], axis=1) * rq
        return jnp.where(mask2, s, NEG_INF)

    def attend(i, s):
        j = 2 * i
        sink = jnp.where(first_half, sink_ref[j], sink_ref[j + 1]) * LOG2E
        mx = jnp.maximum(jnp.max(s, axis=0, keepdims=True), sink)
        p = jnp.exp2(s - mx)
        den = jnp.sum(p, axis=0, keepdims=True) + jnp.exp2(sink - mx)
        h = j // GROUP_A
        return _dot(v_t[h * HD_A:(h + 1) * HD_A, :], p.astype(BF16)), den

    def finish(i, o_t, den):
        o_t = o_t / den
        o_pair = jnp.concatenate([o_t[:, :BLOCK_A], o_t[:, BLOCK_A:]], axis=0)
        o_ref[0, :, 2 * i * HD_A:(2 * i + 2) * HD_A] = o_pair.T.astype(o_ref.dtype)

    n_pairs = N_KV_A * GROUP_A // 2
    ahead = 3
    pending = [logits(i) for i in range(ahead)]
    unfinished = None
    for i in range(n_pairs):
        if i + ahead < n_pairs:
            pending.append(logits(i + ahead))
        out = attend(i, pending.pop(0))
        if unfinished is not None:
            finish(i - 1, *unfinished)
        unfinished = out
    finish(n_pairs - 1, *unfinished)


def swa_attention(qkv, q_norm, k_norm, sinks):
    b, l, n = qkv.shape
    nq = N_KV_A * GROUP_A * HD_A
    kv_w = 2 * N_KV_A * HD_A
    head_sel = (jnp.arange(nq)[:, None] // HD_A == jnp.arange(LANES)[None, :]).astype(BF16)
    return pl.pallas_call(
        _swa_kernel,
        grid=(b, l // BLOCK_A),
        in_specs=[
            pl.BlockSpec(memory_space=pltpu.SMEM),
            pl.BlockSpec((1, BLOCK_A, n), lambda i, j: (i, j, 0)),
            pl.BlockSpec((1, BLOCK_A, kv_w), lambda i, j: (i, jnp.maximum(j - 1, 0), nq // kv_w)),
            pl.BlockSpec((1, HD_A), lambda i, j: (0, 0)),
            pl.BlockSpec((1, HD_A), lambda i, j: (0, 0)),
            pl.BlockSpec((nq, LANES), lambda i, j: (0, 0)),
        ],
        out_specs=pl.BlockSpec((1, BLOCK_A, nq), lambda i, j: (i, j, 0)),
        out_shape=jax.ShapeDtypeStruct((b, l, nq), BF16),
        compiler_params=_params("parallel", "parallel"),
        name="swa_attention",
    )(sinks, qkv, qkv, q_norm.reshape(1, HD_A), k_norm.reshape(1, HD_A), head_sel)


def _split3(x):
    hi = x.astype(BF16)
    r1 = x - hi.astype(F32)
    mid = r1.astype(BF16)
    lo = (r1 - mid.astype(F32)).astype(BF16)
    return hi, mid, lo


def _gla_kernel(q_ref, k_ref, v_ref, r_ref, gl_ref, wgb_ref, gb_ref, on_ref, o_ref, st_ref,
                *, tb, dk, dv):
    @pl.when(pl.program_id(1) == 0)
    def _():
        st_ref[...] = jnp.zeros_like(st_ref)

    ti = lax.broadcasted_iota(jnp.int32, (CHUNK, CHUNK), 0)
    si = lax.broadcasted_iota(jnp.int32, (CHUNK, CHUNK), 1)
    causal = si <= ti
    tri = causal.astype(BF16)
    wgb = wgb_ref[...]
    gb = gb_ref[...]
    on = on_ref[...]

    def chunk(c, carry):
        rs = pl.ds(pl.multiple_of(c * CHUNK, CHUNK), CHUNK)
        gate = _dot(gl_ref[0, rs, :].astype(BF16), wgb) + gb
        log_a = (jnp.minimum(gate, 0.0) - jnp.log1p(jnp.exp(-jnp.abs(gate)))) / GATE_TAU
        hi, mid, lo = _split3(log_a)
        bcum = _dot(tri, hi) + _dot(tri, mid) + _dot(tri, lo)
        b_last = bcum[CHUNK - 1:CHUNK, :]
        q = q_ref[0, rs, :]
        k = k_ref[0, rs, :]
        q_dec = (q * dk ** -0.5 * jnp.exp(bcum)).astype(BF16)
        k_inv = (k * jnp.exp(-bcum)).astype(BF16)
        k_end = (k * jnp.exp(b_last - bcum)).astype(BF16)
        decay = jnp.exp(b_last)
        for h in range(N_HEADS_B):
            ks = slice(h * dk, (h + 1) * dk)
            vs = slice(h * dv, (h + 1) * dv)
            v = v_ref[0, rs, vs].astype(BF16)
            a = jnp.where(causal, _nt_dot(q_dec[:, ks], k_inv[:, ks]), 0.0)
            st = st_ref[h]
            o = _dot(a.astype(BF16), v) + _nt_dot(q_dec[:, ks], st.astype(BF16))
            st_ref[h] = st * decay[:, ks] + _tn_dot(v, k_end[:, ks])
            o = _rms(o, on)
            r = r_ref[0, rs, vs]
            o_ref[0, rs, vs] = (o * (r * (1.0 / (1.0 + jnp.exp(-r))))).astype(o_ref.dtype)
        return carry

    lax.fori_loop(0, tb // CHUNK, chunk, 0)


def gla_attention(proj, w_gate_b, gate_bias, o_norm, *, dk, dv, tb):
    b, l, _ = proj.shape
    dk_all, dv_all = N_HEADS_B * dk, N_HEADS_B * dv
    return pl.pallas_call(
        functools.partial(_gla_kernel, tb=tb, dk=dk, dv=dv),
        grid=(b, l // tb),
        in_specs=[
            pl.BlockSpec((1, tb, dk_all), lambda i, j: (i, j, 0)),
            pl.BlockSpec((1, tb, dk_all), lambda i, j: (i, j, 1)),
            pl.BlockSpec((1, tb, dv_all), lambda i, j: (i, j, 2 * dk_all // dv_all)),
            pl.BlockSpec((1, tb, dv_all), lambda i, j: (i, j, 2 * dk_all // dv_all + 1)),
            pl.BlockSpec((1, tb, LANES), lambda i, j: (i, j, (2 * dk_all + 2 * dv_all) // LANES)),
            pl.BlockSpec((LANES, dk_all), lambda i, j: (0, 0)),
            pl.BlockSpec((1, dk_all), lambda i, j: (0, 0)),
            pl.BlockSpec((1, dv), lambda i, j: (0, 0)),
        ],
        out_specs=pl.BlockSpec((1, tb, dv_all), lambda i, j: (i, j, 0)),
        out_shape=jax.ShapeDtypeStruct((b, l, dv_all), BF16),
        scratch_shapes=[pltpu.VMEM((N_HEADS_B, dv, dk), F32)],
        compiler_params=_params("parallel", "arbitrary"),
        name="gla_attention",
    )(proj, proj, proj, proj, proj, w_gate_b, gate_bias.reshape(1, dk_all), o_norm.reshape(1, dv))


def _dsa_prep_kernel(x_ref, g_ref, win_ref, qln_ref, kvn_ref, wuq_ref, wuk_ref, qan_ref, wiq_ref,
                     ikn_ref, ikb_ref, qabs_ref, qidx_ref, widx_ref, ckv_ref, ckvt_ref, kidx_ref):
    xn = _rms(x_ref[...], g_ref[...]).astype(BF16)
    proj = _dot(xn, win_ref[...])
    o_kv = Q_RANK_C
    o_ki = Q_RANK_C + KV_RANK_C
    o_wi = o_ki + IDX_DIM
    c_q = _rms(proj[:, :o_kv], qln_ref[...]).astype(BF16)
    c_kv = _rms(proj[:, o_kv:o_ki], kvn_ref[...])
    ckv_ref[...] = c_kv.astype(BF16)
    ckvt_ref[...] = c_kv.T.astype(BF16)
    k_idx = proj[:, o_ki:o_wi]
    mu = jnp.mean(k_idx, axis=-1, keepdims=True)
    kc = k_idx - mu
    k_idx = kc * lax.rsqrt(jnp.mean(kc * kc, axis=-1, keepdims=True) + EPS)
    kidx_ref[...] = (k_idx * ikn_ref[...] + ikb_ref[...]).astype(BF16)
    widx_ref[...] = proj[:, o_wi:o_wi + N_IDX_HEADS] * (N_IDX_HEADS ** -0.5 * IDX_DIM ** -0.5)
    q_nope = _dot(c_q, wuq_ref[...])
    q_idx = _dot(c_q, wiq_ref[...])
    qan = qan_ref[...] * (KV_RANK_C ** -0.5 * LOG2E)
    for h in range(N_HEADS_C):
        qh = q_nope[:, h * QK_HEAD_C:(h + 1) * QK_HEAD_C].astype(BF16)
        qa = _rms(_dot(qh, wuk_ref[h]), qan).astype(BF16)
        qi = q_idx[:, h * IDX_DIM:(h + 1) * IDX_DIM].astype(BF16)
        for j in range(qabs_ref.shape[0]):
            qabs_ref[j, h] = qa[j * C_BLOCK:(j + 1) * C_BLOCK]
            qidx_ref[j, h] = qi[j * C_BLOCK:(j + 1) * C_BLOCK]


def dsa_prep(x, g, w_in, q_lat_norm, kv_norm, w_uq, w_uk, q_abs_norm, w_iq, idx_k_norm, idx_k_bias,
             *, tm):
    m, d = x.shape
    nblk = m // C_BLOCK
    bpt = tm // C_BLOCK
    n_in = w_in.shape[1]
    const2 = lambda i: (0, 0)
    return pl.pallas_call(
        _dsa_prep_kernel,
        grid=(m // tm,),
        in_specs=[
            pl.BlockSpec((tm, d), lambda i: (i, 0)),
            pl.BlockSpec((1, d), const2),
            pl.BlockSpec((d, n_in), const2),
            pl.BlockSpec((1, Q_RANK_C), const2),
            pl.BlockSpec((1, KV_RANK_C), const2),
            pl.BlockSpec(w_uq.shape, const2),
            pl.BlockSpec(w_uk.shape, lambda i: (0, 0, 0)),
            pl.BlockSpec((1, KV_RANK_C), const2),
            pl.BlockSpec(w_iq.shape, const2),
            pl.BlockSpec((1, IDX_DIM), const2),
            pl.BlockSpec((1, IDX_DIM), const2),
        ],
        out_specs=[
            pl.BlockSpec((bpt, N_HEADS_C, C_BLOCK, KV_RANK_C), lambda i: (i, 0, 0, 0)),
            pl.BlockSpec((bpt, N_IDX_HEADS, C_BLOCK, IDX_DIM), lambda i: (i, 0, 0, 0)),
            pl.BlockSpec((tm, N_IDX_HEADS), lambda i: (i, 0)),
            pl.BlockSpec((tm, KV_RANK_C), lambda i: (i, 0)),
            pl.BlockSpec((KV_RANK_C, tm), lambda i: (0, i)),
            pl.BlockSpec((tm, IDX_DIM), lambda i: (i, 0)),
        ],
        out_shape=[
            jax.ShapeDtypeStruct((nblk, N_HEADS_C, C_BLOCK, KV_RANK_C), BF16),
            jax.ShapeDtypeStruct((nblk, N_IDX_HEADS, C_BLOCK, IDX_DIM), BF16),
            jax.ShapeDtypeStruct((m, N_IDX_HEADS), F32),
            jax.ShapeDtypeStruct((m, KV_RANK_C), BF16),
            jax.ShapeDtypeStruct((KV_RANK_C, m), BF16),
            jax.ShapeDtypeStruct((m, IDX_DIM), BF16),
        ],
        compiler_params=_params("parallel"),
        name="dsa_prep",
    )(x, g.reshape(1, d), w_in, q_lat_norm.reshape(1, -1), kv_norm.reshape(1, -1), w_uq, w_uk,
      q_abs_norm.reshape(1, -1), w_iq, idx_k_norm.reshape(1, -1), idx_k_bias.reshape(1, -1))


def _sortable_key(s):
    bits = pltpu.bitcast(s, jnp.int32)
    return bits ^ ((bits >> 31) & 0x7FFFFFFF)


def _dsa_attn_kernel(qabs_ref, qidx_ref, widx_ref, ckv_ref, ckvt_ref, kidx_ref, wuv_ref, o_ref,
                     key_ref, m_ref, acc_ref, *, topk, idx_bits):
    n = pl.program_id(1)
    tk = KEY_TILE_C
    n_tiles = (n * C_BLOCK + C_BLOCK + tk - 1) // tk
    q_row = n * C_BLOCK + lax.broadcasted_iota(jnp.int32, (C_BLOCK, tk), 0)
    k_col = lax.broadcasted_iota(jnp.int32, (C_BLOCK, tk), 1)
    k_row = lax.broadcasted_iota(jnp.int32, (tk, C_BLOCK), 0)
    widx = widx_ref[0, 0]

    def tile(t):
        return pl.ds(pl.multiple_of(t * tk, tk), tk)

    def score_tile(t, carry):
        kt = kidx_ref[0, tile(t), :]
        score = jnp.zeros((C_BLOCK, tk), F32)
        for h in range(N_IDX_HEADS):
            rel = jnp.maximum(_nt_dot(qidx_ref[0, h], kt), 0.0)
            score = score + rel * widx[:, h:h + 1]
        score = jnp.where(t * tk + k_col <= q_row, score, NEG_INF)
        key_ref[tile(t), :] = _sortable_key(score.T)
        return carry

    lax.fori_loop(0, n_tiles, score_tile, 0)

    def count(pred):
        def body(t, cnt):
            hit = pred(key_ref[tile(t), :], t).astype(jnp.int32)
            return cnt + jnp.sum(hit.reshape(tk // 8, 8, C_BLOCK), axis=0)
        cnt = lax.fori_loop(0, n_tiles, body, jnp.zeros((8, C_BLOCK), jnp.int32))
        return jnp.sum(cnt, axis=0, keepdims=True)

    def bisect(i, lo):
        cand = lo + lax.shift_left(jnp.int32(1), 31 - i)
        return jnp.where(count(lambda k, t: k >= cand) >= topk, cand, lo)

    thresh = lax.fori_loop(0, 32, bisect, jnp.full((1, C_BLOCK), INT_MIN, jnp.int32))

    n_ge = count(lambda k, t: k >= thresh)
    excess = jnp.max(jnp.where((n_ge > topk) & (thresh > NEG_KEY), 1, 0))

    @pl.when(excess > 0)
    def _():
        want = topk - count(lambda k, t: k > thresh)

        def index_bisect(i, j0):
            cand = j0 + lax.shift_left(jnp.int32(1), idx_bits - 1 - i)
            before = count(lambda k, t: (k == thresh) & (t * tk + k_row < cand))
            return jnp.where(before < want, cand, j0)

        j0 = lax.fori_loop(0, idx_bits, index_bisect, jnp.zeros((1, C_BLOCK), jnp.int32))

        def demote(t, carry):
            keys = key_ref[tile(t), :]
            drop = (keys == thresh) & (t * tk + k_row > j0)
            key_ref[tile(t), :] = jnp.where(drop, thresh - 1, keys)
            return carry

        lax.fori_loop(0, n_tiles, demote, 0)

    thr = jnp.maximum(thresh, NEG_KEY + 1)
    thr2 = jnp.concatenate([thr, thr], axis=1)
    m_ref[...] = jnp.full(m_ref.shape, NEG_INF, F32)
    acc_ref[...] = jnp.zeros_like(acc_ref)
    ones_rows = jnp.ones((ONES_ROWS_C, tk), BF16)

    def attn_tile(t, carry):
        kv = ckv_ref[0, tile(t), :]
        kvt = jnp.concatenate([ckvt_ref[:, tile(t)], ones_rows], axis=0)
        keys = key_ref[tile(t), :]
        sel = jnp.concatenate([keys, keys], axis=1) >= thr2
        def logits(p):
            q = qabs_ref[0, 2 * p:2 * p + 2].reshape(2 * C_BLOCK, KV_RANK_C)
            return jnp.where(sel, _nt_dot(kv, q), NEG_INF)

        def accumulate(p, s):
            m_old = m_ref[p]
            m_new = jnp.maximum(m_old, jnp.max(s, axis=0, keepdims=True))
            pe = jnp.exp2(s - m_new)
            alpha = jnp.exp2(m_old - m_new)
            acc_ref[p] = alpha * acc_ref[p] + _dot(kvt, pe.astype(BF16))
            m_ref[p] = m_new

        n_pairs = N_HEADS_C // 2
        pending = [logits(p) for p in range(ATTN_AHEAD_C)]
        for p in range(n_pairs):
            if p + ATTN_AHEAD_C < n_pairs:
                pending.append(logits(p + ATTN_AHEAD_C))
            accumulate(p, pending.pop(0))
        return carry

    lax.fori_loop(0, n_tiles, attn_tile, 0)

    for p in range(N_HEADS_C // 2):
        denom = acc_ref[p, KV_RANK_C:KV_RANK_C + 1, :]
        o_lat_t = (acc_ref[p, :KV_RANK_C, :] / denom).astype(BF16)
        for e in range(2):
            h = 2 * p + e
            o_ref[0, :, h * V_HEAD_C:(h + 1) * V_HEAD_C] = _tn_dot(
                o_lat_t[:, e * C_BLOCK:(e + 1) * C_BLOCK], wuv_ref[h]).astype(o_ref.dtype)


def dsa_attention(qabs, qidx, widx, ckv, ckvt, kidx, w_uv, *, batch, topk):
    nblk = qabs.shape[0]
    nb = nblk // batch
    l = nb * C_BLOCK
    assert l % KEY_TILE_C == 0 and l >= topk
    ckv = ckv.reshape(batch, l, KV_RANK_C)
    kidx = kidx.reshape(batch, l, IDX_DIM)
    widx = widx.reshape(batch, nb, C_BLOCK, N_IDX_HEADS)
    return pl.pallas_call(
        functools.partial(_dsa_attn_kernel, topk=topk, idx_bits=(l - 1).bit_length()),
        grid=(batch, nb),
        in_specs=[
            pl.BlockSpec((1, N_HEADS_C, C_BLOCK, KV_RANK_C), lambda b, j: (b * nb + j, 0, 0, 0)),
            pl.BlockSpec((1, N_IDX_HEADS, C_BLOCK, IDX_DIM), lambda b, j: (b * nb + j, 0, 0, 0)),
            pl.BlockSpec((1, 1, C_BLOCK, N_IDX_HEADS), lambda b, j: (b, j, 0, 0)),
            pl.BlockSpec((1, l, KV_RANK_C), lambda b, j: (b, 0, 0)),
            pl.BlockSpec((KV_RANK_C, l), lambda b, j: (0, b)),
            pl.BlockSpec((1, l, IDX_DIM), lambda b, j: (b, 0, 0)),
            pl.BlockSpec(w_uv.shape, lambda b, j: (0, 0, 0)),
        ],
        out_specs=pl.BlockSpec((1, C_BLOCK, N_HEADS_C * V_HEAD_C), lambda b, j: (b, j, 0)),
        out_shape=jax.ShapeDtypeStruct((batch, l, N_HEADS_C * V_HEAD_C), BF16),
        scratch_shapes=[
            pltpu.VMEM((l, C_BLOCK), jnp.int32),
            pltpu.VMEM((N_HEADS_C // 2, 1, 2 * C_BLOCK), F32),
            pltpu.VMEM((N_HEADS_C // 2, KV_RANK_C + ONES_ROWS_C, 2 * C_BLOCK), F32),
        ],
        compiler_params=_params("parallel", "arbitrary"),
        name="dsa_attention",
    )(qabs, qidx, widx, ckv, ckvt, kidx, w_uv)


TM = 512
TF = 512


def _swa_layer(h, batch, norm_mix, w_in, q_norm, k_norm, sinks, wo):
    m, d = h.shape
    qkv = norm_matmul(h, norm_mix, w_in.astype(BF16), tm=TM, tn=w_in.shape[1])
    o = swa_attention(qkv.reshape(batch, m // batch, -1), q_norm, k_norm, sinks)
    return matmul_res(o.reshape(m, -1), wo.astype(BF16), h, tm=TM, tn=d)


def _gla_layer(h, batch, norm_mix, w_in, w_gate_b, gate_bias, o_norm, wo):
    m, d = h.shape
    dk_all = w_gate_b.shape[1]
    dv_all = wo.shape[0]
    n_main = 2 * dk_all + 2 * dv_all
    w_pad = jnp.pad(w_in.astype(BF16), ((0, 0), (0, LANES - GATE_RANK)))
    wgb_pad = jnp.pad(w_gate_b.astype(BF16), ((0, LANES - GATE_RANK), (0, 0)))
    proj = norm_matmul(h, norm_mix, w_pad, tm=2 * TM, tn=(n_main + LANES) // 7)
    o = gla_attention(proj.reshape(batch, m // batch, -1), wgb_pad, gate_bias, o_norm,
                      dk=dk_all // N_HEADS_B, dv=dv_all // N_HEADS_B, tb=256)
    return matmul_res(o.reshape(m, -1), wo.astype(BF16), h, tm=TM, tn=d)


def _dsa_layer(h, batch, norm_mix, w_in, q_lat_norm, kv_norm, w_uq, w_uk, q_abs_norm, w_iq,
               idx_k_norm, idx_k_bias, w_uv, wo):
    m, d = h.shape
    l = m // batch
    qabs, qidx, widx, ckv, ckvt, kidx = dsa_prep(
        h, norm_mix, w_in.astype(BF16), q_lat_norm, kv_norm, w_uq.astype(BF16), w_uk.astype(BF16),
        q_abs_norm, w_iq.astype(BF16), idx_k_norm, idx_k_bias, tm=TM)
    o = dsa_attention(qabs, qidx, widx, ckv, ckvt, kidx, w_uv.astype(BF16), batch=batch,
                      topk=min(TOPK_MAX, l // 4))
    return matmul_res(o.reshape(m, -1), wo.astype(BF16), h, tm=TM, tn=d)


def kernel(x, l0_norm_mix, l0_w_in, l0_q_norm, l0_k_norm, l0_sinks, l0_wo, l0_norm_ffn, l0_w_gate_up, l0_w_down, l1_norm_mix, l1_w_in, l1_w_gate_b, l1_gate_bias, l1_o_norm, l1_wo, l1_norm_ffn, l1_w_gate_up, l1_w_down, l2_norm_mix, l2_w_in, l2_q_lat_norm, l2_kv_norm, l2_w_uq, l2_w_uk, l2_q_abs_norm, l2_w_iq, l2_idx_k_norm, l2_idx_k_bias, l2_w_uv, l2_wo, l2_norm_ffn, l2_w_gate_up, l2_w_down, l3_norm_mix, l3_w_in, l3_q_norm, l3_k_norm, l3_sinks, l3_wo, l3_norm_ffn, l3_w_gate_up, l3_w_down):
    batch, seq, d = x.shape
    h = x.reshape(batch * seq, d)

    def channel_mix(h, norm_ffn, w_gate_up, w_down):
        return ffn(h, norm_ffn, w_gate_up.astype(BF16), w_down.astype(BF16), tm=TM, tf=TF)

    h = _swa_layer(h, batch, l0_norm_mix, l0_w_in, l0_q_norm, l0_k_norm, l0_sinks, l0_wo)
    h = channel_mix(h, l0_norm_ffn, l0_w_gate_up, l0_w_down)
    h = _gla_layer(h, batch, l1_norm_mix, l1_w_in, l1_w_gate_b, l1_gate_bias, l1_o_norm, l1_wo)
    h = channel_mix(h, l1_norm_ffn, l1_w_gate_up, l1_w_down)
    h = _dsa_layer(h, batch, l2_norm_mix, l2_w_in, l2_q_lat_norm, l2_kv_norm, l2_w_uq, l2_w_uk,
                   l2_q_abs_norm, l2_w_iq, l2_idx_k_norm, l2_idx_k_bias, l2_w_uv, l2_wo)
    h = channel_mix(h, l2_norm_ffn, l2_w_gate_up, l2_w_down)
    h = _swa_layer(h, batch, l3_norm_mix, l3_w_in, l3_q_norm, l3_k_norm, l3_sinks, l3_wo)
    h = channel_mix(h, l3_norm_ffn, l3_w_gate_up, l3_w_down)
    return h.reshape(batch, seq, d)
```

```python
import functools

import jax
import jax.numpy as jnp
import numpy as np
from jax import lax
from jax.experimental import pallas as pl
from jax.experimental.pallas import tpu as pltpu

F32 = jnp.float32
BF16 = jnp.bfloat16
EPS = 1e-6
NEG_INF = -1e30
LANES = 128

HD_A = 64
N_KV_A = 4
GROUP_A = 8
BLOCK_A = 128
N_HEADS_B = 4
GATE_RANK = 16
GATE_TAU = 16.0
CHUNK = 64
N_HEADS_C = 16
QK_HEAD_C = 128
V_HEAD_C = 128
Q_RANK_C = 512
KV_RANK_C = 256
N_IDX_HEADS = 16
IDX_DIM = 64
TOPK_MAX = 256
C_BLOCK = 128
KEY_TILE_C = 512
ATTN_AHEAD_C = 2
ONES_ROWS_C = 16
BOUND_SLACK_C = 1.01
MAX_FIXED_OFFSET_C = 56.0
INT_MIN = -2 ** 31
LOG2E = 1.4426950408889634
_NEG_BITS = int(np.float32(NEG_INF).view(np.int32))
NEG_KEY = _NEG_BITS ^ ((_NEG_BITS >> 31) & 0x7FFFFFFF)


def _nt_dot(a, b):
    return lax.dot_general(a, b, (((1,), (1,)), ((), ())), preferred_element_type=F32)


def _tn_dot(a, b):
    return lax.dot_general(a, b, (((0,), (0,)), ((), ())), preferred_element_type=F32)


def _dot(a, b):
    return jnp.dot(a, b, preferred_element_type=F32)


def _rms(x, g):
    return x * lax.rsqrt(jnp.mean(x * x, axis=-1, keepdims=True) + EPS) * g


def _params(*sem, vmem_bytes=None):
    return pltpu.CompilerParams(dimension_semantics=sem, vmem_limit_bytes=vmem_bytes)


def _norm_matmul_kernel(x_ref, g_ref, w_ref, o_ref, xn_ref):
    @pl.when(pl.program_id(1) == 0)
    def _():
        xn_ref[...] = _rms(x_ref[...], g_ref[...]).astype(BF16)

    o_ref[...] = _dot(xn_ref[...], w_ref[...]).astype(o_ref.dtype)


def norm_matmul(x, g, w, *, tm, tn, out_dtype=F32):
    m, d = x.shape
    n = w.shape[1]
    return pl.pallas_call(
        _norm_matmul_kernel,
        grid=(m // tm, n // tn),
        in_specs=[
            pl.BlockSpec((tm, d), lambda i, j: (i, 0)),
            pl.BlockSpec((1, d), lambda i, j: (0, 0)),
            pl.BlockSpec((d, tn), lambda i, j: (0, j)),
        ],
        out_specs=pl.BlockSpec((tm, tn), lambda i, j: (i, j)),
        out_shape=jax.ShapeDtypeStruct((m, n), out_dtype),
        scratch_shapes=[pltpu.VMEM((tm, d), BF16)],
        compiler_params=_params("parallel", "arbitrary"),
        name="norm_matmul",
    )(x, g.reshape(1, d), w)


def _matmul_res_kernel(a_ref, w_ref, r_ref, o_ref):
    o_ref[...] = r_ref[...] + _dot(a_ref[...], w_ref[...])


def matmul_res(a, w, r, *, tm, tn):
    m, k = a.shape
    n = w.shape[1]
    return pl.pallas_call(
        _matmul_res_kernel,
        grid=(m // tm, n // tn),
        in_specs=[
            pl.BlockSpec((tm, k), lambda i, j: (i, 0)),
            pl.BlockSpec((k, tn), lambda i, j: (0, j)),
            pl.BlockSpec((tm, tn), lambda i, j: (i, j)),
        ],
        out_specs=pl.BlockSpec((tm, tn), lambda i, j: (i, j)),
        out_shape=jax.ShapeDtypeStruct((m, n), F32),
        compiler_params=_params("parallel", "arbitrary"),
        name="matmul_res",
    )(a, w, r)


def _ffn_kernel(x_ref, g_ref, wg_ref, wu_ref, wd_ref, o_ref, xn_ref):
    @pl.when(pl.program_id(1) == 0)
    def _():
        x = x_ref[...]
        xn_ref[...] = _rms(x, g_ref[...]).astype(BF16)
        o_ref[...] = x

    xn = xn_ref[...]
    gate = _dot(xn, wg_ref[...].astype(BF16))
    up = _dot(xn, wu_ref[...].astype(BF16))
    act = (gate * (1.0 / (1.0 + jnp.exp(-gate))) * up).astype(BF16)
    o_ref[...] += _dot(act, wd_ref[...].astype(BF16))


def ffn(x, g, w_gate_up, w_down, *, tm, tf):
    m, d = x.shape
    ff = w_down.shape[0]
    nf = ff // tf
    w_bytes = w_down.dtype.itemsize
    vmem_bytes = (4 * tm * d * 4 + tm * d * 2 + 6 * d * tf * w_bytes
                  + 3 * tm * tf * 4 + 3 * d * tf * 2)
    return pl.pallas_call(
        _ffn_kernel,
        grid=(m // tm, nf),
        in_specs=[
            pl.BlockSpec((tm, d), lambda i, j: (i, 0)),
            pl.BlockSpec((1, d), lambda i, j: (0, 0)),
            pl.BlockSpec((d, tf), lambda i, j: (0, j)),
            pl.BlockSpec((d, tf), lambda i, j: (0, j + nf)),
            pl.BlockSpec((tf, d), lambda i, j: (j, 0)),
        ],
        out_specs=pl.BlockSpec((tm, d), lambda i, j: (i, 0)),
        out_shape=jax.ShapeDtypeStruct((m, d), F32),
        scratch_shapes=[pltpu.VMEM((tm, d), BF16)],
        compiler_params=_params("parallel", "arbitrary", vmem_bytes=vmem_bytes),
        name="ffn",
    )(x, g.reshape(1, d), w_gate_up, w_gate_up, w_down)


def _swa_kernel(sink_ref, cur_ref, prev_ref, qn_ref, kn_ref, sel_ref, o_ref):
    n = pl.program_id(1)
    nq = N_KV_A * GROUP_A * HD_A
    nkv = N_KV_A * HD_A
    kj = lax.broadcasted_iota(jnp.int32, (2 * BLOCK_A, 2 * BLOCK_A), 0)
    qi = lax.broadcasted_iota(jnp.int32, (2 * BLOCK_A, 2 * BLOCK_A), 1) & (BLOCK_A - 1)
    mask2 = (kj > qi) & (kj <= qi + BLOCK_A) & ((n > 0) | (kj >= BLOCK_A))
    first_half = lax.broadcasted_iota(jnp.int32, (1, 2 * BLOCK_A), 1) < BLOCK_A
    xq = cur_ref[0, :, :nq]
    sq = xq * xq
    sq_hi = sq.astype(BF16)
    sq_lo = (sq - sq_hi.astype(F32)).astype(BF16)
    ssq = _dot(sq_hi, sel_ref[...]) + _dot(sq_lo, sel_ref[...])
    rq_t = (lax.rsqrt(ssq * (1.0 / HD_A) + EPS) * LOG2E).T
    q_bf = xq.astype(BF16)
    kscale = qn_ref[...] * kn_ref[...] * HD_A ** -0.5
    zeros = jnp.zeros((2 * BLOCK_A, HD_A), F32)
    v_all = jnp.concatenate([prev_ref[0, :, nkv:2 * nkv], cur_ref[0, :, nq + nkv:nq + 2 * nkv]], axis=0)
    v_t = v_all.T.astype(BF16)
    k_pads = []
    for h in range(N_KV_A):
        k = jnp.concatenate([prev_ref[0, :, h * HD_A:(h + 1) * HD_A],
                             cur_ref[0, :, nq + h * HD_A:nq + (h + 1) * HD_A]], axis=0)
        kn = k * lax.rsqrt(jnp.mean(k * k, axis=-1, keepdims=True) + EPS) * kscale
        k_pads.append((jnp.concatenate([kn, zeros], axis=1).astype(BF16),
                       jnp.concatenate([zeros, kn], axis=1).astype(BF16)))

    def logits(i):
        j = 2 * i
        k_pad = k_pads[j // GROUP_A]
        q_tile = q_bf[:, j * HD_A:(j + 2) * HD_A]
        rq = jnp.concatenate([rq_t[j:j + 1, :], rq_t[j + 1:j + 2, :]], axis=1)
        s = jnp.concatenate([_nt_dot(k_pad[0], q_tile), _nt_dot(k_pad[1], q_tile)], axis=1) * rq
        return jnp.where(mask2, s, NEG_INF)

    def attend(i, s):
        j = 2 * i
        sink = jnp.where(first_half, sink_ref[j], sink_ref[j + 1]) * LOG2E
        mx = jnp.maximum(jnp.max(s, axis=0, keepdims=True), sink)
        p = jnp.exp2(s - mx)
        den = jnp.sum(p, axis=0, keepdims=True) + jnp.exp2(sink - mx)
        h = j // GROUP_A
        return _dot(v_t[h * HD_A:(h + 1) * HD_A, :], p.astype(BF16)), den

    def finish(i, o_t, den):
        o_t = o_t / den
        o_pair = jnp.concatenate([o_t[:, :BLOCK_A], o_t[:, BLOCK_A:]], axis=0)
        o_ref[0, :, 2 * i * HD_A:(2 * i + 2) * HD_A] = o_pair.T.astype(o_ref.dtype)

    n_pairs = N_KV_A * GROUP_A // 2
    ahead = 3
    pending = [logits(i) for i in range(ahead)]
    unfinished = None
    for i in range(n_pairs):
        if i + ahead < n_pairs:
            pending.append(logits(i + ahead))
        out = attend(i, pending.pop(0))
        if unfinished is not None:
            finish(i - 1, *unfinished)
        unfinished = out
    finish(n_pairs - 1, *unfinished)


def swa_attention(qkv, q_norm, k_norm, sinks):
    b, l, n = qkv.shape
    nq = N_KV_A * GROUP_A * HD_A
    kv_w = 2 * N_KV_A * HD_A
    head_sel = (jnp.arange(nq)[:, None] // HD_A == jnp.arange(LANES)[None, :]).astype(BF16)
    return pl.pallas_call(
        _swa_kernel,
        grid=(b, l // BLOCK_A),
        in_specs=[
            pl.BlockSpec(memory_space=pltpu.SMEM),
            pl.BlockSpec((1, BLOCK_A, n), lambda i, j: (i, j, 0)),
            pl.BlockSpec((1, BLOCK_A, kv_w), lambda i, j: (i, jnp.maximum(j - 1, 0), nq // kv_w)),
            pl.BlockSpec((1, HD_A), lambda i, j: (0, 0)),
            pl.BlockSpec((1, HD_A), lambda i, j: (0, 0)),
            pl.BlockSpec((nq, LANES), lambda i, j: (0, 0)),
        ],
        out_specs=pl.BlockSpec((1, BLOCK_A, nq), lambda i, j: (i, j, 0)),
        out_shape=jax.ShapeDtypeStruct((b, l, nq), BF16),
        compiler_params=_params("parallel", "parallel"),
        name="swa_attention",
    )(sinks, qkv, qkv, q_norm.reshape(1, HD_A), k_norm.reshape(1, HD_A), head_sel)


def _split3(x):
    hi = x.astype(BF16)
    r1 = x - hi.astype(F32)
    mid = r1.astype(BF16)
    lo = (r1 - mid.astype(F32)).astype(BF16)
    return hi, mid, lo


def _gla_kernel(q_ref, k_ref, v_ref, r_ref, gl_ref, wgb_ref, gb_ref, on_ref, o_ref, st_ref,
                *, tb, dk, dv):
    @pl.when(pl.program_id(1) == 0)
    def _():
        st_ref[...] = jnp.zeros_like(st_ref)

    ti = lax.broadcasted_iota(jnp.int32, (CHUNK, CHUNK), 0)
    si = lax.broadcasted_iota(jnp.int32, (CHUNK, CHUNK), 1)
    causal = si <= ti
    tri = causal.astype(BF16)
    wgb = wgb_ref[...]
    gb = gb_ref[...]
    on = on_ref[...]

    def chunk(c, carry):
        rs = pl.ds(pl.multiple_of(c * CHUNK, CHUNK), CHUNK)
        gate = _dot(gl_ref[0, rs, :].astype(BF16), wgb) + gb
        log_a = (jnp.minimum(gate, 0.0) - jnp.log1p(jnp.exp(-jnp.abs(gate)))) / GATE_TAU
        hi, mid, lo = _split3(log_a)
        bcum = _dot(tri, hi) + _dot(tri, mid) + _dot(tri, lo)
        b_last = bcum[CHUNK - 1:CHUNK, :]
        q = q_ref[0, rs, :]
        k = k_ref[0, rs, :]
        q_dec = (q * dk ** -0.5 * jnp.exp(bcum)).astype(BF16)
        k_inv = (k * jnp.exp(-bcum)).astype(BF16)
        k_end = (k * jnp.exp(b_last - bcum)).astype(BF16)
        decay = jnp.exp(b_last)
        for h in range(N_HEADS_B):
            ks = slice(h * dk, (h + 1) * dk)
            vs = slice(h * dv, (h + 1) * dv)
            v = v_ref[0, rs, vs].astype(BF16)
            a = jnp.where(causal, _nt_dot(q_dec[:, ks], k_inv[:, ks]), 0.0)
            st = st_ref[h]
            o = _dot(a.astype(BF16), v) + _nt_dot(q_dec[:, ks], st.astype(BF16))
            st_ref[h] = st * decay[:, ks] + _tn_dot(v, k_end[:, ks])
            o = _rms(o, on)
            r = r_ref[0, rs, vs]
            o_ref[0, rs, vs] = (o * (r * (1.0 / (1.0 + jnp.exp(-r))))).astype(o_ref.dtype)
        return carry

    lax.fori_loop(0, tb // CHUNK, chunk, 0)


def gla_attention(proj, w_gate_b, gate_bias, o_norm, *, dk, dv, tb):
    b, l, _ = proj.shape
    dk_all, dv_all = N_HEADS_B * dk, N_HEADS_B * dv
    return pl.pallas_call(
        functools.partial(_gla_kernel, tb=tb, dk=dk, dv=dv),
        grid=(b, l // tb),
        in_specs=[
            pl.BlockSpec((1, tb, dk_all), lambda i, j: (i, j, 0)),
            pl.BlockSpec((1, tb, dk_all), lambda i, j: (i, j, 1)),
            pl.BlockSpec((1, tb, dv_all), lambda i, j: (i, j, 2 * dk_all // dv_all)),
            pl.BlockSpec((1, tb, dv_all), lambda i, j: (i, j, 2 * dk_all // dv_all + 1)),
            pl.BlockSpec((1, tb, LANES), lambda i, j: (i, j, (2 * dk_all + 2 * dv_all) // LANES)),
            pl.BlockSpec((LANES, dk_all), lambda i, j: (0, 0)),
            pl.BlockSpec((1, dk_all), lambda i, j: (0, 0)),
            pl.BlockSpec((1, dv), lambda i, j: (0, 0)),
        ],
        out_specs=pl.BlockSpec((1, tb, dv_all), lambda i, j: (i, j, 0)),
        out_shape=jax.ShapeDtypeStruct((b, l, dv_all), BF16),
        scratch_shapes=[pltpu.VMEM((N_HEADS_B, dv, dk), F32)],
        compiler_params=_params("parallel", "arbitrary"),
        name="gla_attention",
    )(proj, proj, proj, proj, proj, w_gate_b, gate_bias.reshape(1, dk_all), o_norm.reshape(1, dv))


def _dsa_prep_kernel(x_ref, g_ref, win_ref, qln_ref, kvn_ref, wuq_ref, wuk_ref, qan_ref, wiq_ref,
                     ikn_ref, ikb_ref, qabs_ref, qidx_ref, widx_ref, ckv_ref, ckvt_ref, kidx_ref):
    xn = _rms(x_ref[...], g_ref[...]).astype(BF16)
    proj = _dot(xn, win_ref[...])
    o_kv = Q_RANK_C
    o_ki = Q_RANK_C + KV_RANK_C
    o_wi = o_ki + IDX_DIM
    c_q = _rms(proj[:, :o_kv], qln_ref[...]).astype(BF16)
    c_kv = _rms(proj[:, o_kv:o_ki], kvn_ref[...])
    ckv_ref[...] = c_kv.astype(BF16)
    ckvt_ref[...] = c_kv.T.astype(BF16)
    k_idx = proj[:, o_ki:o_wi]
    mu = jnp.mean(k_idx, axis=-1, keepdims=True)
    kc = k_idx - mu
    k_idx = kc * lax.rsqrt(jnp.mean(kc * kc, axis=-1, keepdims=True) + EPS)
    kidx_ref[...] = (k_idx * ikn_ref[...] + ikb_ref[...]).astype(BF16)
    widx_ref[...] = proj[:, o_wi:o_wi + N_IDX_HEADS] * (N_IDX_HEADS ** -0.5 * IDX_DIM ** -0.5)
    q_nope = _dot(c_q, wuq_ref[...])
    q_idx = _dot(c_q, wiq_ref[...])
    qan = qan_ref[...] * (KV_RANK_C ** -0.5 * LOG2E)
    for h in range(N_HEADS_C):
        qh = q_nope[:, h * QK_HEAD_C:(h + 1) * QK_HEAD_C].astype(BF16)
        qa = _rms(_dot(qh, wuk_ref[h]), qan).astype(BF16)
        qi = q_idx[:, h * IDX_DIM:(h + 1) * IDX_DIM].astype(BF16)
        for j in range(qabs_ref.shape[0]):
            qabs_ref[j, h] = qa[j * C_BLOCK:(j + 1) * C_BLOCK]
            qidx_ref[j, h] = qi[j * C_BLOCK:(j + 1) * C_BLOCK]


def dsa_prep(x, g, w_in, q_lat_norm, kv_norm, w_uq, w_uk, q_abs_norm, w_iq, idx_k_norm, idx_k_bias,
             *, tm):
    m, d = x.shape
    nblk = m // C_BLOCK
    bpt = tm // C_BLOCK
    n_in = w_in.shape[1]
    const2 = lambda i: (0, 0)
    return pl.pallas_call(
        _dsa_prep_kernel,
        grid=(m // tm,),
        in_specs=[
            pl.BlockSpec((tm, d), lambda i: (i, 0)),
            pl.BlockSpec((1, d), const2),
            pl.BlockSpec((d, n_in), const2),
            pl.BlockSpec((1, Q_RANK_C), const2),
            pl.BlockSpec((1, KV_RANK_C), const2),
            pl.BlockSpec(w_uq.shape, const2),
            pl.BlockSpec(w_uk.shape, lambda i: (0, 0, 0)),
            pl.BlockSpec((1, KV_RANK_C), const2),
            pl.BlockSpec(w_iq.shape, const2),
            pl.BlockSpec((1, IDX_DIM), const2),
            pl.BlockSpec((1, IDX_DIM), const2),
        ],
        out_specs=[
            pl.BlockSpec((bpt, N_HEADS_C, C_BLOCK, KV_RANK_C), lambda i: (i, 0, 0, 0)),
            pl.BlockSpec((bpt, N_IDX_HEADS, C_BLOCK, IDX_DIM), lambda i: (i, 0, 0, 0)),
            pl.BlockSpec((tm, N_IDX_HEADS), lambda i: (i, 0)),
            pl.BlockSpec((tm, KV_RANK_C), lambda i: (i, 0)),
            pl.BlockSpec((KV_RANK_C, tm), lambda i: (0, i)),
            pl.BlockSpec((tm, IDX_DIM), lambda i: (i, 0)),
        ],
        out_shape=[
            jax.ShapeDtypeStruct((nblk, N_HEADS_C, C_BLOCK, KV_RANK_C), BF16),
            jax.ShapeDtypeStruct((nblk, N_IDX_HEADS, C_BLOCK, IDX_DIM), BF16),
            jax.ShapeDtypeStruct((m, N_IDX_HEADS), F32),
            jax.ShapeDtypeStruct((m, KV_RANK_C), BF16),
            jax.ShapeDtypeStruct((KV_RANK_C, m), BF16),
            jax.ShapeDtypeStruct((m, IDX_DIM), BF16),
        ],
        compiler_params=_params("parallel"),
        name="dsa_prep",
    )(x, g.reshape(1, d), w_in, q_lat_norm.reshape(1, -1), kv_norm.reshape(1, -1), w_uq, w_uk,
      q_abs_norm.reshape(1, -1), w_iq, idx_k_norm.reshape(1, -1), idx_k_bias.reshape(1, -1))


def _sortable_key(s):
    bits = pltpu.bitcast(s, jnp.int32)
    return bits ^ ((bits >> 31) & 0x7FFFFFFF)


def _dsa_attn_kernel(qabs_ref, qidx_ref, widx_ref, ckv_ref, ckvt_ref, kidx_ref, wuv_ref, qan_ref,
                     kvn_ref, o_ref, key_ref, m_ref, acc_ref, *, topk, idx_bits):
    n = pl.program_id(1)
    tk = KEY_TILE_C
    n_tiles = (n * C_BLOCK + C_BLOCK + tk - 1) // tk
    q_row = n * C_BLOCK + lax.broadcasted_iota(jnp.int32, (C_BLOCK, tk), 0)
    k_col = lax.broadcasted_iota(jnp.int32, (C_BLOCK, tk), 1)
    k_row = lax.broadcasted_iota(jnp.int32, (tk, C_BLOCK), 0)
    widx = widx_ref[0, 0]

    def tile(t):
        return pl.ds(pl.multiple_of(t * tk, tk), tk)

    def score_tile(t, carry):
        kt = kidx_ref[0, tile(t), :]
        score = jnp.zeros((C_BLOCK, tk), F32)
        for h in range(N_IDX_HEADS):
            rel = jnp.maximum(_nt_dot(qidx_ref[0, h], kt), 0.0)
            score = score + rel * widx[:, h:h + 1]
        score = jnp.where(t * tk + k_col <= q_row, score, NEG_INF)
        key_ref[tile(t), :] = _sortable_key(score.T)
        return carry

    lax.fori_loop(0, n_tiles, score_tile, 0)

    def count(pred):
        def body(t, cnt):
            hit = pred(key_ref[tile(t), :], t).astype(jnp.int32)
            return cnt + jnp.sum(hit.reshape(tk // 8, 8, C_BLOCK), axis=0)
        cnt = lax.fori_loop(0, n_tiles, body, jnp.zeros((8, C_BLOCK), jnp.int32))
        return jnp.sum(cnt, axis=0, keepdims=True)

    def bisect(i, lo):
        cand = lo + lax.shift_left(jnp.int32(1), 31 - i)
        return jnp.where(count(lambda k, t: k >= cand) >= topk, cand, lo)

    thresh = lax.fori_loop(0, 32, bisect, jnp.full((1, C_BLOCK), INT_MIN, jnp.int32))

    n_ge = count(lambda k, t: k >= thresh)
    excess = jnp.max(jnp.where((n_ge > topk) & (thresh > NEG_KEY), 1, 0))

    @pl.when(excess > 0)
    def _():
        want = topk - count(lambda k, t: k > thresh)

        def index_bisect(i, j0):
            cand = j0 + lax.shift_left(jnp.int32(1), idx_bits - 1 - i)
            before = count(lambda k, t: (k == thresh) & (t * tk + k_row < cand))
            return jnp.where(before < want, cand, j0)

        j0 = lax.fori_loop(0, idx_bits, index_bisect, jnp.zeros((1, C_BLOCK), jnp.int32))

        def demote(t, carry):
            keys = key_ref[tile(t), :]
            drop = (keys == thresh) & (t * tk + k_row > j0)
            key_ref[tile(t), :] = jnp.where(drop, thresh - 1, keys)
            return carry

        lax.fori_loop(0, n_tiles, demote, 0)

    thr = jnp.maximum(thresh, NEG_KEY + 1)
    thr2 = jnp.concatenate([thr, thr], axis=1)
    acc_ref[...] = jnp.zeros_like(acc_ref)
    ones_rows = jnp.ones((ONES_ROWS_C, tk), BF16)
    n_pairs = N_HEADS_C // 2

    def pair_q(p):
        return qabs_ref[0, 2 * p:2 * p + 2].reshape(2 * C_BLOCK, KV_RANK_C)

    bound = (KV_RANK_C * BOUND_SLACK_C * (KV_RANK_C ** -0.5 * LOG2E)
             * jnp.max(jnp.abs(qan_ref[...])) * jnp.max(jnp.abs(kvn_ref[...])))
    small_logits = bound <= MAX_FIXED_OFFSET_C

    def attn_loop(accumulate):
        def attn_tile(t, carry):
            kv = ckv_ref[0, tile(t), :]
            kvt = jnp.concatenate([ckvt_ref[:, tile(t)], ones_rows], axis=0)
            keys = key_ref[tile(t), :]
            sel = jnp.concatenate([keys, keys], axis=1) >= thr2

            def logits(p):
                return jnp.where(sel, _nt_dot(kv, pair_q(p)), NEG_INF)

            pending = [logits(p) for p in range(ATTN_AHEAD_C)]
            for p in range(n_pairs):
                if p + ATTN_AHEAD_C < n_pairs:
                    pending.append(logits(p + ATTN_AHEAD_C))
                accumulate(p, pending.pop(0), kvt)
            return carry

        lax.fori_loop(0, n_tiles, attn_tile, 0)

    def accumulate_fixed(p, s, kvt):
        acc_ref[p] += _dot(kvt, jnp.exp2(s - bound).astype(BF16))

    def accumulate_online(p, s, kvt):
        m_old = m_ref[p]
        m_new = jnp.maximum(m_old, jnp.max(s, axis=0, keepdims=True))
        pe = jnp.exp2(s - m_new)
        acc_ref[p] = jnp.exp2(m_old - m_new) * acc_ref[p] + _dot(kvt, pe.astype(BF16))
        m_ref[p] = m_new

    @pl.when(small_logits)
    def _():
        attn_loop(accumulate_fixed)

    @pl.when(jnp.logical_not(small_logits))
    def _():
        m_ref[...] = jnp.full(m_ref.shape, NEG_INF, F32)
        attn_loop(accumulate_online)

    for p in range(N_HEADS_C // 2):
        denom = acc_ref[p, KV_RANK_C:KV_RANK_C + 1, :]
        o_lat_t = (acc_ref[p, :KV_RANK_C, :] / denom).astype(BF16)
        for e in range(2):
            h = 2 * p + e
            o_ref[0, :, h * V_HEAD_C:(h + 1) * V_HEAD_C] = _tn_dot(
                o_lat_t[:, e * C_BLOCK:(e + 1) * C_BLOCK], wuv_ref[h]).astype(o_ref.dtype)


def dsa_attention(qabs, qidx, widx, ckv, ckvt, kidx, w_uv, q_abs_norm, kv_norm, *, batch, topk):
    nblk = qabs.shape[0]
    nb = nblk // batch
    l = nb * C_BLOCK
    assert l % KEY_TILE_C == 0 and l >= topk
    ckv = ckv.reshape(batch, l, KV_RANK_C)
    kidx = kidx.reshape(batch, l, IDX_DIM)
    widx = widx.reshape(batch, nb, C_BLOCK, N_IDX_HEADS)
    return pl.pallas_call(
        functools.partial(_dsa_attn_kernel, topk=topk, idx_bits=(l - 1).bit_length()),
        grid=(batch, nb),
        in_specs=[
            pl.BlockSpec((1, N_HEADS_C, C_BLOCK, KV_RANK_C), lambda b, j: (b * nb + j, 0, 0, 0)),
            pl.BlockSpec((1, N_IDX_HEADS, C_BLOCK, IDX_DIM), lambda b, j: (b * nb + j, 0, 0, 0)),
            pl.BlockSpec((1, 1, C_BLOCK, N_IDX_HEADS), lambda b, j: (b, j, 0, 0)),
            pl.BlockSpec((1, l, KV_RANK_C), lambda b, j: (b, 0, 0)),
            pl.BlockSpec((KV_RANK_C, l), lambda b, j: (0, b)),
            pl.BlockSpec((1, l, IDX_DIM), lambda b, j: (b, 0, 0)),
            pl.BlockSpec(w_uv.shape, lambda b, j: (0, 0, 0)),
            pl.BlockSpec((1, KV_RANK_C), lambda b, j: (0, 0)),
            pl.BlockSpec((1, KV_RANK_C), lambda b, j: (0, 0)),
        ],
        out_specs=pl.BlockSpec((1, C_BLOCK, N_HEADS_C * V_HEAD_C), lambda b, j: (b, j, 0)),
        out_shape=jax.ShapeDtypeStruct((batch, l, N_HEADS_C * V_HEAD_C), BF16),
        scratch_shapes=[
            pltpu.VMEM((l, C_BLOCK), jnp.int32),
            pltpu.VMEM((N_HEADS_C // 2, 1, 2 * C_BLOCK), F32),
            pltpu.VMEM((N_HEADS_C // 2, KV_RANK_C + ONES_ROWS_C, 2 * C_BLOCK), F32),
        ],
        compiler_params=_params("parallel", "arbitrary"),
        name="dsa_attention",
    )(qabs, qidx, widx, ckv, ckvt, kidx, w_uv, q_abs_norm.reshape(1, -1), kv_norm.reshape(1, -1))


TM = 512
TF = 512


def _swa_layer(h, batch, norm_mix, w_in, q_norm, k_norm, sinks, wo):
    m, d = h.shape
    qkv = norm_matmul(h, norm_mix, w_in.astype(BF16), tm=TM, tn=w_in.shape[1])
    o = swa_attention(qkv.reshape(batch, m // batch, -1), q_norm, k_norm, sinks)
    return matmul_res(o.reshape(m, -1), wo.astype(BF16), h, tm=TM, tn=d)


def _gla_layer(h, batch, norm_mix, w_in, w_gate_b, gate_bias, o_norm, wo):
    m, d = h.shape
    dk_all = w_gate_b.shape[1]
    dv_all = wo.shape[0]
    n_main = 2 * dk_all + 2 * dv_all
    w_pad = jnp.pad(w_in.astype(BF16), ((0, 0), (0, LANES - GATE_RANK)))
    wgb_pad = jnp.pad(w_gate_b.astype(BF16), ((0, LANES - GATE_RANK), (0, 0)))
    proj = norm_matmul(h, norm_mix, w_pad, tm=2 * TM, tn=(n_main + LANES) // 7)
    o = gla_attention(proj.reshape(batch, m // batch, -1), wgb_pad, gate_bias, o_norm,
                      dk=dk_all // N_HEADS_B, dv=dv_all // N_HEADS_B, tb=256)
    return matmul_res(o.reshape(m, -1), wo.astype(BF16), h, tm=TM, tn=d)


def _dsa_layer(h, batch, norm_mix, w_in, q_lat_norm, kv_norm, w_uq, w_uk, q_abs_norm, w_iq,
               idx_k_norm, idx_k_bias, w_uv, wo):
    m, d = h.shape
    l = m // batch
    qabs, qidx, widx, ckv, ckvt, kidx = dsa_prep(
        h, norm_mix, w_in.astype(BF16), q_lat_norm, kv_norm, w_uq.astype(BF16), w_uk.astype(BF16),
        q_abs_norm, w_iq.astype(BF16), idx_k_norm, idx_k_bias, tm=TM)
    o = dsa_attention(qabs, qidx, widx, ckv, ckvt, kidx, w_uv.astype(BF16), q_abs_norm, kv_norm, batch=batch,
                      topk=min(TOPK_MAX, l // 4))
    return matmul_res(o.reshape(m, -1), wo.astype(BF16), h, tm=TM, tn=d)


def kernel(x, l0_norm_mix, l0_w_in, l0_q_norm, l0_k_norm, l0_sinks, l0_wo, l0_norm_ffn, l0_w_gate_up, l0_w_down, l1_norm_mix, l1_w_in, l1_w_gate_b, l1_gate_bias, l1_o_norm, l1_wo, l1_norm_ffn, l1_w_gate_up, l1_w_down, l2_norm_mix, l2_w_in, l2_q_lat_norm, l2_kv_norm, l2_w_uq, l2_w_uk, l2_q_abs_norm, l2_w_iq, l2_idx_k_norm, l2_idx_k_bias, l2_w_uv, l2_wo, l2_norm_ffn, l2_w_gate_up, l2_w_down, l3_norm_mix, l3_w_in, l3_q_norm, l3_k_norm, l3_sinks, l3_wo, l3_norm_ffn, l3_w_gate_up, l3_w_down):
    batch, seq, d = x.shape
    h = x.reshape(batch * seq, d)

    def channel_mix(h, norm_ffn, w_gate_up, w_down):
        return ffn(h, norm_ffn, w_gate_up, w_down, tm=2 * TM, tf=TF // 2)

    h = _swa_layer(h, batch, l0_norm_mix, l0_w_in, l0_q_norm, l0_k_norm, l0_sinks, l0_wo)
    h = channel_mix(h, l0_norm_ffn, l0_w_gate_up, l0_w_down)
    h = _gla_layer(h, batch, l1_norm_mix, l1_w_in, l1_w_gate_b, l1_gate_bias, l1_o_norm, l1_wo)
    h = channel_mix(h, l1_norm_ffn, l1_w_gate_up, l1_w_down)
    h = _dsa_layer(h, batch, l2_norm_mix, l2_w_in, l2_q_lat_norm, l2_kv_norm, l2_w_uq, l2_w_uk,
                   l2_q_abs_norm, l2_w_iq, l2_idx_k_norm, l2_idx_k_bias, l2_w_uv, l2_wo)
    h = channel_mix(h, l2_norm_ffn, l2_w_gate_up, l2_w_down)
    h = _swa_layer(h, batch, l3_norm_mix, l3_w_in, l3_q_norm, l3_k_norm, l3_sinks, l3_wo)
    h = channel_mix(h, l3_norm_ffn, l3_w_gate_up, l3_w_down)
    return h.reshape(batch, seq, d)
```

```python
import functools

import jax
import jax.numpy as jnp
import numpy as np
from jax import lax
from jax.experimental import pallas as pl
from jax.experimental.pallas import tpu as pltpu

F32 = jnp.float32
BF16 = jnp.bfloat16
EPS = 1e-6
NEG_INF = -1e30
LANES = 128

HD_A = 64
N_KV_A = 4
GROUP_A = 8
BLOCK_A = 128
N_HEADS_B = 4
GATE_RANK = 16
GATE_TAU = 16.0
CHUNK = 64
N_HEADS_C = 16
QK_HEAD_C = 128
V_HEAD_C = 128
Q_RANK_C = 512
KV_RANK_C = 256
N_IDX_HEADS = 16
IDX_DIM = 64
TOPK_MAX = 256
C_BLOCK = 128
KEY_TILE_C = 512
ATTN_AHEAD_C = 2
ONES_ROWS_C = 16
BOUND_SLACK_C = 1.01
MAX_FIXED_OFFSET_C = 56.0
INT_MIN = -2 ** 31
LOG2E = 1.4426950408889634
_NEG_BITS = int(np.float32(NEG_INF).view(np.int32))
NEG_KEY = _NEG_BITS ^ ((_NEG_BITS >> 31) & 0x7FFFFFFF)


def _nt_dot(a, b):
    return lax.dot_general(a, b, (((1,), (1,)), ((), ())), preferred_element_type=F32)


def _tn_dot(a, b):
    return lax.dot_general(a, b, (((0,), (0,)), ((), ())), preferred_element_type=F32)


def _dot(a, b):
    return jnp.dot(a, b, preferred_element_type=F32)


def _rms(x, g):
    return x * lax.rsqrt(jnp.mean(x * x, axis=-1, keepdims=True) + EPS) * g


def _params(*sem, vmem_bytes=None):
    return pltpu.CompilerParams(dimension_semantics=sem, vmem_limit_bytes=vmem_bytes)


def _norm_matmul_kernel(x_ref, g_ref, w_ref, o_ref, xn_ref):
    @pl.when(pl.program_id(1) == 0)
    def _():
        xn_ref[...] = _rms(x_ref[...], g_ref[...]).astype(BF16)

    o_ref[...] = _dot(xn_ref[...], w_ref[...]).astype(o_ref.dtype)


def norm_matmul(x, g, w, *, tm, tn, out_dtype=F32):
    m, d = x.shape
    n = w.shape[1]
    return pl.pallas_call(
        _norm_matmul_kernel,
        grid=(m // tm, n // tn),
        in_specs=[
            pl.BlockSpec((tm, d), lambda i, j: (i, 0)),
            pl.BlockSpec((1, d), lambda i, j: (0, 0)),
            pl.BlockSpec((d, tn), lambda i, j: (0, j)),
        ],
        out_specs=pl.BlockSpec((tm, tn), lambda i, j: (i, j)),
        out_shape=jax.ShapeDtypeStruct((m, n), out_dtype),
        scratch_shapes=[pltpu.VMEM((tm, d), BF16)],
        compiler_params=_params("parallel", "arbitrary"),
        name="norm_matmul",
    )(x, g.reshape(1, d), w)


def _norm_matmul_gate_kernel(x_ref, g_ref, w_ref, wg_ref, o_ref, gl_ref, xn_ref):
    @pl.when(pl.program_id(1) == 0)
    def _():
        xn = _rms(x_ref[...], g_ref[...]).astype(BF16)
        xn_ref[...] = xn
        gl_ref[...] = _dot(xn, wg_ref[...])

    o_ref[...] = _dot(xn_ref[...], w_ref[...].astype(BF16)).astype(o_ref.dtype)


def norm_matmul_gate(x, g, w, n_main, *, tm, tn):
    m, d = x.shape
    w_gate = jnp.pad(w[:, n_main:].astype(BF16), ((0, 0), (0, LANES - (w.shape[1] - n_main))))
    vmem_bytes = (2 * tm * d * 4 + tm * d * 2 + 2 * d * tn * 4 + d * tn * 2
                  + 2 * tm * tn * 2 + tm * tn * 4 + 2 * tm * LANES * 4 + 2 * d * LANES * 2)
    return pl.pallas_call(
        _norm_matmul_gate_kernel,
        grid=(m // tm, n_main // tn),
        in_specs=[
            pl.BlockSpec((tm, d), lambda i, j: (i, 0)),
            pl.BlockSpec((1, d), lambda i, j: (0, 0)),
            pl.BlockSpec((d, tn), lambda i, j: (0, j)),
            pl.BlockSpec((d, LANES), lambda i, j: (0, 0)),
        ],
        out_specs=[
            pl.BlockSpec((tm, tn), lambda i, j: (i, j)),
            pl.BlockSpec((tm, LANES), lambda i, j: (i, 0)),
        ],
        out_shape=[
            jax.ShapeDtypeStruct((m, n_main), BF16),
            jax.ShapeDtypeStruct((m, LANES), F32),
        ],
        scratch_shapes=[pltpu.VMEM((tm, d), BF16)],
        compiler_params=_params("parallel", "arbitrary", vmem_bytes=vmem_bytes),
        name="norm_matmul_gate",
    )(x, g.reshape(1, d), w, w_gate)


def _matmul_res_kernel(a_ref, w_ref, r_ref, o_ref):
    o_ref[...] = r_ref[...] + _dot(a_ref[...], w_ref[...])


def matmul_res(a, w, r, *, tm, tn):
    m, k = a.shape
    n = w.shape[1]
    return pl.pallas_call(
        _matmul_res_kernel,
        grid=(m // tm, n // tn),
        in_specs=[
            pl.BlockSpec((tm, k), lambda i, j: (i, 0)),
            pl.BlockSpec((k, tn), lambda i, j: (0, j)),
            pl.BlockSpec((tm, tn), lambda i, j: (i, j)),
        ],
        out_specs=pl.BlockSpec((tm, tn), lambda i, j: (i, j)),
        out_shape=jax.ShapeDtypeStruct((m, n), F32),
        compiler_params=_params("parallel", "arbitrary"),
        name="matmul_res",
    )(a, w, r)


def _ffn_kernel(x_ref, g_ref, wg_ref, wu_ref, wd_ref, o_ref, xn_ref):
    @pl.when(pl.program_id(1) == 0)
    def _():
        x = x_ref[...]
        xn_ref[...] = _rms(x, g_ref[...]).astype(BF16)
        o_ref[...] = x

    xn = xn_ref[...]
    gate = _dot(xn, wg_ref[...].astype(BF16))
    up = _dot(xn, wu_ref[...].astype(BF16))
    act = (gate * (1.0 / (1.0 + jnp.exp(-gate))) * up).astype(BF16)
    o_ref[...] += _dot(act, wd_ref[...].astype(BF16))


def ffn(x, g, w_gate_up, w_down, *, tm, tf):
    m, d = x.shape
    ff = w_down.shape[0]
    nf = ff // tf
    w_bytes = w_down.dtype.itemsize
    vmem_bytes = (4 * tm * d * 4 + tm * d * 2 + 6 * d * tf * w_bytes
                  + 3 * tm * tf * 4 + 3 * d * tf * 2)
    return pl.pallas_call(
        _ffn_kernel,
        grid=(m // tm, nf),
        in_specs=[
            pl.BlockSpec((tm, d), lambda i, j: (i, 0)),
            pl.BlockSpec((1, d), lambda i, j: (0, 0)),
            pl.BlockSpec((d, tf), lambda i, j: (0, j)),
            pl.BlockSpec((d, tf), lambda i, j: (0, j + nf)),
            pl.BlockSpec((tf, d), lambda i, j: (j, 0)),
        ],
        out_specs=pl.BlockSpec((tm, d), lambda i, j: (i, 0)),
        out_shape=jax.ShapeDtypeStruct((m, d), F32),
        scratch_shapes=[pltpu.VMEM((tm, d), BF16)],
        compiler_params=_params("parallel", "arbitrary", vmem_bytes=vmem_bytes),
        name="ffn",
    )(x, g.reshape(1, d), w_gate_up, w_gate_up, w_down)


def _swa_kernel(sink_ref, cur_ref, prev_ref, qn_ref, kn_ref, sel_ref, o_ref):
    n = pl.program_id(1)
    nq = N_KV_A * GROUP_A * HD_A
    nkv = N_KV_A * HD_A
    kj = lax.broadcasted_iota(jnp.int32, (2 * BLOCK_A, 2 * BLOCK_A), 0)
    qi = lax.broadcasted_iota(jnp.int32, (2 * BLOCK_A, 2 * BLOCK_A), 1) & (BLOCK_A - 1)
    mask2 = (kj > qi) & (kj <= qi + BLOCK_A) & ((n > 0) | (kj >= BLOCK_A))
    first_half = lax.broadcasted_iota(jnp.int32, (1, 2 * BLOCK_A), 1) < BLOCK_A
    q_bf = cur_ref[0, :, :nq]
    xq = q_bf.astype(F32)
    sq = xq * xq
    sq_hi = sq.astype(BF16)
    sq_lo = (sq - sq_hi.astype(F32)).astype(BF16)
    ssq = _dot(sq_hi, sel_ref[...]) + _dot(sq_lo, sel_ref[...])
    rq_t = (lax.rsqrt(ssq * (1.0 / HD_A) + EPS) * LOG2E).T
    kscale = qn_ref[...] * kn_ref[...] * HD_A ** -0.5
    zeros = jnp.zeros((2 * BLOCK_A, HD_A), F32)
    v_all = jnp.concatenate([prev_ref[0, :, nkv:2 * nkv], cur_ref[0, :, nq + nkv:nq + 2 * nkv]], axis=0)
    v_t = v_all.astype(F32).T.astype(BF16)
    k_pads = []
    for h in range(N_KV_A):
        k = jnp.concatenate([prev_ref[0, :, h * HD_A:(h + 1) * HD_A],
                             cur_ref[0, :, nq + h * HD_A:nq + (h + 1) * HD_A]], axis=0).astype(F32)
        kn = k * lax.rsqrt(jnp.mean(k * k, axis=-1, keepdims=True) + EPS) * kscale
        k_pads.append((jnp.concatenate([kn, zeros], axis=1).astype(BF16),
                       jnp.concatenate([zeros, kn], axis=1).astype(BF16)))

    def logits(i):
        j = 2 * i
        k_pad = k_pads[j // GROUP_A]
        q_tile = q_bf[:, j * HD_A:(j + 2) * HD_A]
        rq = jnp.concatenate([rq_t[j:j + 1, :], rq_t[j + 1:j + 2, :]], axis=1)
        s = jnp.concatenate([_nt_dot(k_pad[0], q_tile), _nt_dot(k_pad[1], q_tile)], axis=1) * rq
        return jnp.where(mask2, s, NEG_INF)

    def attend(i, s):
        j = 2 * i
        sink = jnp.where(first_half, sink_ref[j], sink_ref[j + 1]) * LOG2E
        mx = jnp.maximum(jnp.max(s, axis=0, keepdims=True), sink)
        p = jnp.exp2(s - mx)
        den = jnp.sum(p, axis=0, keepdims=True) + jnp.exp2(sink - mx)
        h = j // GROUP_A
        return _dot(v_t[h * HD_A:(h + 1) * HD_A, :], p.astype(BF16)), den

    def finish(i, o_t, den):
        o_t = o_t / den
        o_pair = jnp.concatenate([o_t[:, :BLOCK_A], o_t[:, BLOCK_A:]], axis=0)
        o_ref[0, :, 2 * i * HD_A:(2 * i + 2) * HD_A] = o_pair.T.astype(o_ref.dtype)

    n_pairs = N_KV_A * GROUP_A // 2
    ahead = 3
    pending = [logits(i) for i in range(ahead)]
    unfinished = None
    for i in range(n_pairs):
        if i + ahead < n_pairs:
            pending.append(logits(i + ahead))
        out = attend(i, pending.pop(0))
        if unfinished is not None:
            finish(i - 1, *unfinished)
        unfinished = out
    finish(n_pairs - 1, *unfinished)


def swa_attention(qkv, q_norm, k_norm, sinks):
    b, l, n = qkv.shape
    nq = N_KV_A * GROUP_A * HD_A
    kv_w = 2 * N_KV_A * HD_A
    head_sel = (jnp.arange(nq)[:, None] // HD_A == jnp.arange(LANES)[None, :]).astype(BF16)
    return pl.pallas_call(
        _swa_kernel,
        grid=(b, l // BLOCK_A),
        in_specs=[
            pl.BlockSpec(memory_space=pltpu.SMEM),
            pl.BlockSpec((1, BLOCK_A, n), lambda i, j: (i, j, 0)),
            pl.BlockSpec((1, BLOCK_A, kv_w), lambda i, j: (i, jnp.maximum(j - 1, 0), nq // kv_w)),
            pl.BlockSpec((1, HD_A), lambda i, j: (0, 0)),
            pl.BlockSpec((1, HD_A), lambda i, j: (0, 0)),
            pl.BlockSpec((nq, LANES), lambda i, j: (0, 0)),
        ],
        out_specs=pl.BlockSpec((1, BLOCK_A, nq), lambda i, j: (i, j, 0)),
        out_shape=jax.ShapeDtypeStruct((b, l, nq), BF16),
        compiler_params=_params("parallel", "parallel"),
        name="swa_attention",
    )(sinks, qkv, qkv, q_norm.reshape(1, HD_A), k_norm.reshape(1, HD_A), head_sel)


def _split3(x):
    hi = x.astype(BF16)
    r1 = x - hi.astype(F32)
    mid = r1.astype(BF16)
    lo = (r1 - mid.astype(F32)).astype(BF16)
    return hi, mid, lo


def _gla_kernel(q_ref, k_ref, v_ref, r_ref, gl_ref, wgb_ref, gb_ref, on_ref, o_ref, st_ref,
                *, tb, dk, dv):
    @pl.when(pl.program_id(1) == 0)
    def _():
        st_ref[...] = jnp.zeros_like(st_ref)

    ti = lax.broadcasted_iota(jnp.int32, (CHUNK, CHUNK), 0)
    si = lax.broadcasted_iota(jnp.int32, (CHUNK, CHUNK), 1)
    causal = si <= ti
    tri = causal.astype(BF16)
    wgb = wgb_ref[...]
    gb = gb_ref[...]
    on = on_ref[...]

    def chunk(c, carry):
        rs = pl.ds(pl.multiple_of(c * CHUNK, CHUNK), CHUNK)
        gate = _dot(gl_ref[0, rs, :].astype(BF16), wgb) + gb
        log_a = (jnp.minimum(gate, 0.0) - jnp.log1p(jnp.exp(-jnp.abs(gate)))) / GATE_TAU
        hi, mid, lo = _split3(log_a)
        bcum = _dot(tri, hi) + _dot(tri, mid) + _dot(tri, lo)
        b_last = bcum[CHUNK - 1:CHUNK, :]
        q = q_ref[0, rs, :].astype(F32)
        k = k_ref[0, rs, :].astype(F32)
        q_dec = (q * dk ** -0.5 * jnp.exp(bcum)).astype(BF16)
        k_inv = (k * jnp.exp(-bcum)).astype(BF16)
        k_end = (k * jnp.exp(b_last - bcum)).astype(BF16)
        decay = jnp.exp(b_last)
        for h in range(N_HEADS_B):
            ks = slice(h * dk, (h + 1) * dk)
            vs = slice(h * dv, (h + 1) * dv)
            v = v_ref[0, rs, vs].astype(BF16)
            a = jnp.where(causal, _nt_dot(q_dec[:, ks], k_inv[:, ks]), 0.0)
            st = st_ref[h]
            o = _dot(a.astype(BF16), v) + _nt_dot(q_dec[:, ks], st.astype(BF16))
            st_ref[h] = st * decay[:, ks] + _tn_dot(v, k_end[:, ks])
            o = _rms(o, on)
            r = r_ref[0, rs, vs].astype(F32)
            o_ref[0, rs, vs] = (o * (r * (1.0 / (1.0 + jnp.exp(-r))))).astype(o_ref.dtype)
        return carry

    lax.fori_loop(0, tb // CHUNK, chunk, 0)


def gla_attention(proj, g_low, w_gate_b, gate_bias, o_norm, *, dk, dv, tb):
    b, l, _ = proj.shape
    dk_all, dv_all = N_HEADS_B * dk, N_HEADS_B * dv
    return pl.pallas_call(
        functools.partial(_gla_kernel, tb=tb, dk=dk, dv=dv),
        grid=(b, l // tb),
        in_specs=[
            pl.BlockSpec((1, tb, dk_all), lambda i, j: (i, j, 0)),
            pl.BlockSpec((1, tb, dk_all), lambda i, j: (i, j, 1)),
            pl.BlockSpec((1, tb, dv_all), lambda i, j: (i, j, 2 * dk_all // dv_all)),
            pl.BlockSpec((1, tb, dv_all), lambda i, j: (i, j, 2 * dk_all // dv_all + 1)),
            pl.BlockSpec((1, tb, LANES), lambda i, j: (i, j, 0)),
            pl.BlockSpec((LANES, dk_all), lambda i, j: (0, 0)),
            pl.BlockSpec((1, dk_all), lambda i, j: (0, 0)),
            pl.BlockSpec((1, dv), lambda i, j: (0, 0)),
        ],
        out_specs=pl.BlockSpec((1, tb, dv_all), lambda i, j: (i, j, 0)),
        out_shape=jax.ShapeDtypeStruct((b, l, dv_all), BF16),
        scratch_shapes=[pltpu.VMEM((N_HEADS_B, dv, dk), F32)],
        compiler_params=_params("parallel", "arbitrary"),
        name="gla_attention",
    )(proj, proj, proj, proj, g_low, w_gate_b, gate_bias.reshape(1, dk_all), o_norm.reshape(1, dv))


def _dsa_prep_kernel(x_ref, g_ref, win_ref, qln_ref, kvn_ref, wuq_ref, wuk_ref, qan_ref, wiq_ref,
                     ikn_ref, ikb_ref, qabs_ref, qidx_ref, widx_ref, ckv_ref, ckvt_ref, kidx_ref):
    xn = _rms(x_ref[...], g_ref[...]).astype(BF16)
    proj = _dot(xn, win_ref[...])
    o_kv = Q_RANK_C
    o_ki = Q_RANK_C + KV_RANK_C
    o_wi = o_ki + IDX_DIM
    c_q = _rms(proj[:, :o_kv], qln_ref[...]).astype(BF16)
    c_kv = _rms(proj[:, o_kv:o_ki], kvn_ref[...])
    ckv_ref[...] = c_kv.astype(BF16)
    ckvt_ref[...] = c_kv.T.astype(BF16)
    k_idx = proj[:, o_ki:o_wi]
    mu = jnp.mean(k_idx, axis=-1, keepdims=True)
    kc = k_idx - mu
    k_idx = kc * lax.rsqrt(jnp.mean(kc * kc, axis=-1, keepdims=True) + EPS)
    kidx_ref[...] = (k_idx * ikn_ref[...] + ikb_ref[...]).astype(BF16)
    widx_ref[...] = proj[:, o_wi:o_wi + N_IDX_HEADS] * (N_IDX_HEADS ** -0.5 * IDX_DIM ** -0.5)
    q_nope = _dot(c_q, wuq_ref[...])
    q_idx = _dot(c_q, wiq_ref[...])
    qan = qan_ref[...] * (KV_RANK_C ** -0.5 * LOG2E)
    for h in range(N_HEADS_C):
        qh = q_nope[:, h * QK_HEAD_C:(h + 1) * QK_HEAD_C].astype(BF16)
        qa = _rms(_dot(qh, wuk_ref[h]), qan).astype(BF16)
        qi = q_idx[:, h * IDX_DIM:(h + 1) * IDX_DIM].astype(BF16)
        for j in range(qabs_ref.shape[0]):
            qabs_ref[j, h] = qa[j * C_BLOCK:(j + 1) * C_BLOCK]
            qidx_ref[j, h] = qi[j * C_BLOCK:(j + 1) * C_BLOCK]


def dsa_prep(x, g, w_in, q_lat_norm, kv_norm, w_uq, w_uk, q_abs_norm, w_iq, idx_k_norm, idx_k_bias,
             *, tm):
    m, d = x.shape
    nblk = m // C_BLOCK
    bpt = tm // C_BLOCK
    n_in = w_in.shape[1]
    const2 = lambda i: (0, 0)
    return pl.pallas_call(
        _dsa_prep_kernel,
        grid=(m // tm,),
        in_specs=[
            pl.BlockSpec((tm, d), lambda i: (i, 0)),
            pl.BlockSpec((1, d), const2),
            pl.BlockSpec((d, n_in), const2),
            pl.BlockSpec((1, Q_RANK_C), const2),
            pl.BlockSpec((1, KV_RANK_C), const2),
            pl.BlockSpec(w_uq.shape, const2),
            pl.BlockSpec(w_uk.shape, lambda i: (0, 0, 0)),
            pl.BlockSpec((1, KV_RANK_C), const2),
            pl.BlockSpec(w_iq.shape, const2),
            pl.BlockSpec((1, IDX_DIM), const2),
            pl.BlockSpec((1, IDX_DIM), const2),
        ],
        out_specs=[
            pl.BlockSpec((bpt, N_HEADS_C, C_BLOCK, KV_RANK_C), lambda i: (i, 0, 0, 0)),
            pl.BlockSpec((bpt, N_IDX_HEADS, C_BLOCK, IDX_DIM), lambda i: (i, 0, 0, 0)),
            pl.BlockSpec((tm, N_IDX_HEADS), lambda i: (i, 0)),
            pl.BlockSpec((tm, KV_RANK_C), lambda i: (i, 0)),
            pl.BlockSpec((KV_RANK_C, tm), lambda i: (0, i)),
            pl.BlockSpec((tm, IDX_DIM), lambda i: (i, 0)),
        ],
        out_shape=[
            jax.ShapeDtypeStruct((nblk, N_HEADS_C, C_BLOCK, KV_RANK_C), BF16),
            jax.ShapeDtypeStruct((nblk, N_IDX_HEADS, C_BLOCK, IDX_DIM), BF16),
            jax.ShapeDtypeStruct((m, N_IDX_HEADS), F32),
            jax.ShapeDtypeStruct((m, KV_RANK_C), BF16),
            jax.ShapeDtypeStruct((KV_RANK_C, m), BF16),
            jax.ShapeDtypeStruct((m, IDX_DIM), BF16),
        ],
        compiler_params=_params("parallel"),
        name="dsa_prep",
    )(x, g.reshape(1, d), w_in, q_lat_norm.reshape(1, -1), kv_norm.reshape(1, -1), w_uq, w_uk,
      q_abs_norm.reshape(1, -1), w_iq, idx_k_norm.reshape(1, -1), idx_k_bias.reshape(1, -1))


def _sortable_key(s):
    bits = pltpu.bitcast(s, jnp.int32)
    return bits ^ ((bits >> 31) & 0x7FFFFFFF)


def _dsa_attn_kernel(qabs_ref, qidx_ref, widx_ref, ckv_ref, ckvt_ref, kidx_ref, wuv_ref, qan_ref,
                     kvn_ref, o_ref, key_ref, m_ref, acc_ref, *, topk, idx_bits):
    n = pl.program_id(1)
    tk = KEY_TILE_C
    n_tiles = (n * C_BLOCK + C_BLOCK + tk - 1) // tk
    q_row = n * C_BLOCK + lax.broadcasted_iota(jnp.int32, (C_BLOCK, tk), 0)
    k_col = lax.broadcasted_iota(jnp.int32, (C_BLOCK, tk), 1)
    k_row = lax.broadcasted_iota(jnp.int32, (tk, C_BLOCK), 0)
    widx = widx_ref[0, 0]

    def tile(t):
        return pl.ds(pl.multiple_of(t * tk, tk), tk)

    def score_tile(t, carry):
        kt = kidx_ref[0, tile(t), :]
        score = jnp.zeros((C_BLOCK, tk), F32)
        for h in range(N_IDX_HEADS):
            rel = jnp.maximum(_nt_dot(qidx_ref[0, h], kt), 0.0)
            score = score + rel * widx[:, h:h + 1]
        score = jnp.where(t * tk + k_col <= q_row, score, NEG_INF)
        key_ref[tile(t), :] = _sortable_key(score.T)
        return carry

    lax.fori_loop(0, n_tiles, score_tile, 0)

    def count(pred):
        def body(t, cnt):
            hit = pred(key_ref[tile(t), :], t).astype(jnp.int32)
            return cnt + jnp.sum(hit.reshape(tk // 8, 8, C_BLOCK), axis=0)
        cnt = lax.fori_loop(0, n_tiles, body, jnp.zeros((8, C_BLOCK), jnp.int32))
        return jnp.sum(cnt, axis=0, keepdims=True)

    def bisect(i, lo):
        cand = lo + lax.shift_left(jnp.int32(1), 31 - i)
        return jnp.where(count(lambda k, t: k >= cand) >= topk, cand, lo)

    thresh = lax.fori_loop(0, 32, bisect, jnp.full((1, C_BLOCK), INT_MIN, jnp.int32))

    n_ge = count(lambda k, t: k >= thresh)
    excess = jnp.max(jnp.where((n_ge > topk) & (thresh > NEG_KEY), 1, 0))

    @pl.when(excess > 0)
    def _():
        want = topk - count(lambda k, t: k > thresh)

        def index_bisect(i, j0):
            cand = j0 + lax.shift_left(jnp.int32(1), idx_bits - 1 - i)
            before = count(lambda k, t: (k == thresh) & (t * tk + k_row < cand))
            return jnp.where(before < want, cand, j0)

        j0 = lax.fori_loop(0, idx_bits, index_bisect, jnp.zeros((1, C_BLOCK), jnp.int32))

        def demote(t, carry):
            keys = key_ref[tile(t), :]
            drop = (keys == thresh) & (t * tk + k_row > j0)
            key_ref[tile(t), :] = jnp.where(drop, thresh - 1, keys)
            return carry

        lax.fori_loop(0, n_tiles, demote, 0)

    thr = jnp.maximum(thresh, NEG_KEY + 1)
    thr2 = jnp.concatenate([thr, thr], axis=1)
    acc_ref[...] = jnp.zeros_like(acc_ref)
    ones_rows = jnp.ones((ONES_ROWS_C, tk), BF16)
    n_pairs = N_HEADS_C // 2

    def pair_q(p):
        return qabs_ref[0, 2 * p:2 * p + 2].reshape(2 * C_BLOCK, KV_RANK_C)

    bound = (KV_RANK_C * BOUND_SLACK_C * (KV_RANK_C ** -0.5 * LOG2E)
             * jnp.max(jnp.abs(qan_ref[...])) * jnp.max(jnp.abs(kvn_ref[...])))
    small_logits = bound <= MAX_FIXED_OFFSET_C

    def attn_loop(accumulate):
        def attn_tile(t, carry):
            kv = ckv_ref[0, tile(t), :]
            kvt = jnp.concatenate([ckvt_ref[:, tile(t)], ones_rows], axis=0)
            keys = key_ref[tile(t), :]
            sel = jnp.concatenate([keys, keys], axis=1) >= thr2

            def logits(p):
                return jnp.where(sel, _nt_dot(kv, pair_q(p)), NEG_INF)

            pending = [logits(p) for p in range(ATTN_AHEAD_C)]
            for p in range(n_pairs):
                if p + ATTN_AHEAD_C < n_pairs:
                    pending.append(logits(p + ATTN_AHEAD_C))
                accumulate(p, pending.pop(0), kvt)
            return carry

        lax.fori_loop(0, n_tiles, attn_tile, 0)

    def accumulate_fixed(p, s, kvt):
        acc_ref[p] += _dot(kvt, jnp.exp2(s - bound).astype(BF16))

    def accumulate_online(p, s, kvt):
        m_old = m_ref[p]
        m_new = jnp.maximum(m_old, jnp.max(s, axis=0, keepdims=True))
        pe = jnp.exp2(s - m_new)
        acc_ref[p] = jnp.exp2(m_old - m_new) * acc_ref[p] + _dot(kvt, pe.astype(BF16))
        m_ref[p] = m_new

    @pl.when(small_logits)
    def _():
        attn_loop(accumulate_fixed)

    @pl.when(jnp.logical_not(small_logits))
    def _():
        m_ref[...] = jnp.full(m_ref.shape, NEG_INF, F32)
        attn_loop(accumulate_online)

    for p in range(N_HEADS_C // 2):
        denom = acc_ref[p, KV_RANK_C:KV_RANK_C + 1, :]
        o_lat_t = (acc_ref[p, :KV_RANK_C, :] / denom).astype(BF16)
        for e in range(2):
            h = 2 * p + e
            o_ref[0, :, h * V_HEAD_C:(h + 1) * V_HEAD_C] = _tn_dot(
                o_lat_t[:, e * C_BLOCK:(e + 1) * C_BLOCK], wuv_ref[h]).astype(o_ref.dtype)


def dsa_attention(qabs, qidx, widx, ckv, ckvt, kidx, w_uv, q_abs_norm, kv_norm, *, batch, topk):
    nblk = qabs.shape[0]
    nb = nblk // batch
    l = nb * C_BLOCK
    assert l % KEY_TILE_C == 0 and l >= topk
    ckv = ckv.reshape(batch, l, KV_RANK_C)
    kidx = kidx.reshape(batch, l, IDX_DIM)
    widx = widx.reshape(batch, nb, C_BLOCK, N_IDX_HEADS)
    return pl.pallas_call(
        functools.partial(_dsa_attn_kernel, topk=topk, idx_bits=(l - 1).bit_length()),
        grid=(batch, nb),
        in_specs=[
            pl.BlockSpec((1, N_HEADS_C, C_BLOCK, KV_RANK_C), lambda b, j: (b * nb + j, 0, 0, 0)),
            pl.BlockSpec((1, N_IDX_HEADS, C_BLOCK, IDX_DIM), lambda b, j: (b * nb + j, 0, 0, 0)),
            pl.BlockSpec((1, 1, C_BLOCK, N_IDX_HEADS), lambda b, j: (b, j, 0, 0)),
            pl.BlockSpec((1, l, KV_RANK_C), lambda b, j: (b, 0, 0)),
            pl.BlockSpec((KV_RANK_C, l), lambda b, j: (0, b)),
            pl.BlockSpec((1, l, IDX_DIM), lambda b, j: (b, 0, 0)),
            pl.BlockSpec(w_uv.shape, lambda b, j: (0, 0, 0)),
            pl.BlockSpec((1, KV_RANK_C), lambda b, j: (0, 0)),
            pl.BlockSpec((1, KV_RANK_C), lambda b, j: (0, 0)),
        ],
        out_specs=pl.BlockSpec((1, C_BLOCK, N_HEADS_C * V_HEAD_C), lambda b, j: (b, j, 0)),
        out_shape=jax.ShapeDtypeStruct((batch, l, N_HEADS_C * V_HEAD_C), BF16),
        scratch_shapes=[
            pltpu.VMEM((l, C_BLOCK), jnp.int32),
            pltpu.VMEM((N_HEADS_C // 2, 1, 2 * C_BLOCK), F32),
            pltpu.VMEM((N_HEADS_C // 2, KV_RANK_C + ONES_ROWS_C, 2 * C_BLOCK), F32),
        ],
        compiler_params=_params("parallel", "arbitrary"),
        name="dsa_attention",
    )(qabs, qidx, widx, ckv, ckvt, kidx, w_uv, q_abs_norm.reshape(1, -1), kv_norm.reshape(1, -1))


TM = 512
TF = 512


def _swa_layer(h, batch, norm_mix, w_in, q_norm, k_norm, sinks, wo):
    m, d = h.shape
    qkv = norm_matmul(h, norm_mix, w_in.astype(BF16), tm=TM, tn=w_in.shape[1], out_dtype=BF16)
    o = swa_attention(qkv.reshape(batch, m // batch, -1), q_norm, k_norm, sinks)
    return matmul_res(o.reshape(m, -1), wo.astype(BF16), h, tm=TM, tn=d)


def _gla_layer(h, batch, norm_mix, w_in, w_gate_b, gate_bias, o_norm, wo):
    m, d = h.shape
    dk_all = w_gate_b.shape[1]
    dv_all = wo.shape[0]
    n_main = 2 * dk_all + 2 * dv_all
    wgb_pad = jnp.pad(w_gate_b.astype(BF16), ((0, LANES - GATE_RANK), (0, 0)))
    proj, g_low = norm_matmul_gate(h, norm_mix, w_in, n_main, tm=2 * TM, tn=n_main // 8)
    o = gla_attention(proj.reshape(batch, m // batch, -1), g_low.reshape(batch, m // batch, -1),
                      wgb_pad, gate_bias, o_norm,
                      dk=dk_all // N_HEADS_B, dv=dv_all // N_HEADS_B, tb=256)
    return matmul_res(o.reshape(m, -1), wo.astype(BF16), h, tm=TM, tn=d)


def _dsa_layer(h, batch, norm_mix, w_in, q_lat_norm, kv_norm, w_uq, w_uk, q_abs_norm, w_iq,
               idx_k_norm, idx_k_bias, w_uv, wo):
    m, d = h.shape
    l = m // batch
    qabs, qidx, widx, ckv, ckvt, kidx = dsa_prep(
        h, norm_mix, w_in.astype(BF16), q_lat_norm, kv_norm, w_uq.astype(BF16), w_uk.astype(BF16),
        q_abs_norm, w_iq.astype(BF16), idx_k_norm, idx_k_bias, tm=TM)
    o = dsa_attention(qabs, qidx, widx, ckv, ckvt, kidx, w_uv.astype(BF16), q_abs_norm, kv_norm, batch=batch,
                      topk=min(TOPK_MAX, l // 4))
    return matmul_res(o.reshape(m, -1), wo.astype(BF16), h, tm=TM, tn=d)


def kernel(x, l0_norm_mix, l0_w_in, l0_q_norm, l0_k_norm, l0_sinks, l0_wo, l0_norm_ffn, l0_w_gate_up, l0_w_down, l1_norm_mix, l1_w_in, l1_w_gate_b, l1_gate_bias, l1_o_norm, l1_wo, l1_norm_ffn, l1_w_gate_up, l1_w_down, l2_norm_mix, l2_w_in, l2_q_lat_norm, l2_kv_norm, l2_w_uq, l2_w_uk, l2_q_abs_norm, l2_w_iq, l2_idx_k_norm, l2_idx_k_bias, l2_w_uv, l2_wo, l2_norm_ffn, l2_w_gate_up, l2_w_down, l3_norm_mix, l3_w_in, l3_q_norm, l3_k_norm, l3_sinks, l3_wo, l3_norm_ffn, l3_w_gate_up, l3_w_down):
    batch, seq, d = x.shape
    h = x.reshape(batch * seq, d)

    def channel_mix(h, norm_ffn, w_gate_up, w_down):
        return ffn(h, norm_ffn, w_gate_up, w_down, tm=2 * TM, tf=TF // 2)

    h = _swa_layer(h, batch, l0_norm_mix, l0_w_in, l0_q_norm, l0_k_norm, l0_sinks, l0_wo)
    h = channel_mix(h, l0_norm_ffn, l0_w_gate_up, l0_w_down)
    h = _gla_layer(h, batch, l1_norm_mix, l1_w_in, l1_w_gate_b, l1_gate_bias, l1_o_norm, l1_wo)
    h = channel_mix(h, l1_norm_ffn, l1_w_gate_up, l1_w_down)
    h = _dsa_layer(h, batch, l2_norm_mix, l2_w_in, l2_q_lat_norm, l2_kv_norm, l2_w_uq, l2_w_uk,
                   l2_q_abs_norm, l2_w_iq, l2_idx_k_norm, l2_idx_k_bias, l2_w_uv, l2_wo)
    h = channel_mix(h, l2_norm_ffn, l2_w_gate_up, l2_w_down)
    h = _swa_layer(h, batch, l3_norm_mix, l3_w_in, l3_q_norm, l3_k_norm, l3_sinks, l3_wo)
    h = channel_mix(h, l3_norm_ffn, l3_w_gate_up, l3_w_down)
    return h.reshape(batch, seq, d)
```

```python
import functools

import jax
import jax.numpy as jnp
import numpy as np
from jax import lax
from jax.experimental import pallas as pl
from jax.experimental.pallas import tpu as pltpu

F32 = jnp.float32
BF16 = jnp.bfloat16
EPS = 1e-6
NEG_INF = -1e30
LANES = 128

HD_A = 64
N_KV_A = 4
GROUP_A = 8
BLOCK_A = 128
N_HEADS_B = 4
GATE_RANK = 16
GATE_TAU = 16.0
CHUNK = 64
N_HEADS_C = 16
QK_HEAD_C = 128
V_HEAD_C = 128
Q_RANK_C = 512
KV_RANK_C = 256
N_IDX_HEADS = 16
IDX_DIM = 64
TOPK_MAX = 256
C_BLOCK = 128
KEY_TILE_C = 512
ATTN_AHEAD_C = 2
ONES_ROWS_C = 16
BOUND_SLACK_C = 1.01
MAX_FIXED_OFFSET_C = 56.0
INT_MIN = -2 ** 31
INT16_MIN = -2 ** 15
LOG2E = 1.4426950408889634
_NEG_BITS = int(np.float32(NEG_INF).view(np.int32))
NEG_KEY = _NEG_BITS ^ ((_NEG_BITS >> 31) & 0x7FFFFFFF)


def _nt_dot(a, b):
    return lax.dot_general(a, b, (((1,), (1,)), ((), ())), preferred_element_type=F32)


def _tn_dot(a, b):
    return lax.dot_general(a, b, (((0,), (0,)), ((), ())), preferred_element_type=F32)


def _dot(a, b):
    return jnp.dot(a, b, preferred_element_type=F32)


def _rms(x, g):
    return x * lax.rsqrt(jnp.mean(x * x, axis=-1, keepdims=True) + EPS) * g


def _params(*sem, vmem_bytes=None):
    return pltpu.CompilerParams(dimension_semantics=sem, vmem_limit_bytes=vmem_bytes)


def _norm_matmul_kernel(x_ref, g_ref, w_ref, o_ref, xn_ref):
    @pl.when(pl.program_id(1) == 0)
    def _():
        xn_ref[...] = _rms(x_ref[...], g_ref[...]).astype(BF16)

    o_ref[...] = _dot(xn_ref[...], w_ref[...]).astype(o_ref.dtype)


def norm_matmul(x, g, w, *, tm, tn, out_dtype=F32):
    m, d = x.shape
    n = w.shape[1]
    return pl.pallas_call(
        _norm_matmul_kernel,
        grid=(m // tm, n // tn),
        in_specs=[
            pl.BlockSpec((tm, d), lambda i, j: (i, 0)),
            pl.BlockSpec((1, d), lambda i, j: (0, 0)),
            pl.BlockSpec((d, tn), lambda i, j: (0, j)),
        ],
        out_specs=pl.BlockSpec((tm, tn), lambda i, j: (i, j)),
        out_shape=jax.ShapeDtypeStruct((m, n), out_dtype),
        scratch_shapes=[pltpu.VMEM((tm, d), BF16)],
        compiler_params=_params("parallel", "arbitrary"),
        name="norm_matmul",
    )(x, g.reshape(1, d), w)


def _norm_matmul_gate_kernel(x_ref, g_ref, w_ref, wg_ref, o_ref, gl_ref, xn_ref):
    @pl.when(pl.program_id(1) == 0)
    def _():
        xn = _rms(x_ref[...], g_ref[...]).astype(BF16)
        xn_ref[...] = xn
        gl_ref[...] = _nt_dot(xn, wg_ref[...])

    o_ref[...] = _nt_dot(xn_ref[...], w_ref[...].astype(BF16)).astype(o_ref.dtype)


def norm_matmul_gate(x, g, w, n_main, *, tm, tn):
    m, d = x.shape
    w_t = w.T
    w_gate = jnp.pad(w_t[n_main:].astype(BF16), ((0, LANES - (w.shape[1] - n_main)), (0, 0)))
    vmem_bytes = (2 * tm * d * 4 + tm * d * 2 + 2 * d * tn * 4 + d * tn * 2
                  + 2 * tm * tn * 2 + tm * tn * 4 + 2 * tm * LANES * 4 + 2 * d * LANES * 2)
    return pl.pallas_call(
        _norm_matmul_gate_kernel,
        grid=(m // tm, n_main // tn),
        in_specs=[
            pl.BlockSpec((tm, d), lambda i, j: (i, 0)),
            pl.BlockSpec((1, d), lambda i, j: (0, 0)),
            pl.BlockSpec((tn, d), lambda i, j: (j, 0)),
            pl.BlockSpec((LANES, d), lambda i, j: (0, 0)),
        ],
        out_specs=[
            pl.BlockSpec((tm, tn), lambda i, j: (i, j)),
            pl.BlockSpec((tm, LANES), lambda i, j: (i, 0)),
        ],
        out_shape=[
            jax.ShapeDtypeStruct((m, n_main), BF16),
            jax.ShapeDtypeStruct((m, LANES), F32),
        ],
        scratch_shapes=[pltpu.VMEM((tm, d), BF16)],
        compiler_params=_params("parallel", "arbitrary", vmem_bytes=vmem_bytes),
        name="norm_matmul_gate",
    )(x, g.reshape(1, d), w_t, w_gate)


def _matmul_res_kernel(a_ref, w_ref, r_ref, o_ref):
    o_ref[...] = r_ref[...] + _dot(a_ref[...], w_ref[...])


def matmul_res(a, w, r, *, tm, tn):
    m, k = a.shape
    n = w.shape[1]
    return pl.pallas_call(
        _matmul_res_kernel,
        grid=(m // tm, n // tn),
        in_specs=[
            pl.BlockSpec((tm, k), lambda i, j: (i, 0)),
            pl.BlockSpec((k, tn), lambda i, j: (0, j)),
            pl.BlockSpec((tm, tn), lambda i, j: (i, j)),
        ],
        out_specs=pl.BlockSpec((tm, tn), lambda i, j: (i, j)),
        out_shape=jax.ShapeDtypeStruct((m, n), F32),
        compiler_params=_params("parallel", "arbitrary"),
        name="matmul_res",
    )(a, w, r)


def _ffn_kernel(x_ref, g_ref, wg_ref, wu_ref, wd_ref, o_ref, xn_ref):
    @pl.when(pl.program_id(1) == 0)
    def _():
        x = x_ref[...]
        xn_ref[...] = _rms(x, g_ref[...]).astype(BF16)
        o_ref[...] = x

    xn = xn_ref[...]
    gate = _dot(xn, wg_ref[...].astype(BF16))
    up = _dot(xn, wu_ref[...].astype(BF16))
    act = (gate * (1.0 / (1.0 + jnp.exp(-gate))) * up).astype(BF16)
    o_ref[...] += _dot(act, wd_ref[...].astype(BF16))


def ffn(x, g, w_gate_up, w_down, *, tm, tf):
    m, d = x.shape
    ff = w_down.shape[0]
    nf = ff // tf
    w_bytes = w_down.dtype.itemsize
    vmem_bytes = (4 * tm * d * 4 + tm * d * 2 + 6 * d * tf * w_bytes
                  + 3 * tm * tf * 4 + 3 * d * tf * 2)
    return pl.pallas_call(
        _ffn_kernel,
        grid=(m // tm, nf),
        in_specs=[
            pl.BlockSpec((tm, d), lambda i, j: (i, 0)),
            pl.BlockSpec((1, d), lambda i, j: (0, 0)),
            pl.BlockSpec((d, tf), lambda i, j: (0, j)),
            pl.BlockSpec((d, tf), lambda i, j: (0, j + nf)),
            pl.BlockSpec((tf, d), lambda i, j: (j, 0)),
        ],
        out_specs=pl.BlockSpec((tm, d), lambda i, j: (i, 0)),
        out_shape=jax.ShapeDtypeStruct((m, d), F32),
        scratch_shapes=[pltpu.VMEM((tm, d), BF16)],
        compiler_params=_params("parallel", "arbitrary", vmem_bytes=vmem_bytes),
        name="ffn",
    )(x, g.reshape(1, d), w_gate_up, w_gate_up, w_down)


def _swa_kernel(sink_ref, cur_ref, prev_ref, qn_ref, kn_ref, sel_ref, o_ref):
    n = pl.program_id(1)
    nq = N_KV_A * GROUP_A * HD_A
    nkv = N_KV_A * HD_A
    kj = lax.broadcasted_iota(jnp.int32, (2 * BLOCK_A, 2 * BLOCK_A), 0)
    qi = lax.broadcasted_iota(jnp.int32, (2 * BLOCK_A, 2 * BLOCK_A), 1) & (BLOCK_A - 1)
    mask2 = (kj > qi) & (kj <= qi + BLOCK_A) & ((n > 0) | (kj >= BLOCK_A))
    first_half = lax.broadcasted_iota(jnp.int32, (1, 2 * BLOCK_A), 1) < BLOCK_A
    q_bf = cur_ref[0, :, :nq]
    xq = q_bf.astype(F32)
    sq = xq * xq
    sq_hi = sq.astype(BF16)
    sq_lo = (sq - sq_hi.astype(F32)).astype(BF16)
    ssq = _dot(sq_hi, sel_ref[...]) + _dot(sq_lo, sel_ref[...])
    rq_t = (lax.rsqrt(ssq * (1.0 / HD_A) + EPS) * LOG2E).T
    kscale = qn_ref[...] * kn_ref[...] * HD_A ** -0.5
    zeros = jnp.zeros((2 * BLOCK_A, HD_A), F32)
    v_all = jnp.concatenate([prev_ref[0, :, nkv:2 * nkv], cur_ref[0, :, nq + nkv:nq + 2 * nkv]], axis=0)
    v_t = v_all.astype(F32).T.astype(BF16)
    k_pads = []
    for h in range(N_KV_A):
        k = jnp.concatenate([prev_ref[0, :, h * HD_A:(h + 1) * HD_A],
                             cur_ref[0, :, nq + h * HD_A:nq + (h + 1) * HD_A]], axis=0).astype(F32)
        kn = k * lax.rsqrt(jnp.mean(k * k, axis=-1, keepdims=True) + EPS) * kscale
        k_pads.append((jnp.concatenate([kn, zeros], axis=1).astype(BF16),
                       jnp.concatenate([zeros, kn], axis=1).astype(BF16)))

    def logits(i):
        j = 2 * i
        k_pad = k_pads[j // GROUP_A]
        q_tile = q_bf[:, j * HD_A:(j + 2) * HD_A]
        rq = jnp.concatenate([rq_t[j:j + 1, :], rq_t[j + 1:j + 2, :]], axis=1)
        s = jnp.concatenate([_nt_dot(k_pad[0], q_tile), _nt_dot(k_pad[1], q_tile)], axis=1) * rq
        return jnp.where(mask2, s, NEG_INF)

    def attend(i, s):
        j = 2 * i
        sink = jnp.where(first_half, sink_ref[j], sink_ref[j + 1]) * LOG2E
        mx = jnp.maximum(jnp.max(s, axis=0, keepdims=True), sink)
        p = jnp.exp2(s - mx)
        den = jnp.sum(p, axis=0, keepdims=True) + jnp.exp2(sink - mx)
        h = j // GROUP_A
        return _dot(v_t[h * HD_A:(h + 1) * HD_A, :], p.astype(BF16)), den

    def finish(i, o_t, den):
        o_t = o_t / den
        o_pair = jnp.concatenate([o_t[:, :BLOCK_A], o_t[:, BLOCK_A:]], axis=0)
        o_ref[0, :, 2 * i * HD_A:(2 * i + 2) * HD_A] = o_pair.T.astype(o_ref.dtype)

    n_pairs = N_KV_A * GROUP_A // 2
    ahead = 3
    pending = [logits(i) for i in range(ahead)]
    unfinished = None
    for i in range(n_pairs):
        if i + ahead < n_pairs:
            pending.append(logits(i + ahead))
        out = attend(i, pending.pop(0))
        if unfinished is not None:
            finish(i - 1, *unfinished)
        unfinished = out
    finish(n_pairs - 1, *unfinished)


def swa_attention(qkv, q_norm, k_norm, sinks):
    b, l, n = qkv.shape
    nq = N_KV_A * GROUP_A * HD_A
    kv_w = 2 * N_KV_A * HD_A
    head_sel = (jnp.arange(nq)[:, None] // HD_A == jnp.arange(LANES)[None, :]).astype(BF16)
    return pl.pallas_call(
        _swa_kernel,
        grid=(b, l // BLOCK_A),
        in_specs=[
            pl.BlockSpec(memory_space=pltpu.SMEM),
            pl.BlockSpec((1, BLOCK_A, n), lambda i, j: (i, j, 0)),
            pl.BlockSpec((1, BLOCK_A, kv_w), lambda i, j: (i, jnp.maximum(j - 1, 0), nq // kv_w)),
            pl.BlockSpec((1, HD_A), lambda i, j: (0, 0)),
            pl.BlockSpec((1, HD_A), lambda i, j: (0, 0)),
            pl.BlockSpec((nq, LANES), lambda i, j: (0, 0)),
        ],
        out_specs=pl.BlockSpec((1, BLOCK_A, nq), lambda i, j: (i, j, 0)),
        out_shape=jax.ShapeDtypeStruct((b, l, nq), BF16),
        compiler_params=_params("parallel", "parallel"),
        name="swa_attention",
    )(sinks, qkv, qkv, q_norm.reshape(1, HD_A), k_norm.reshape(1, HD_A), head_sel)


def _split3(x):
    hi = x.astype(BF16)
    r1 = x - hi.astype(F32)
    mid = r1.astype(BF16)
    lo = (r1 - mid.astype(F32)).astype(BF16)
    return hi, mid, lo


def _gla_kernel(q_ref, k_ref, v_ref, r_ref, gl_ref, wgb_ref, gb_ref, on_ref, o_ref, st_ref,
                *, tb, dk, dv):
    @pl.when(pl.program_id(1) == 0)
    def _():
        st_ref[...] = jnp.zeros_like(st_ref)

    ti = lax.broadcasted_iota(jnp.int32, (CHUNK, CHUNK), 0)
    si = lax.broadcasted_iota(jnp.int32, (CHUNK, CHUNK), 1)
    causal = si <= ti
    tri = causal.astype(BF16)
    wgb = wgb_ref[...]
    gb = gb_ref[...]
    on = on_ref[...]

    def chunk(c, carry):
        rs = pl.ds(pl.multiple_of(c * CHUNK, CHUNK), CHUNK)
        gate = _dot(gl_ref[0, rs, :].astype(BF16), wgb) + gb
        log_a = (jnp.minimum(gate, 0.0) - jnp.log1p(jnp.exp(-jnp.abs(gate)))) / GATE_TAU
        hi, mid, lo = _split3(log_a)
        bcum = _dot(tri, hi) + _dot(tri, mid) + _dot(tri, lo)
        b_last = bcum[CHUNK - 1:CHUNK, :]
        q = q_ref[0, rs, :].astype(F32)
        k = k_ref[0, rs, :].astype(F32)
        q_dec = (q * dk ** -0.5 * jnp.exp(bcum)).astype(BF16)
        k_inv = (k * jnp.exp(-bcum)).astype(BF16)
        k_end = (k * jnp.exp(b_last - bcum)).astype(BF16)
        decay = jnp.exp(b_last)
        for h in range(N_HEADS_B):
            ks = slice(h * dk, (h + 1) * dk)
            vs = slice(h * dv, (h + 1) * dv)
            v = v_ref[0, rs, vs].astype(BF16)
            a = jnp.where(causal, _nt_dot(q_dec[:, ks], k_inv[:, ks]), 0.0)
            st = st_ref[h]
            o = _dot(a.astype(BF16), v) + _nt_dot(q_dec[:, ks], st.astype(BF16))
            st_ref[h] = st * decay[:, ks] + _tn_dot(v, k_end[:, ks])
            o = _rms(o, on)
            r = r_ref[0, rs, vs].astype(F32)
            o_ref[0, rs, vs] = (o * (r * (1.0 / (1.0 + jnp.exp(-r))))).astype(o_ref.dtype)
        return carry

    lax.fori_loop(0, tb // CHUNK, chunk, 0)


def gla_attention(proj, g_low, w_gate_b, gate_bias, o_norm, *, dk, dv, tb):
    b, l, _ = proj.shape
    dk_all, dv_all = N_HEADS_B * dk, N_HEADS_B * dv
    return pl.pallas_call(
        functools.partial(_gla_kernel, tb=tb, dk=dk, dv=dv),
        grid=(b, l // tb),
        in_specs=[
            pl.BlockSpec((1, tb, dk_all), lambda i, j: (i, j, 0)),
            pl.BlockSpec((1, tb, dk_all), lambda i, j: (i, j, 1)),
            pl.BlockSpec((1, tb, dv_all), lambda i, j: (i, j, 2 * dk_all // dv_all)),
            pl.BlockSpec((1, tb, dv_all), lambda i, j: (i, j, 2 * dk_all // dv_all + 1)),
            pl.BlockSpec((1, tb, LANES), lambda i, j: (i, j, 0)),
            pl.BlockSpec((LANES, dk_all), lambda i, j: (0, 0)),
            pl.BlockSpec((1, dk_all), lambda i, j: (0, 0)),
            pl.BlockSpec((1, dv), lambda i, j: (0, 0)),
        ],
        out_specs=pl.BlockSpec((1, tb, dv_all), lambda i, j: (i, j, 0)),
        out_shape=jax.ShapeDtypeStruct((b, l, dv_all), BF16),
        scratch_shapes=[pltpu.VMEM((N_HEADS_B, dv, dk), F32)],
        compiler_params=_params("parallel", "arbitrary"),
        name="gla_attention",
    )(proj, proj, proj, proj, g_low, w_gate_b, gate_bias.reshape(1, dk_all), o_norm.reshape(1, dv))


def _dsa_prep_kernel(x_ref, g_ref, win_ref, qln_ref, kvn_ref, wuq_ref, wuk_ref, qan_ref, wiq_ref,
                     ikn_ref, ikb_ref, qabs_ref, qidx_ref, widx_ref, ckv_ref, ckvt_ref, kidx_ref):
    xn = _rms(x_ref[...], g_ref[...]).astype(BF16)
    proj = _nt_dot(xn, win_ref[...])
    o_kv = Q_RANK_C
    o_ki = Q_RANK_C + KV_RANK_C
    o_wi = o_ki + IDX_DIM
    c_q = _rms(proj[:, :o_kv], qln_ref[...]).astype(BF16)
    c_kv = _rms(proj[:, o_kv:o_ki], kvn_ref[...])
    ckv_ref[...] = c_kv.astype(BF16)
    ckvt_ref[...] = c_kv.T.astype(BF16)
    k_idx = proj[:, o_ki:o_wi]
    mu = jnp.mean(k_idx, axis=-1, keepdims=True)
    kc = k_idx - mu
    k_idx = kc * lax.rsqrt(jnp.mean(kc * kc, axis=-1, keepdims=True) + EPS)
    kidx_ref[...] = (k_idx * ikn_ref[...] + ikb_ref[...]).astype(BF16)
    widx_ref[...] = proj[:, o_wi:o_wi + N_IDX_HEADS] * (N_IDX_HEADS ** -0.5 * IDX_DIM ** -0.5)
    q_nope = _dot(c_q, wuq_ref[...])
    q_idx = _dot(c_q, wiq_ref[...])
    qan = qan_ref[...] * (KV_RANK_C ** -0.5 * LOG2E)
    for h in range(N_HEADS_C):
        qh = q_nope[:, h * QK_HEAD_C:(h + 1) * QK_HEAD_C].astype(BF16)
        qa = _rms(_dot(qh, wuk_ref[h]), qan).astype(BF16)
        qi = q_idx[:, h * IDX_DIM:(h + 1) * IDX_DIM].astype(BF16)
        for j in range(qabs_ref.shape[0]):
            qabs_ref[j, h] = qa[j * C_BLOCK:(j + 1) * C_BLOCK]
            qidx_ref[j, h] = qi[j * C_BLOCK:(j + 1) * C_BLOCK]


def dsa_prep(x, g, w_in, q_lat_norm, kv_norm, w_uq, w_uk, q_abs_norm, w_iq, idx_k_norm, idx_k_bias,
             *, tm):
    m, d = x.shape
    nblk = m // C_BLOCK
    bpt = tm // C_BLOCK
    n_in = w_in.shape[0]
    const2 = lambda i: (0, 0)
    return pl.pallas_call(
        _dsa_prep_kernel,
        grid=(m // tm,),
        in_specs=[
            pl.BlockSpec((tm, d), lambda i: (i, 0)),
            pl.BlockSpec((1, d), const2),
            pl.BlockSpec((n_in, d), const2),
            pl.BlockSpec((1, Q_RANK_C), const2),
            pl.BlockSpec((1, KV_RANK_C), const2),
            pl.BlockSpec(w_uq.shape, const2),
            pl.BlockSpec(w_uk.shape, lambda i: (0, 0, 0)),
            pl.BlockSpec((1, KV_RANK_C), const2),
            pl.BlockSpec(w_iq.shape, const2),
            pl.BlockSpec((1, IDX_DIM), const2),
            pl.BlockSpec((1, IDX_DIM), const2),
        ],
        out_specs=[
            pl.BlockSpec((bpt, N_HEADS_C, C_BLOCK, KV_RANK_C), lambda i: (i, 0, 0, 0)),
            pl.BlockSpec((bpt, N_IDX_HEADS, C_BLOCK, IDX_DIM), lambda i: (i, 0, 0, 0)),
            pl.BlockSpec((tm, N_IDX_HEADS), lambda i: (i, 0)),
            pl.BlockSpec((tm, KV_RANK_C), lambda i: (i, 0)),
            pl.BlockSpec((KV_RANK_C, tm), lambda i: (0, i)),
            pl.BlockSpec((tm, IDX_DIM), lambda i: (i, 0)),
        ],
        out_shape=[
            jax.ShapeDtypeStruct((nblk, N_HEADS_C, C_BLOCK, KV_RANK_C), BF16),
            jax.ShapeDtypeStruct((nblk, N_IDX_HEADS, C_BLOCK, IDX_DIM), BF16),
            jax.ShapeDtypeStruct((m, N_IDX_HEADS), F32),
            jax.ShapeDtypeStruct((m, KV_RANK_C), BF16),
            jax.ShapeDtypeStruct((KV_RANK_C, m), BF16),
            jax.ShapeDtypeStruct((m, IDX_DIM), BF16),
        ],
        compiler_params=_params("parallel"),
        name="dsa_prep",
    )(x, g.reshape(1, d), w_in, q_lat_norm.reshape(1, -1), kv_norm.reshape(1, -1), w_uq, w_uk,
      q_abs_norm.reshape(1, -1), w_iq, idx_k_norm.reshape(1, -1), idx_k_bias.reshape(1, -1))


def _sortable_key(s):
    bits = pltpu.bitcast(s, jnp.int32)
    return bits ^ ((bits >> 31) & 0x7FFFFFFF)


def _dsa_attn_kernel(qabs_ref, qidx_ref, widx_ref, ckv_ref, ckvt_ref, kidx_ref, wuv_ref, qan_ref,
                     kvn_ref, o_ref, key_ref, half_ref, m_ref, acc_ref, *, topk, idx_bits):
    n = pl.program_id(1)
    tk = KEY_TILE_C
    n_tiles = (n * C_BLOCK + C_BLOCK + tk - 1) // tk
    q_row = n * C_BLOCK + lax.broadcasted_iota(jnp.int32, (C_BLOCK, tk), 0)
    k_col = lax.broadcasted_iota(jnp.int32, (C_BLOCK, tk), 1)
    k_row = lax.broadcasted_iota(jnp.int32, (tk, C_BLOCK), 0)
    widx = widx_ref[0, 0]

    def tile(t):
        return pl.ds(pl.multiple_of(t * tk, tk), tk)

    def score_tile(t, carry):
        kt = kidx_ref[0, tile(t), :]
        score = jnp.zeros((C_BLOCK, tk), F32)
        for h in range(N_IDX_HEADS):
            rel = jnp.maximum(_nt_dot(qidx_ref[0, h], kt), 0.0)
            score = score + rel * widx[:, h:h + 1]
        score = jnp.where(t * tk + k_col <= q_row, score, NEG_INF)
        key = _sortable_key(score.T)
        key_ref[tile(t), :] = key
        half_ref[tile(t), :] = (key >> 16).astype(jnp.int16)
        return carry

    lax.fori_loop(0, n_tiles, score_tile, 0)

    def count(pred):
        def body(t, cnt):
            hit = pred(key_ref[tile(t), :], t).astype(jnp.int32)
            return cnt + jnp.sum(hit.reshape(tk // 8, 8, C_BLOCK), axis=0)
        cnt = lax.fori_loop(0, n_tiles, body, jnp.zeros((8, C_BLOCK), jnp.int32))
        return jnp.sum(cnt, axis=0, keepdims=True)

    def count_half(pred):
        def body(t, cnt):
            hit = jnp.where(pred(half_ref[tile(t), :]), jnp.int16(1), jnp.int16(0))
            parts = [hit[r:r + 16] for r in range(0, tk, 16)]
            while len(parts) > 1:
                parts = [a + b for a, b in zip(parts[::2], parts[1::2])]
            return cnt + parts[0]
        cnt = lax.fori_loop(0, n_tiles, body, jnp.zeros((16, C_BLOCK), jnp.int16))
        return jnp.sum(cnt.astype(jnp.int32), axis=0, keepdims=True)

    def bisect_half(want):
        def step(i, lo):
            cand = lo + lax.shift_left(jnp.int32(1), 15 - i)
            n_ge = count_half(lambda k: k >= cand.astype(jnp.int16))
            return jnp.where(n_ge >= want, cand, lo)
        return lax.fori_loop(0, 16, step, jnp.full((1, C_BLOCK), INT16_MIN, jnp.int32))

    top = bisect_half(topk)
    top16 = top.astype(jnp.int16)
    want_low = topk - count_half(lambda k: k > top16)

    def low_tile(t, carry):
        low = (key_ref[tile(t), :] & 0xFFFF) + INT16_MIN
        shares = half_ref[tile(t), :] == top16
        half_ref[tile(t), :] = jnp.where(shares, low.astype(jnp.int16), jnp.int16(INT16_MIN))
        return carry

    lax.fori_loop(0, n_tiles, low_tile, 0)
    thresh = top * 65536 + (bisect_half(want_low) - INT16_MIN)

    n_ge = count(lambda k, t: k >= thresh)
    excess = jnp.max(jnp.where((n_ge > topk) & (thresh > NEG_KEY), 1, 0))

    @pl.when(excess > 0)
    def _():
        want = topk - count(lambda k, t: k > thresh)

        def index_bisect(i, j0):
            cand = j0 + lax.shift_left(jnp.int32(1), idx_bits - 1 - i)
            before = count(lambda k, t: (k == thresh) & (t * tk + k_row < cand))
            return jnp.where(before < want, cand, j0)

        j0 = lax.fori_loop(0, idx_bits, index_bisect, jnp.zeros((1, C_BLOCK), jnp.int32))

        def demote(t, carry):
            keys = key_ref[tile(t), :]
            drop = (keys == thresh) & (t * tk + k_row > j0)
            key_ref[tile(t), :] = jnp.where(drop, thresh - 1, keys)
            return carry

        lax.fori_loop(0, n_tiles, demote, 0)

    thr = jnp.maximum(thresh, NEG_KEY + 1)
    thr2 = jnp.concatenate([thr, thr], axis=1)
    acc_ref[...] = jnp.zeros_like(acc_ref)
    ones_rows = jnp.ones((ONES_ROWS_C, tk), BF16)
    n_pairs = N_HEADS_C // 2

    def pair_q(p):
        return qabs_ref[0, 2 * p:2 * p + 2].reshape(2 * C_BLOCK, KV_RANK_C)

    bound = (KV_RANK_C * BOUND_SLACK_C * (KV_RANK_C ** -0.5 * LOG2E)
             * jnp.max(jnp.abs(qan_ref[...])) * jnp.max(jnp.abs(kvn_ref[...])))
    small_logits = bound <= MAX_FIXED_OFFSET_C

    def attn_loop(accumulate):
        def attn_tile(t, carry):
            kv = ckv_ref[0, tile(t), :]
            kvt = jnp.concatenate([ckvt_ref[:, tile(t)], ones_rows], axis=0)
            keys = key_ref[tile(t), :]
            sel = jnp.concatenate([keys, keys], axis=1) >= thr2

            def logits(p):
                return jnp.where(sel, _nt_dot(kv, pair_q(p)), NEG_INF)

            pending = [logits(p) for p in range(ATTN_AHEAD_C)]
            for p in range(n_pairs):
                if p + ATTN_AHEAD_C < n_pairs:
                    pending.append(logits(p + ATTN_AHEAD_C))
                accumulate(p, pending.pop(0), kvt)
            return carry

        lax.fori_loop(0, n_tiles, attn_tile, 0)

    def accumulate_fixed(p, s, kvt):
        acc_ref[p] += _dot(kvt, jnp.exp2(s - bound).astype(BF16))

    def accumulate_online(p, s, kvt):
        m_old = m_ref[p]
        m_new = jnp.maximum(m_old, jnp.max(s, axis=0, keepdims=True))
        pe = jnp.exp2(s - m_new)
        acc_ref[p] = jnp.exp2(m_old - m_new) * acc_ref[p] + _dot(kvt, pe.astype(BF16))
        m_ref[p] = m_new

    @pl.when(small_logits)
    def _():
        attn_loop(accumulate_fixed)

    @pl.when(jnp.logical_not(small_logits))
    def _():
        m_ref[...] = jnp.full(m_ref.shape, NEG_INF, F32)
        attn_loop(accumulate_online)

    for p in range(N_HEADS_C // 2):
        denom = acc_ref[p, KV_RANK_C:KV_RANK_C + 1, :]
        o_lat_t = (acc_ref[p, :KV_RANK_C, :] / denom).astype(BF16)
        for e in range(2):
            h = 2 * p + e
            o_ref[0, :, h * V_HEAD_C:(h + 1) * V_HEAD_C] = _tn_dot(
                o_lat_t[:, e * C_BLOCK:(e + 1) * C_BLOCK], wuv_ref[h]).astype(o_ref.dtype)


def dsa_attention(qabs, qidx, widx, ckv, ckvt, kidx, w_uv, q_abs_norm, kv_norm, *, batch, topk):
    nblk = qabs.shape[0]
    nb = nblk // batch
    l = nb * C_BLOCK
    assert l % KEY_TILE_C == 0 and l >= topk
    ckv = ckv.reshape(batch, l, KV_RANK_C)
    kidx = kidx.reshape(batch, l, IDX_DIM)
    widx = widx.reshape(batch, nb, C_BLOCK, N_IDX_HEADS)
    return pl.pallas_call(
        functools.partial(_dsa_attn_kernel, topk=topk, idx_bits=(l - 1).bit_length()),
        grid=(batch, nb),
        in_specs=[
            pl.BlockSpec((1, N_HEADS_C, C_BLOCK, KV_RANK_C), lambda b, j: (b * nb + j, 0, 0, 0)),
            pl.BlockSpec((1, N_IDX_HEADS, C_BLOCK, IDX_DIM), lambda b, j: (b * nb + j, 0, 0, 0)),
            pl.BlockSpec((1, 1, C_BLOCK, N_IDX_HEADS), lambda b, j: (b, j, 0, 0)),
            pl.BlockSpec((1, l, KV_RANK_C), lambda b, j: (b, 0, 0)),
            pl.BlockSpec((KV_RANK_C, l), lambda b, j: (0, b)),
            pl.BlockSpec((1, l, IDX_DIM), lambda b, j: (b, 0, 0)),
            pl.BlockSpec(w_uv.shape, lambda b, j: (0, 0, 0)),
            pl.BlockSpec((1, KV_RANK_C), lambda b, j: (0, 0)),
            pl.BlockSpec((1, KV_RANK_C), lambda b, j: (0, 0)),
        ],
        out_specs=pl.BlockSpec((1, C_BLOCK, N_HEADS_C * V_HEAD_C), lambda b, j: (b, j, 0)),
        out_shape=jax.ShapeDtypeStruct((batch, l, N_HEADS_C * V_HEAD_C), BF16),
        scratch_shapes=[
            pltpu.VMEM((l, C_BLOCK), jnp.int32),
            pltpu.VMEM((l, C_BLOCK), jnp.int16),
            pltpu.VMEM((N_HEADS_C // 2, 1, 2 * C_BLOCK), F32),
            pltpu.VMEM((N_HEADS_C // 2, KV_RANK_C + ONES_ROWS_C, 2 * C_BLOCK), F32),
        ],
        compiler_params=_params("parallel", "arbitrary"),
        name="dsa_attention",
    )(qabs, qidx, widx, ckv, ckvt, kidx, w_uv, q_abs_norm.reshape(1, -1), kv_norm.reshape(1, -1))


TM = 512
TF = 512


def _swa_layer(h, batch, norm_mix, w_in, q_norm, k_norm, sinks, wo):
    m, d = h.shape
    qkv = norm_matmul(h, norm_mix, w_in.astype(BF16), tm=TM, tn=w_in.shape[1], out_dtype=BF16)
    o = swa_attention(qkv.reshape(batch, m // batch, -1), q_norm, k_norm, sinks)
    return matmul_res(o.reshape(m, -1), wo.astype(BF16), h, tm=TM, tn=d)


def _gla_layer(h, batch, norm_mix, w_in, w_gate_b, gate_bias, o_norm, wo):
    m, d = h.shape
    dk_all = w_gate_b.shape[1]
    dv_all = wo.shape[0]
    n_main = 2 * dk_all + 2 * dv_all
    wgb_pad = jnp.pad(w_gate_b.astype(BF16), ((0, LANES - GATE_RANK), (0, 0)))
    proj, g_low = norm_matmul_gate(h, norm_mix, w_in, n_main, tm=2 * TM, tn=n_main // 8)
    o = gla_attention(proj.reshape(batch, m // batch, -1), g_low.reshape(batch, m // batch, -1),
                      wgb_pad, gate_bias, o_norm,
                      dk=dk_all // N_HEADS_B, dv=dv_all // N_HEADS_B, tb=256)
    return matmul_res(o.reshape(m, -1), wo.astype(BF16), h, tm=TM, tn=d)


def _dsa_layer(h, batch, norm_mix, w_in, q_lat_norm, kv_norm, w_uq, w_uk, q_abs_norm, w_iq,
               idx_k_norm, idx_k_bias, w_uv, wo):
    m, d = h.shape
    l = m // batch
    qabs, qidx, widx, ckv, ckvt, kidx = dsa_prep(
        h, norm_mix, w_in.T.astype(BF16), q_lat_norm, kv_norm, w_uq.astype(BF16), w_uk.astype(BF16),
        q_abs_norm, w_iq.astype(BF16), idx_k_norm, idx_k_bias, tm=TM)
    o = dsa_attention(qabs, qidx, widx, ckv, ckvt, kidx, w_uv.astype(BF16), q_abs_norm, kv_norm, batch=batch,
                      topk=min(TOPK_MAX, l // 4))
    return matmul_res(o.reshape(m, -1), wo.astype(BF16), h, tm=TM, tn=d)


def kernel(x, l0_norm_mix, l0_w_in, l0_q_norm, l0_k_norm, l0_sinks, l0_wo, l0_norm_ffn, l0_w_gate_up, l0_w_down, l1_norm_mix, l1_w_in, l1_w_gate_b, l1_gate_bias, l1_o_norm, l1_wo, l1_norm_ffn, l1_w_gate_up, l1_w_down, l2_norm_mix, l2_w_in, l2_q_lat_norm, l2_kv_norm, l2_w_uq, l2_w_uk, l2_q_abs_norm, l2_w_iq, l2_idx_k_norm, l2_idx_k_bias, l2_w_uv, l2_wo, l2_norm_ffn, l2_w_gate_up, l2_w_down, l3_norm_mix, l3_w_in, l3_q_norm, l3_k_norm, l3_sinks, l3_wo, l3_norm_ffn, l3_w_gate_up, l3_w_down):
    batch, seq, d = x.shape
    h = x.reshape(batch * seq, d)

    def channel_mix(h, norm_ffn, w_gate_up, w_down):
        return ffn(h, norm_ffn, w_gate_up, w_down, tm=2 * TM, tf=TF // 2)

    h = _swa_layer(h, batch, l0_norm_mix, l0_w_in, l0_q_norm, l0_k_norm, l0_sinks, l0_wo)
    h = channel_mix(h, l0_norm_ffn, l0_w_gate_up, l0_w_down)
    h = _gla_layer(h, batch, l1_norm_mix, l1_w_in, l1_w_gate_b, l1_gate_bias, l1_o_norm, l1_wo)
    h = channel_mix(h, l1_norm_ffn, l1_w_gate_up, l1_w_down)
    h = _dsa_layer(h, batch, l2_norm_mix, l2_w_in, l2_q_lat_norm, l2_kv_norm, l2_w_uq, l2_w_uk,
                   l2_q_abs_norm, l2_w_iq, l2_idx_k_norm, l2_idx_k_bias, l2_w_uv, l2_wo)
    h = channel_mix(h, l2_norm_ffn, l2_w_gate_up, l2_w_down)
    h = _swa_layer(h, batch, l3_norm_mix, l3_w_in, l3_q_norm, l3_k_norm, l3_sinks, l3_wo)
    h = channel_mix(h, l3_norm_ffn, l3_w_gate_up, l3_w_down)
    return h.reshape(batch, seq, d)
```

```python
import functools

import jax
import jax.numpy as jnp
import numpy as np
from jax import lax
from jax.experimental import pallas as pl
from jax.experimental.pallas import tpu as pltpu

F32 = jnp.float32
BF16 = jnp.bfloat16
EPS = 1e-6
NEG_INF = -1e30
LANES = 128

HD_A = 64
N_KV_A = 4
GROUP_A = 8
BLOCK_A = 128
N_HEADS_B = 4
GATE_RANK = 16
GATE_TAU = 16.0
CHUNK = 64
N_HEADS_C = 16
QK_HEAD_C = 128
V_HEAD_C = 128
Q_RANK_C = 512
KV_RANK_C = 256
N_IDX_HEADS = 16
IDX_DIM = 64
TOPK_MAX = 256
C_BLOCK = 128
KEY_TILE_C = 512
ATTN_AHEAD_C = 2
ONES_ROWS_C = 16
BOUND_SLACK_C = 1.01
MAX_FIXED_OFFSET_C = 56.0
INT_MIN = -2 ** 31
LOG2E = 1.4426950408889634
_NEG_BITS = int(np.float32(NEG_INF).view(np.int32))
NEG_KEY = _NEG_BITS ^ ((_NEG_BITS >> 31) & 0x7FFFFFFF)


def _nt_dot(a, b):
    return lax.dot_general(a, b, (((1,), (1,)), ((), ())), preferred_element_type=F32)


def _tn_dot(a, b):
    return lax.dot_general(a, b, (((0,), (0,)), ((), ())), preferred_element_type=F32)


def _dot(a, b):
    return jnp.dot(a, b, preferred_element_type=F32)


def _rms(x, g):
    return x * lax.rsqrt(jnp.mean(x * x, axis=-1, keepdims=True) + EPS) * g


def _params(*sem, vmem_bytes=None):
    return pltpu.CompilerParams(dimension_semantics=sem, vmem_limit_bytes=vmem_bytes)


def _norm_matmul_kernel(x_ref, g_ref, w_ref, o_ref, xn_ref):
    @pl.when(pl.program_id(1) == 0)
    def _():
        xn_ref[...] = _rms(x_ref[...], g_ref[...]).astype(BF16)

    o_ref[...] = _dot(xn_ref[...], w_ref[...].astype(BF16)).astype(o_ref.dtype)


def norm_matmul(x, g, w, *, tm, tn, out_dtype=F32):
    m, d = x.shape
    n = w.shape[1]
    assert m % tm == 0 and n % tn == 0
    return pl.pallas_call(
        _norm_matmul_kernel,
        grid=(m // tm, n // tn),
        in_specs=[
            pl.BlockSpec((tm, d), lambda i, j: (i, 0)),
            pl.BlockSpec((1, d), lambda i, j: (0, 0)),
            pl.BlockSpec((d, tn), lambda i, j: (0, j)),
        ],
        out_specs=pl.BlockSpec((tm, tn), lambda i, j: (i, j)),
        out_shape=jax.ShapeDtypeStruct((m, n), out_dtype),
        scratch_shapes=[pltpu.VMEM((tm, d), BF16)],
        compiler_params=_params("parallel", "arbitrary"),
        name="norm_matmul",
    )(x, g.reshape(1, d), w)


def _norm_matmul_gate_kernel(x_ref, g_ref, w_ref, wg_ref, o_ref, gl_ref, xn_ref):
    @pl.when(pl.program_id(1) == 0)
    def _():
        xn = _rms(x_ref[...], g_ref[...]).astype(BF16)
        xn_ref[...] = xn
        gl_ref[...] = _nt_dot(xn, wg_ref[...])

    o_ref[...] = _nt_dot(xn_ref[...], w_ref[...].astype(BF16)).astype(o_ref.dtype)


def norm_matmul_gate(x, g, w, n_main, *, tm, tn):
    m, d = x.shape
    assert m % tm == 0 and n_main % tn == 0
    w_t = w.T
    w_gate = jnp.pad(w_t[n_main:].astype(BF16), ((0, LANES - (w.shape[1] - n_main)), (0, 0)))
    vmem_bytes = (2 * tm * d * 4 + tm * d * 2 + 2 * d * tn * 4 + d * tn * 2
                  + 2 * tm * tn * 2 + tm * tn * 4 + 2 * tm * LANES * 4 + 2 * d * LANES * 2)
    return pl.pallas_call(
        _norm_matmul_gate_kernel,
        grid=(m // tm, n_main // tn),
        in_specs=[
            pl.BlockSpec((tm, d), lambda i, j: (i, 0)),
            pl.BlockSpec((1, d), lambda i, j: (0, 0)),
            pl.BlockSpec((tn, d), lambda i, j: (j, 0)),
            pl.BlockSpec((LANES, d), lambda i, j: (0, 0)),
        ],
        out_specs=[
            pl.BlockSpec((tm, tn), lambda i, j: (i, j)),
            pl.BlockSpec((tm, LANES), lambda i, j: (i, 0)),
        ],
        out_shape=[
            jax.ShapeDtypeStruct((m, n_main), BF16),
            jax.ShapeDtypeStruct((m, LANES), F32),
        ],
        scratch_shapes=[pltpu.VMEM((tm, d), BF16)],
        compiler_params=_params("parallel", "arbitrary", vmem_bytes=vmem_bytes),
        name="norm_matmul_gate",
    )(x, g.reshape(1, d), w_t, w_gate)


def _matmul_res_kernel(a_ref, w_ref, r_ref, o_ref):
    o_ref[...] = r_ref[...] + _dot(a_ref[...], w_ref[...].astype(BF16))


def matmul_res(a, w, r, *, tm, tn):
    m, k = a.shape
    n = w.shape[1]
    assert m % tm == 0 and n % tn == 0
    return pl.pallas_call(
        _matmul_res_kernel,
        grid=(m // tm, n // tn),
        in_specs=[
            pl.BlockSpec((tm, k), lambda i, j: (i, 0)),
            pl.BlockSpec((k, tn), lambda i, j: (0, j)),
            pl.BlockSpec((tm, tn), lambda i, j: (i, j)),
        ],
        out_specs=pl.BlockSpec((tm, tn), lambda i, j: (i, j)),
        out_shape=jax.ShapeDtypeStruct((m, n), F32),
        compiler_params=_params("parallel", "arbitrary"),
        name="matmul_res",
    )(a, w, r)


def _ffn_kernel(x_ref, g_ref, wg_ref, wu_ref, wd_ref, o_ref, xn_ref):
    @pl.when(pl.program_id(1) == 0)
    def _():
        x = x_ref[...]
        xn_ref[...] = _rms(x, g_ref[...]).astype(BF16)
        o_ref[...] = x

    xn = xn_ref[...]
    gate = _dot(xn, wg_ref[...].astype(BF16))
    up = _dot(xn, wu_ref[...].astype(BF16))
    act = (gate * (1.0 / (1.0 + jnp.exp(-gate))) * up).astype(BF16)
    o_ref[...] += _dot(act, wd_ref[...].astype(BF16))


def ffn(x, g, w_gate_up, w_down, *, tm, tf):
    m, d = x.shape
    ff = w_down.shape[0]
    nf = ff // tf
    assert m % tm == 0 and ff % tf == 0
    w_bytes = w_down.dtype.itemsize
    vmem_bytes = (4 * tm * d * 4 + tm * d * 2 + 6 * d * tf * w_bytes
                  + 3 * tm * tf * 4 + 3 * d * tf * 2)
    return pl.pallas_call(
        _ffn_kernel,
        grid=(m // tm, nf),
        in_specs=[
            pl.BlockSpec((tm, d), lambda i, j: (i, 0)),
            pl.BlockSpec((1, d), lambda i, j: (0, 0)),
            pl.BlockSpec((d, tf), lambda i, j: (0, j)),
            pl.BlockSpec((d, tf), lambda i, j: (0, j + nf)),
            pl.BlockSpec((tf, d), lambda i, j: (j, 0)),
        ],
        out_specs=pl.BlockSpec((tm, d), lambda i, j: (i, 0)),
        out_shape=jax.ShapeDtypeStruct((m, d), F32),
        scratch_shapes=[pltpu.VMEM((tm, d), BF16)],
        compiler_params=_params("parallel", "arbitrary", vmem_bytes=vmem_bytes),
        name="ffn",
    )(x, g.reshape(1, d), w_gate_up, w_gate_up, w_down)


def _swa_kernel(sink_ref, cur_ref, prev_ref, qn_ref, kn_ref, sel_ref, o_ref):
    n = pl.program_id(1)
    nq = N_KV_A * GROUP_A * HD_A
    nkv = N_KV_A * HD_A
    kj = lax.broadcasted_iota(jnp.int32, (2 * BLOCK_A, 2 * BLOCK_A), 0)
    qi = lax.broadcasted_iota(jnp.int32, (2 * BLOCK_A, 2 * BLOCK_A), 1) & (BLOCK_A - 1)
    mask2 = (kj > qi) & (kj <= qi + BLOCK_A) & ((n > 0) | (kj >= BLOCK_A))
    first_half = lax.broadcasted_iota(jnp.int32, (1, 2 * BLOCK_A), 1) < BLOCK_A
    q_bf = cur_ref[0, :, :nq]
    xq = q_bf.astype(F32)
    sq = xq * xq
    sq_hi = sq.astype(BF16)
    sq_lo = (sq - sq_hi.astype(F32)).astype(BF16)
    ssq = _dot(sq_hi, sel_ref[...]) + _dot(sq_lo, sel_ref[...])
    rq_t = (lax.rsqrt(ssq * (1.0 / HD_A) + EPS) * LOG2E).T
    kscale = qn_ref[...] * kn_ref[...] * HD_A ** -0.5
    zeros = jnp.zeros((2 * BLOCK_A, HD_A), F32)
    v_all = jnp.concatenate([prev_ref[0, :, nkv:2 * nkv], cur_ref[0, :, nq + nkv:nq + 2 * nkv]], axis=0)
    v_t = v_all.astype(F32).T.astype(BF16)
    k_pads = []
    for h in range(N_KV_A):
        k = jnp.concatenate([prev_ref[0, :, h * HD_A:(h + 1) * HD_A],
                             cur_ref[0, :, nq + h * HD_A:nq + (h + 1) * HD_A]], axis=0).astype(F32)
        kn = k * lax.rsqrt(jnp.mean(k * k, axis=-1, keepdims=True) + EPS) * kscale
        k_pads.append((jnp.concatenate([kn, zeros], axis=1).astype(BF16),
                       jnp.concatenate([zeros, kn], axis=1).astype(BF16)))

    def logits(i):
        j = 2 * i
        k_pad = k_pads[j // GROUP_A]
        q_tile = q_bf[:, j * HD_A:(j + 2) * HD_A]
        rq = jnp.concatenate([rq_t[j:j + 1, :], rq_t[j + 1:j + 2, :]], axis=1)
        s = jnp.concatenate([_nt_dot(k_pad[0], q_tile), _nt_dot(k_pad[1], q_tile)], axis=1) * rq
        return jnp.where(mask2, s, NEG_INF)

    def attend(i, s):
        j = 2 * i
        sink = jnp.where(first_half, sink_ref[j], sink_ref[j + 1]) * LOG2E
        mx = jnp.maximum(jnp.max(s, axis=0, keepdims=True), sink)
        p = jnp.exp2(s - mx)
        den = jnp.sum(p, axis=0, keepdims=True) + jnp.exp2(sink - mx)
        h = j // GROUP_A
        return _dot(v_t[h * HD_A:(h + 1) * HD_A, :], p.astype(BF16)), den

    def finish(i, o_t, den):
        o_t = o_t / den
        o_pair = jnp.concatenate([o_t[:, :BLOCK_A], o_t[:, BLOCK_A:]], axis=0)
        o_ref[0, :, 2 * i * HD_A:(2 * i + 2) * HD_A] = o_pair.T.astype(o_ref.dtype)

    n_pairs = N_KV_A * GROUP_A // 2
    ahead = 3
    pending = [logits(i) for i in range(ahead)]
    unfinished = None
    for i in range(n_pairs):
        if i + ahead < n_pairs:
            pending.append(logits(i + ahead))
        out = attend(i, pending.pop(0))
        if unfinished is not None:
            finish(i - 1, *unfinished)
        unfinished = out
    finish(n_pairs - 1, *unfinished)


def swa_attention(qkv, q_norm, k_norm, sinks):
    b, l, n = qkv.shape
    nq = N_KV_A * GROUP_A * HD_A
    kv_w = 2 * N_KV_A * HD_A
    head_sel = (jnp.arange(nq)[:, None] // HD_A == jnp.arange(LANES)[None, :]).astype(BF16)
    return pl.pallas_call(
        _swa_kernel,
        grid=(b, l // BLOCK_A),
        in_specs=[
            pl.BlockSpec(memory_space=pltpu.SMEM),
            pl.BlockSpec((1, BLOCK_A, n), lambda i, j: (i, j, 0)),
            pl.BlockSpec((1, BLOCK_A, kv_w), lambda i, j: (i, jnp.maximum(j - 1, 0), nq // kv_w)),
            pl.BlockSpec((1, HD_A), lambda i, j: (0, 0)),
            pl.BlockSpec((1, HD_A), lambda i, j: (0, 0)),
            pl.BlockSpec((nq, LANES), lambda i, j: (0, 0)),
        ],
        out_specs=pl.BlockSpec((1, BLOCK_A, nq), lambda i, j: (i, j, 0)),
        out_shape=jax.ShapeDtypeStruct((b, l, nq), BF16),
        compiler_params=_params("parallel", "parallel"),
        name="swa_attention",
    )(sinks, qkv, qkv, q_norm.reshape(1, HD_A), k_norm.reshape(1, HD_A), head_sel)


def _split3(x):
    hi = x.astype(BF16)
    r1 = x - hi.astype(F32)
    mid = r1.astype(BF16)
    lo = (r1 - mid.astype(F32)).astype(BF16)
    return hi, mid, lo


def _gla_kernel(q_ref, k_ref, v_ref, r_ref, gl_ref, wgb_ref, gb_ref, on_ref, o_ref, st_ref,
                *, tb, dk, dv):
    @pl.when(pl.program_id(1) == 0)
    def _():
        st_ref[...] = jnp.zeros_like(st_ref)

    ti = lax.broadcasted_iota(jnp.int32, (CHUNK, CHUNK), 0)
    si = lax.broadcasted_iota(jnp.int32, (CHUNK, CHUNK), 1)
    causal = si <= ti
    tri = causal.astype(BF16)
    wgb = wgb_ref[...]
    gb = gb_ref[...]
    on = on_ref[...]

    def decays(c):
        rs = pl.ds(c * CHUNK, CHUNK)
        gate = _dot(gl_ref[0, rs, :].astype(BF16), wgb) + gb
        log_a = (jnp.minimum(gate, 0.0) - jnp.log1p(jnp.exp(-jnp.abs(gate)))) / GATE_TAU
        hi, mid, lo = _split3(log_a)
        bcum = _dot(tri, hi) + _dot(tri, mid) + _dot(tri, lo)
        b_last = bcum[CHUNK - 1:CHUNK, :]
        q = q_ref[0, rs, :].astype(F32)
        k = k_ref[0, rs, :].astype(F32)
        q_dec = (q * dk ** -0.5 * jnp.exp(bcum)).astype(BF16)
        k_inv = (k * jnp.exp(-bcum)).astype(BF16)
        k_end = (k * jnp.exp(b_last - bcum)).astype(BF16)
        return q_dec, k_inv, k_end, jnp.exp(b_last)

    def recur(c, q_dec, k_inv, k_end, decay):
        rs = pl.ds(c * CHUNK, CHUNK)
        for h in range(N_HEADS_B):
            ks = slice(h * dk, (h + 1) * dk)
            vs = slice(h * dv, (h + 1) * dv)
            v = v_ref[0, rs, vs].astype(BF16)
            a = jnp.where(causal, _nt_dot(q_dec[:, ks], k_inv[:, ks]), 0.0)
            st = st_ref[h]
            o = _dot(a.astype(BF16), v) + _nt_dot(q_dec[:, ks], st.astype(BF16))
            st_ref[h] = st * decay[:, ks] + _tn_dot(v, k_end[:, ks])
            o = _rms(o, on)
            r = r_ref[0, rs, vs].astype(F32)
            o_ref[0, rs, vs] = (o * (r * (1.0 / (1.0 + jnp.exp(-r))))).astype(o_ref.dtype)

    n_chunks = tb // CHUNK
    ready = decays(0)
    for c in range(n_chunks):
        cur = ready
        if c + 1 < n_chunks:
            ready = decays(c + 1)
        recur(c, *cur)


def gla_attention(proj, g_low, w_gate_b, gate_bias, o_norm, *, dk, dv, tb):
    b, l, _ = proj.shape
    dk_all, dv_all = N_HEADS_B * dk, N_HEADS_B * dv
    return pl.pallas_call(
        functools.partial(_gla_kernel, tb=tb, dk=dk, dv=dv),
        grid=(b, l // tb),
        in_specs=[
            pl.BlockSpec((1, tb, dk_all), lambda i, j: (i, j, 0)),
            pl.BlockSpec((1, tb, dk_all), lambda i, j: (i, j, 1)),
            pl.BlockSpec((1, tb, dv_all), lambda i, j: (i, j, 2 * dk_all // dv_all)),
            pl.BlockSpec((1, tb, dv_all), lambda i, j: (i, j, 2 * dk_all // dv_all + 1)),
            pl.BlockSpec((1, tb, LANES), lambda i, j: (i, j, 0)),
            pl.BlockSpec((LANES, dk_all), lambda i, j: (0, 0)),
            pl.BlockSpec((1, dk_all), lambda i, j: (0, 0)),
            pl.BlockSpec((1, dv), lambda i, j: (0, 0)),
        ],
        out_specs=pl.BlockSpec((1, tb, dv_all), lambda i, j: (i, j, 0)),
        out_shape=jax.ShapeDtypeStruct((b, l, dv_all), BF16),
        scratch_shapes=[pltpu.VMEM((N_HEADS_B, dv, dk), F32)],
        compiler_params=_params("parallel", "arbitrary"),
        name="gla_attention",
    )(proj, proj, proj, proj, g_low, w_gate_b, gate_bias.reshape(1, dk_all), o_norm.reshape(1, dv))


def _dsa_prep_kernel(x_ref, g_ref, win_ref, qln_ref, kvn_ref, wuq_ref, wuk_ref, qan_ref, wiq_ref,
                     ikn_ref, ikb_ref, qabs_ref, qidx_ref, widx_ref, ckv_ref, ckvt_ref, kidx_ref):
    xn = _rms(x_ref[...], g_ref[...]).astype(BF16)
    proj = _nt_dot(xn, win_ref[...])
    o_kv = Q_RANK_C
    o_ki = Q_RANK_C + KV_RANK_C
    o_wi = o_ki + IDX_DIM
    c_q = _rms(proj[:, :o_kv], qln_ref[...]).astype(BF16)
    c_kv = _rms(proj[:, o_kv:o_ki], kvn_ref[...])
    ckv_ref[...] = c_kv.astype(BF16)
    ckvt_ref[...] = c_kv.T.astype(BF16)
    k_idx = proj[:, o_ki:o_wi]
    mu = jnp.mean(k_idx, axis=-1, keepdims=True)
    kc = k_idx - mu
    k_idx = kc * lax.rsqrt(jnp.mean(kc * kc, axis=-1, keepdims=True) + EPS)
    kidx_ref[...] = (k_idx * ikn_ref[...] + ikb_ref[...]).astype(BF16)
    widx_ref[...] = proj[:, o_wi:o_wi + N_IDX_HEADS] * (N_IDX_HEADS ** -0.5 * IDX_DIM ** -0.5)
    q_nope = _dot(c_q, wuq_ref[...])
    q_idx = _dot(c_q, wiq_ref[...])
    qan = qan_ref[...] * (KV_RANK_C ** -0.5 * LOG2E)
    for h in range(N_HEADS_C):
        qh = q_nope[:, h * QK_HEAD_C:(h + 1) * QK_HEAD_C].astype(BF16)
        qa = _rms(_dot(qh, wuk_ref[h]), qan).astype(BF16)
        qi = q_idx[:, h * IDX_DIM:(h + 1) * IDX_DIM].astype(BF16)
        for j in range(qabs_ref.shape[0]):
            qabs_ref[j, h] = qa[j * C_BLOCK:(j + 1) * C_BLOCK]
            qidx_ref[j, h] = qi[j * C_BLOCK:(j + 1) * C_BLOCK]


def dsa_prep(x, g, w_in, q_lat_norm, kv_norm, w_uq, w_uk, q_abs_norm, w_iq, idx_k_norm, idx_k_bias,
             *, tm):
    m, d = x.shape
    nblk = m // C_BLOCK
    bpt = tm // C_BLOCK
    n_in = w_in.shape[0]
    const2 = lambda i: (0, 0)
    return pl.pallas_call(
        _dsa_prep_kernel,
        grid=(m // tm,),
        in_specs=[
            pl.BlockSpec((tm, d), lambda i: (i, 0)),
            pl.BlockSpec((1, d), const2),
            pl.BlockSpec((n_in, d), const2),
            pl.BlockSpec((1, Q_RANK_C), const2),
            pl.BlockSpec((1, KV_RANK_C), const2),
            pl.BlockSpec(w_uq.shape, const2),
            pl.BlockSpec(w_uk.shape, lambda i: (0, 0, 0)),
            pl.BlockSpec((1, KV_RANK_C), const2),
            pl.BlockSpec(w_iq.shape, const2),
            pl.BlockSpec((1, IDX_DIM), const2),
            pl.BlockSpec((1, IDX_DIM), const2),
        ],
        out_specs=[
            pl.BlockSpec((bpt, N_HEADS_C, C_BLOCK, KV_RANK_C), lambda i: (i, 0, 0, 0)),
            pl.BlockSpec((bpt, N_IDX_HEADS, C_BLOCK, IDX_DIM), lambda i: (i, 0, 0, 0)),
            pl.BlockSpec((tm, N_IDX_HEADS), lambda i: (i, 0)),
            pl.BlockSpec((tm, KV_RANK_C), lambda i: (i, 0)),
            pl.BlockSpec((KV_RANK_C, tm), lambda i: (0, i)),
            pl.BlockSpec((tm, IDX_DIM), lambda i: (i, 0)),
        ],
        out_shape=[
            jax.ShapeDtypeStruct((nblk, N_HEADS_C, C_BLOCK, KV_RANK_C), BF16),
            jax.ShapeDtypeStruct((nblk, N_IDX_HEADS, C_BLOCK, IDX_DIM), BF16),
            jax.ShapeDtypeStruct((m, N_IDX_HEADS), F32),
            jax.ShapeDtypeStruct((m, KV_RANK_C), BF16),
            jax.ShapeDtypeStruct((KV_RANK_C, m), BF16),
            jax.ShapeDtypeStruct((m, IDX_DIM), BF16),
        ],
        compiler_params=_params("parallel"),
        name="dsa_prep",
    )(x, g.reshape(1, d), w_in, q_lat_norm.reshape(1, -1), kv_norm.reshape(1, -1), w_uq, w_uk,
      q_abs_norm.reshape(1, -1), w_iq, idx_k_norm.reshape(1, -1), idx_k_bias.reshape(1, -1))


def _sortable_key(s):
    bits = pltpu.bitcast(s, jnp.int32)
    return bits ^ ((bits >> 31) & 0x7FFFFFFF)


def _dsa_attn_kernel(qabs_ref, qidx_ref, widx_ref, ckv_ref, ckvt_ref, kidx_ref, wuv_ref, qan_ref,
                     kvn_ref, o_ref, key_ref, m_ref, acc_ref, *, topk, idx_bits):
    n = pl.program_id(1)
    tk = KEY_TILE_C
    n_tiles = (n * C_BLOCK + C_BLOCK + tk - 1) // tk
    q_row = n * C_BLOCK + lax.broadcasted_iota(jnp.int32, (C_BLOCK, tk), 0)
    k_col = lax.broadcasted_iota(jnp.int32, (C_BLOCK, tk), 1)
    k_row = lax.broadcasted_iota(jnp.int32, (tk, C_BLOCK), 0)
    widx = widx_ref[0, 0]

    def tile(t):
        return pl.ds(pl.multiple_of(t * tk, tk), tk)

    def score_tile(t, carry):
        kt = kidx_ref[0, tile(t), :]
        score = jnp.zeros((C_BLOCK, tk), F32)
        for h in range(N_IDX_HEADS):
            rel = jnp.maximum(_nt_dot(qidx_ref[0, h], kt), 0.0)
            score = score + rel * widx[:, h:h + 1]
        score = jnp.where(t * tk + k_col <= q_row, score, NEG_INF)
        key_ref[tile(t), :] = _sortable_key(score.T)
        return carry

    lax.fori_loop(0, n_tiles, score_tile, 0)

    def count(pred):
        def body(t, cnt):
            hit = pred(key_ref[tile(t), :], t).astype(jnp.int32)
            return cnt + jnp.sum(hit.reshape(tk // 8, 8, C_BLOCK), axis=0)
        cnt = lax.fori_loop(0, n_tiles, body, jnp.zeros((8, C_BLOCK), jnp.int32))
        return jnp.sum(cnt, axis=0, keepdims=True)

    def bisect(i, lo):
        cand = lo + lax.shift_left(jnp.int32(1), 31 - i)
        return jnp.where(count(lambda k, t: k >= cand) >= topk, cand, lo)

    thresh = lax.fori_loop(0, 32, bisect, jnp.full((1, C_BLOCK), INT_MIN, jnp.int32))

    n_ge = count(lambda k, t: k >= thresh)
    excess = jnp.max(jnp.where((n_ge > topk) & (thresh > NEG_KEY), 1, 0))

    @pl.when(excess > 0)
    def _():
        want = topk - count(lambda k, t: k > thresh)

        def index_bisect(i, j0):
            cand = j0 + lax.shift_left(jnp.int32(1), idx_bits - 1 - i)
            before = count(lambda k, t: (k == thresh) & (t * tk + k_row < cand))
            return jnp.where(before < want, cand, j0)

        j0 = lax.fori_loop(0, idx_bits, index_bisect, jnp.zeros((1, C_BLOCK), jnp.int32))

        def demote(t, carry):
            keys = key_ref[tile(t), :]
            drop = (keys == thresh) & (t * tk + k_row > j0)
            key_ref[tile(t), :] = jnp.where(drop, thresh - 1, keys)
            return carry

        lax.fori_loop(0, n_tiles, demote, 0)

    thr = jnp.maximum(thresh, NEG_KEY + 1)
    thr2 = jnp.concatenate([thr, thr], axis=1)
    acc_ref[...] = jnp.zeros_like(acc_ref)
    ones_rows = jnp.ones((ONES_ROWS_C, tk), BF16)
    n_pairs = N_HEADS_C // 2

    def pair_q(p):
        return qabs_ref[0, 2 * p:2 * p + 2].reshape(2 * C_BLOCK, KV_RANK_C)

    bound = (KV_RANK_C * BOUND_SLACK_C * (KV_RANK_C ** -0.5 * LOG2E)
             * jnp.max(jnp.abs(qan_ref[...])) * jnp.max(jnp.abs(kvn_ref[...])))
    small_logits = bound <= MAX_FIXED_OFFSET_C

    def attn_loop(accumulate):
        def attn_tile(t, carry):
            kv = ckv_ref[0, tile(t), :]
            kvt = jnp.concatenate([ckvt_ref[:, tile(t)], ones_rows], axis=0)
            keys = key_ref[tile(t), :]
            sel = jnp.concatenate([keys, keys], axis=1) >= thr2

            def logits(p):
                return jnp.where(sel, _nt_dot(kv, pair_q(p)), NEG_INF)

            pending = [logits(p) for p in range(ATTN_AHEAD_C)]
            for p in range(n_pairs):
                if p + ATTN_AHEAD_C < n_pairs:
                    pending.append(logits(p + ATTN_AHEAD_C))
                accumulate(p, pending.pop(0), kvt)
            return carry

        lax.fori_loop(0, n_tiles, attn_tile, 0)

    def accumulate_fixed(p, s, kvt):
        acc_ref[p] += _dot(kvt, jnp.exp2(s - bound).astype(BF16))

    def accumulate_online(p, s, kvt):
        m_old = m_ref[p]
        m_new = jnp.maximum(m_old, jnp.max(s, axis=0, keepdims=True))
        pe = jnp.exp2(s - m_new)
        acc_ref[p] = jnp.exp2(m_old - m_new) * acc_ref[p] + _dot(kvt, pe.astype(BF16))
        m_ref[p] = m_new

    @pl.when(small_logits)
    def _():
        attn_loop(accumulate_fixed)

    @pl.when(jnp.logical_not(small_logits))
    def _():
        m_ref[...] = jnp.full(m_ref.shape, NEG_INF, F32)
        attn_loop(accumulate_online)

    for p in range(N_HEADS_C // 2):
        denom = acc_ref[p, KV_RANK_C:KV_RANK_C + 1, :]
        o_lat_t = (acc_ref[p, :KV_RANK_C, :] / denom).astype(BF16)
        for e in range(2):
            h = 2 * p + e
            o_ref[0, :, h * V_HEAD_C:(h + 1) * V_HEAD_C] = _tn_dot(
                o_lat_t[:, e * C_BLOCK:(e + 1) * C_BLOCK], wuv_ref[h]).astype(o_ref.dtype)


def dsa_attention(qabs, qidx, widx, ckv, ckvt, kidx, w_uv, q_abs_norm, kv_norm, *, batch, topk):
    nblk = qabs.shape[0]
    nb = nblk // batch
    l = nb * C_BLOCK
    assert l % KEY_TILE_C == 0 and l >= topk
    ckv = ckv.reshape(batch, l, KV_RANK_C)
    kidx = kidx.reshape(batch, l, IDX_DIM)
    widx = widx.reshape(batch, nb, C_BLOCK, N_IDX_HEADS)
    return pl.pallas_call(
        functools.partial(_dsa_attn_kernel, topk=topk, idx_bits=(l - 1).bit_length()),
        grid=(batch, nb),
        in_specs=[
            pl.BlockSpec((1, N_HEADS_C, C_BLOCK, KV_RANK_C), lambda b, j: (b * nb + j, 0, 0, 0)),
            pl.BlockSpec((1, N_IDX_HEADS, C_BLOCK, IDX_DIM), lambda b, j: (b * nb + j, 0, 0, 0)),
            pl.BlockSpec((1, 1, C_BLOCK, N_IDX_HEADS), lambda b, j: (b, j, 0, 0)),
            pl.BlockSpec((1, l, KV_RANK_C), lambda b, j: (b, 0, 0)),
            pl.BlockSpec((KV_RANK_C, l), lambda b, j: (0, b)),
            pl.BlockSpec((1, l, IDX_DIM), lambda b, j: (b, 0, 0)),
            pl.BlockSpec(w_uv.shape, lambda b, j: (0, 0, 0)),
            pl.BlockSpec((1, KV_RANK_C), lambda b, j: (0, 0)),
            pl.BlockSpec((1, KV_RANK_C), lambda b, j: (0, 0)),
        ],
        out_specs=pl.BlockSpec((1, C_BLOCK, N_HEADS_C * V_HEAD_C), lambda b, j: (b, j, 0)),
        out_shape=jax.ShapeDtypeStruct((batch, l, N_HEADS_C * V_HEAD_C), BF16),
        scratch_shapes=[
            pltpu.VMEM((l, C_BLOCK), jnp.int32),
            pltpu.VMEM((N_HEADS_C // 2, 1, 2 * C_BLOCK), F32),
            pltpu.VMEM((N_HEADS_C // 2, KV_RANK_C + ONES_ROWS_C, 2 * C_BLOCK), F32),
        ],
        compiler_params=_params("parallel", "arbitrary"),
        name="dsa_attention",
    )(qabs, qidx, widx, ckv, ckvt, kidx, w_uv, q_abs_norm.reshape(1, -1), kv_norm.reshape(1, -1))


TM = 512
TF = 512


def _swa_layer(h, batch, norm_mix, w_in, q_norm, k_norm, sinks, wo):
    m, d = h.shape
    qkv = norm_matmul(h, norm_mix, w_in, tm=2 * TM, tn=TM, out_dtype=BF16)
    o = swa_attention(qkv.reshape(batch, m // batch, -1), q_norm, k_norm, sinks)
    return matmul_res(o.reshape(m, -1), wo, h, tm=min(4 * TM, m), tn=TM)


def _gla_layer(h, batch, norm_mix, w_in, w_gate_b, gate_bias, o_norm, wo):
    m, d = h.shape
    dk_all = w_gate_b.shape[1]
    dv_all = wo.shape[0]
    n_main = 2 * dk_all + 2 * dv_all
    wgb_pad = jnp.pad(w_gate_b.astype(BF16), ((0, LANES - GATE_RANK), (0, 0)))
    proj, g_low = norm_matmul_gate(h, norm_mix, w_in, n_main, tm=2 * TM, tn=n_main // 8)
    o = gla_attention(proj.reshape(batch, m // batch, -1), g_low.reshape(batch, m // batch, -1),
                      wgb_pad, gate_bias, o_norm,
                      dk=dk_all // N_HEADS_B, dv=dv_all // N_HEADS_B, tb=256)
    return matmul_res(o.reshape(m, -1), wo, h, tm=min(4 * TM, m), tn=TM)


def _dsa_layer(h, batch, norm_mix, w_in, q_lat_norm, kv_norm, w_uq, w_uk, q_abs_norm, w_iq,
               idx_k_norm, idx_k_bias, w_uv, wo):
    m, d = h.shape
    l = m // batch
    qabs, qidx, widx, ckv, ckvt, kidx = dsa_prep(
        h, norm_mix, w_in.T.astype(BF16), q_lat_norm, kv_norm, w_uq.astype(BF16), w_uk.astype(BF16),
        q_abs_norm, w_iq.astype(BF16), idx_k_norm, idx_k_bias, tm=TM)
    o = dsa_attention(qabs, qidx, widx, ckv, ckvt, kidx, w_uv.astype(BF16), q_abs_norm, kv_norm, batch=batch,
                      topk=min(TOPK_MAX, l // 4))
    return matmul_res(o.reshape(m, -1), wo, h, tm=min(4 * TM, m), tn=TM)


def kernel(x, l0_norm_mix, l0_w_in, l0_q_norm, l0_k_norm, l0_sinks, l0_wo, l0_norm_ffn, l0_w_gate_up, l0_w_down, l1_norm_mix, l1_w_in, l1_w_gate_b, l1_gate_bias, l1_o_norm, l1_wo, l1_norm_ffn, l1_w_gate_up, l1_w_down, l2_norm_mix, l2_w_in, l2_q_lat_norm, l2_kv_norm, l2_w_uq, l2_w_uk, l2_q_abs_norm, l2_w_iq, l2_idx_k_norm, l2_idx_k_bias, l2_w_uv, l2_wo, l2_norm_ffn, l2_w_gate_up, l2_w_down, l3_norm_mix, l3_w_in, l3_q_norm, l3_k_norm, l3_sinks, l3_wo, l3_norm_ffn, l3_w_gate_up, l3_w_down):
    batch, seq, d = x.shape
    h = x.reshape(batch * seq, d)

    def channel_mix(h, norm_ffn, w_gate_up, w_down):
        return ffn(h, norm_ffn, w_gate_up, w_down, tm=2 * TM, tf=TF // 2)

    h = _swa_layer(h, batch, l0_norm_mix, l0_w_in, l0_q_norm, l0_k_norm, l0_sinks, l0_wo)
    h = channel_mix(h, l0_norm_ffn, l0_w_gate_up, l0_w_down)
    h = _gla_layer(h, batch, l1_norm_mix, l1_w_in, l1_w_gate_b, l1_gate_bias, l1_o_norm, l1_wo)
    h = channel_mix(h, l1_norm_ffn, l1_w_gate_up, l1_w_down)
    h = _dsa_layer(h, batch, l2_norm_mix, l2_w_in, l2_q_lat_norm, l2_kv_norm, l2_w_uq, l2_w_uk,
                   l2_q_abs_norm, l2_w_iq, l2_idx_k_norm, l2_idx_k_bias, l2_w_uv, l2_wo)
    h = channel_mix(h, l2_norm_ffn, l2_w_gate_up, l2_w_down)
    h = _swa_layer(h, batch, l3_norm_mix, l3_w_in, l3_q_norm, l3_k_norm, l3_sinks, l3_wo)
    h = channel_mix(h, l3_norm_ffn, l3_w_gate_up, l3_w_down)
    return h.reshape(batch, seq, d)
```

```python
import functools

import jax
import jax.numpy as jnp
import numpy as np
from jax import lax
from jax.experimental import pallas as pl
from jax.experimental.pallas import tpu as pltpu

F32 = jnp.float32
BF16 = jnp.bfloat16
EPS = 1e-6
NEG_INF = -1e30
LANES = 128

HD_A = 64
N_KV_A = 4
GROUP_A = 8
BLOCK_A = 128
N_HEADS_B = 4
GATE_RANK = 16
GATE_TAU = 16.0
CHUNK = 64
N_HEADS_C = 16
QK_HEAD_C = 128
V_HEAD_C = 128
Q_RANK_C = 512
KV_RANK_C = 256
N_IDX_HEADS = 16
IDX_DIM = 64
TOPK_MAX = 256
C_BLOCK = 128
KEY_TILE_C = 512
ATTN_AHEAD_C = 2
ONES_ROWS_C = 16
BOUND_SLACK_C = 1.01
MAX_FIXED_OFFSET_C = 56.0
INT_MIN = -2 ** 31
LOG2E = 1.4426950408889634
_NEG_BITS = int(np.float32(NEG_INF).view(np.int32))
NEG_KEY = _NEG_BITS ^ ((_NEG_BITS >> 31) & 0x7FFFFFFF)


def _nt_dot(a, b):
    return lax.dot_general(a, b, (((1,), (1,)), ((), ())), preferred_element_type=F32)


def _tn_dot(a, b):
    return lax.dot_general(a, b, (((0,), (0,)), ((), ())), preferred_element_type=F32)


def _dot(a, b):
    return jnp.dot(a, b, preferred_element_type=F32)


def _rms(x, g):
    return x * lax.rsqrt(jnp.mean(x * x, axis=-1, keepdims=True) + EPS) * g


def _params(*sem, vmem_bytes=None):
    return pltpu.CompilerParams(dimension_semantics=sem, vmem_limit_bytes=vmem_bytes)


def _norm_matmul_kernel(x_ref, g_ref, w_ref, o_ref, xn_ref):
    @pl.when(pl.program_id(1) == 0)
    def _():
        xn_ref[...] = _rms(x_ref[...], g_ref[...]).astype(BF16)

    o_ref[...] = _dot(xn_ref[...], w_ref[...].astype(BF16)).astype(o_ref.dtype)


def norm_matmul(x, g, w, *, tm, tn, out_dtype=F32):
    m, d = x.shape
    n = w.shape[1]
    assert m % tm == 0 and n % tn == 0
    return pl.pallas_call(
        _norm_matmul_kernel,
        grid=(m // tm, n // tn),
        in_specs=[
            pl.BlockSpec((tm, d), lambda i, j: (i, 0)),
            pl.BlockSpec((1, d), lambda i, j: (0, 0)),
            pl.BlockSpec((d, tn), lambda i, j: (0, j)),
        ],
        out_specs=pl.BlockSpec((tm, tn), lambda i, j: (i, j)),
        out_shape=jax.ShapeDtypeStruct((m, n), out_dtype),
        scratch_shapes=[pltpu.VMEM((tm, d), BF16)],
        compiler_params=_params("parallel", "arbitrary"),
        name="norm_matmul",
    )(x, g.reshape(1, d), w)


def _norm_matmul_gate_kernel(x_ref, g_ref, w_ref, wg_ref, o_ref, gl_ref, xn_ref):
    @pl.when(pl.program_id(1) == 0)
    def _():
        xn = _rms(x_ref[...], g_ref[...]).astype(BF16)
        xn_ref[...] = xn
        gl_ref[...] = _nt_dot(xn, wg_ref[...])

    o_ref[...] = _nt_dot(xn_ref[...], w_ref[...].astype(BF16)).astype(o_ref.dtype)


def norm_matmul_gate(x, g, w, n_main, *, tm, tn):
    m, d = x.shape
    assert m % tm == 0 and n_main % tn == 0
    w_t = w.T
    w_gate = jnp.pad(w_t[n_main:].astype(BF16), ((0, LANES - (w.shape[1] - n_main)), (0, 0)))
    vmem_bytes = (2 * tm * d * 4 + tm * d * 2 + 2 * d * tn * 4 + d * tn * 2
                  + 2 * tm * tn * 2 + tm * tn * 4 + 2 * tm * LANES * 4 + 2 * d * LANES * 2)
    return pl.pallas_call(
        _norm_matmul_gate_kernel,
        grid=(m // tm, n_main // tn),
        in_specs=[
            pl.BlockSpec((tm, d), lambda i, j: (i, 0)),
            pl.BlockSpec((1, d), lambda i, j: (0, 0)),
            pl.BlockSpec((tn, d), lambda i, j: (j, 0)),
            pl.BlockSpec((LANES, d), lambda i, j: (0, 0)),
        ],
        out_specs=[
            pl.BlockSpec((tm, tn), lambda i, j: (i, j)),
            pl.BlockSpec((tm, LANES), lambda i, j: (i, 0)),
        ],
        out_shape=[
            jax.ShapeDtypeStruct((m, n_main), BF16),
            jax.ShapeDtypeStruct((m, LANES), F32),
        ],
        scratch_shapes=[pltpu.VMEM((tm, d), BF16)],
        compiler_params=_params("parallel", "arbitrary", vmem_bytes=vmem_bytes),
        name="norm_matmul_gate",
    )(x, g.reshape(1, d), w_t, w_gate)


def _matmul_res_kernel(a_ref, w_ref, r_ref, o_ref):
    o_ref[...] = r_ref[...] + _dot(a_ref[...], w_ref[...].astype(BF16))


def matmul_res(a, w, r, *, tm, tn):
    m, k = a.shape
    n = w.shape[1]
    assert m % tm == 0 and n % tn == 0
    return pl.pallas_call(
        _matmul_res_kernel,
        grid=(m // tm, n // tn),
        in_specs=[
            pl.BlockSpec((tm, k), lambda i, j: (i, 0)),
            pl.BlockSpec((k, tn), lambda i, j: (0, j)),
            pl.BlockSpec((tm, tn), lambda i, j: (i, j)),
        ],
        out_specs=pl.BlockSpec((tm, tn), lambda i, j: (i, j)),
        out_shape=jax.ShapeDtypeStruct((m, n), F32),
        compiler_params=_params("parallel", "arbitrary"),
        name="matmul_res",
    )(a, w, r)


def _ffn_kernel(x_ref, g_ref, wg_ref, wu_ref, wd_ref, o_ref, xn_ref):
    @pl.when(pl.program_id(1) == 0)
    def _():
        x = x_ref[...]
        xn_ref[...] = _rms(x, g_ref[...]).astype(BF16)
        o_ref[...] = x

    xn = xn_ref[...]
    gate = _dot(xn, wg_ref[...].astype(BF16))
    up = _dot(xn, wu_ref[...].astype(BF16))
    act = (gate * (1.0 / (1.0 + jnp.exp(-gate))) * up).astype(BF16)
    o_ref[...] += _dot(act, wd_ref[...].astype(BF16))


def ffn(x, g, w_gate_up, w_down, *, tm, tf):
    m, d = x.shape
    ff = w_down.shape[0]
    nf = ff // tf
    assert m % tm == 0 and ff % tf == 0
    w_bytes = w_down.dtype.itemsize
    vmem_bytes = (4 * tm * d * 4 + tm * d * 2 + 6 * d * tf * w_bytes
                  + 3 * tm * tf * 4 + 3 * d * tf * 2)
    return pl.pallas_call(
        _ffn_kernel,
        grid=(m // tm, nf),
        in_specs=[
            pl.BlockSpec((tm, d), lambda i, j: (i, 0)),
            pl.BlockSpec((1, d), lambda i, j: (0, 0)),
            pl.BlockSpec((d, tf), lambda i, j: (0, j)),
            pl.BlockSpec((d, tf), lambda i, j: (0, j + nf)),
            pl.BlockSpec((tf, d), lambda i, j: (j, 0)),
        ],
        out_specs=pl.BlockSpec((tm, d), lambda i, j: (i, 0)),
        out_shape=jax.ShapeDtypeStruct((m, d), F32),
        scratch_shapes=[pltpu.VMEM((tm, d), BF16)],
        compiler_params=_params("parallel", "arbitrary", vmem_bytes=vmem_bytes),
        name="ffn",
    )(x, g.reshape(1, d), w_gate_up, w_gate_up, w_down)


def _swa_kernel(sink_ref, cur_ref, prev_ref, qn_ref, kn_ref, sel_ref, o_ref):
    n = pl.program_id(1)
    nq = N_KV_A * GROUP_A * HD_A
    nkv = N_KV_A * HD_A
    kj = lax.broadcasted_iota(jnp.int32, (2 * BLOCK_A, 2 * BLOCK_A), 0)
    qi = lax.broadcasted_iota(jnp.int32, (2 * BLOCK_A, 2 * BLOCK_A), 1) & (BLOCK_A - 1)
    mask2 = (kj > qi) & (kj <= qi + BLOCK_A) & ((n > 0) | (kj >= BLOCK_A))
    first_half = lax.broadcasted_iota(jnp.int32, (1, 2 * BLOCK_A), 1) < BLOCK_A
    q_bf = cur_ref[0, :, :nq]
    xq = q_bf.astype(F32)
    sq = xq * xq
    sq_hi = sq.astype(BF16)
    sq_lo = (sq - sq_hi.astype(F32)).astype(BF16)
    ssq = _dot(sq_hi, sel_ref[...]) + _dot(sq_lo, sel_ref[...])
    rq_t = (lax.rsqrt(ssq * (1.0 / HD_A) + EPS) * LOG2E).T
    kscale = qn_ref[...] * kn_ref[...] * HD_A ** -0.5
    zeros = jnp.zeros((2 * BLOCK_A, HD_A), F32)
    v_all = jnp.concatenate([prev_ref[0, :, nkv:2 * nkv], cur_ref[0, :, nq + nkv:nq + 2 * nkv]], axis=0)
    v_t = v_all.astype(F32).T.astype(BF16)
    k_pads = []
    for h in range(N_KV_A):
        k = jnp.concatenate([prev_ref[0, :, h * HD_A:(h + 1) * HD_A],
                             cur_ref[0, :, nq + h * HD_A:nq + (h + 1) * HD_A]], axis=0).astype(F32)
        kn = k * lax.rsqrt(jnp.mean(k * k, axis=-1, keepdims=True) + EPS) * kscale
        k_pads.append((jnp.concatenate([kn, zeros], axis=1).astype(BF16),
                       jnp.concatenate([zeros, kn], axis=1).astype(BF16)))

    def logits(i):
        j = 2 * i
        k_pad = k_pads[j // GROUP_A]
        q_tile = q_bf[:, j * HD_A:(j + 2) * HD_A]
        rq = jnp.concatenate([rq_t[j:j + 1, :], rq_t[j + 1:j + 2, :]], axis=1)
        s = jnp.concatenate([_nt_dot(k_pad[0], q_tile), _nt_dot(k_pad[1], q_tile)], axis=1) * rq
        return jnp.where(mask2, s, NEG_INF)

    def attend(i, s):
        j = 2 * i
        sink = jnp.where(first_half, sink_ref[j], sink_ref[j + 1]) * LOG2E
        mx = jnp.maximum(jnp.max(s, axis=0, keepdims=True), sink)
        p = jnp.exp2(s - mx)
        den = jnp.sum(p, axis=0, keepdims=True) + jnp.exp2(sink - mx)
        h = j // GROUP_A
        return _dot(v_t[h * HD_A:(h + 1) * HD_A, :], p.astype(BF16)), den

    def finish(i, o_t, den):
        o_t = o_t / den
        o_pair = jnp.concatenate([o_t[:, :BLOCK_A], o_t[:, BLOCK_A:]], axis=0)
        o_ref[0, :, 2 * i * HD_A:(2 * i + 2) * HD_A] = o_pair.T.astype(o_ref.dtype)

    n_pairs = N_KV_A * GROUP_A // 2
    ahead = 3
    pending = [logits(i) for i in range(ahead)]
    unfinished = None
    for i in range(n_pairs):
        if i + ahead < n_pairs:
            pending.append(logits(i + ahead))
        out = attend(i, pending.pop(0))
        if unfinished is not None:
            finish(i - 1, *unfinished)
        unfinished = out
    finish(n_pairs - 1, *unfinished)


def swa_attention(qkv, q_norm, k_norm, sinks):
    b, l, n = qkv.shape
    nq = N_KV_A * GROUP_A * HD_A
    kv_w = 2 * N_KV_A * HD_A
    head_sel = (jnp.arange(nq)[:, None] // HD_A == jnp.arange(LANES)[None, :]).astype(BF16)
    return pl.pallas_call(
        _swa_kernel,
        grid=(b, l // BLOCK_A),
        in_specs=[
            pl.BlockSpec(memory_space=pltpu.SMEM),
            pl.BlockSpec((1, BLOCK_A, n), lambda i, j: (i, j, 0)),
            pl.BlockSpec((1, BLOCK_A, kv_w), lambda i, j: (i, jnp.maximum(j - 1, 0), nq // kv_w)),
            pl.BlockSpec((1, HD_A), lambda i, j: (0, 0)),
            pl.BlockSpec((1, HD_A), lambda i, j: (0, 0)),
            pl.BlockSpec((nq, LANES), lambda i, j: (0, 0)),
        ],
        out_specs=pl.BlockSpec((1, BLOCK_A, nq), lambda i, j: (i, j, 0)),
        out_shape=jax.ShapeDtypeStruct((b, l, nq), BF16),
        compiler_params=_params("parallel", "parallel"),
        name="swa_attention",
    )(sinks, qkv, qkv, q_norm.reshape(1, HD_A), k_norm.reshape(1, HD_A), head_sel)


def _split3(x):
    hi = x.astype(BF16)
    r1 = x - hi.astype(F32)
    mid = r1.astype(BF16)
    lo = (r1 - mid.astype(F32)).astype(BF16)
    return hi, mid, lo


def _gla_kernel(q_ref, k_ref, v_ref, r_ref, gl_ref, wgb_ref, gb_ref, on_ref, o_ref, st_ref,
                *, tb, dk, dv):
    @pl.when(pl.program_id(1) == 0)
    def _():
        st_ref[...] = jnp.zeros_like(st_ref)

    ti = lax.broadcasted_iota(jnp.int32, (CHUNK, CHUNK), 0)
    si = lax.broadcasted_iota(jnp.int32, (CHUNK, CHUNK), 1)
    causal = si <= ti
    tri = causal.astype(BF16)
    wgb = wgb_ref[...]
    gb = gb_ref[...]
    on = on_ref[...]

    def decays(c):
        rs = pl.ds(c * CHUNK, CHUNK)
        gate = _dot(gl_ref[0, rs, :].astype(BF16), wgb) + gb
        log_a = (jnp.minimum(gate, 0.0) - jnp.log1p(jnp.exp(-jnp.abs(gate)))) / GATE_TAU
        hi, mid, lo = _split3(log_a)
        bcum = _dot(tri, hi) + _dot(tri, mid) + _dot(tri, lo)
        b_last = bcum[CHUNK - 1:CHUNK, :]
        q = q_ref[0, rs, :].astype(F32)
        k = k_ref[0, rs, :].astype(F32)
        q_dec = (q * dk ** -0.5 * jnp.exp(bcum)).astype(BF16)
        k_inv = (k * jnp.exp(-bcum)).astype(BF16)
        k_end = (k * jnp.exp(b_last - bcum)).astype(BF16)
        return q_dec, k_inv, k_end, jnp.exp(b_last)

    def recur(c, q_dec, k_inv, k_end, decay):
        rs = pl.ds(c * CHUNK, CHUNK)
        for h in range(N_HEADS_B):
            ks = slice(h * dk, (h + 1) * dk)
            vs = slice(h * dv, (h + 1) * dv)
            v = v_ref[0, rs, vs].astype(BF16)
            a = jnp.where(causal, _nt_dot(q_dec[:, ks], k_inv[:, ks]), 0.0)
            st = st_ref[h]
            o = _dot(a.astype(BF16), v) + _nt_dot(q_dec[:, ks], st.astype(BF16))
            st_ref[h] = st * decay[:, ks] + _tn_dot(v, k_end[:, ks])
            o = _rms(o, on)
            r = r_ref[0, rs, vs].astype(F32)
            o_ref[0, rs, vs] = (o * (r * (1.0 / (1.0 + jnp.exp(-r))))).astype(o_ref.dtype)

    n_chunks = tb // CHUNK
    ready = decays(0)
    for c in range(n_chunks):
        cur = ready
        if c + 1 < n_chunks:
            ready = decays(c + 1)
        recur(c, *cur)


def gla_attention(proj, g_low, w_gate_b, gate_bias, o_norm, *, dk, dv, tb):
    b, l, _ = proj.shape
    dk_all, dv_all = N_HEADS_B * dk, N_HEADS_B * dv
    return pl.pallas_call(
        functools.partial(_gla_kernel, tb=tb, dk=dk, dv=dv),
        grid=(b, l // tb),
        in_specs=[
            pl.BlockSpec((1, tb, dk_all), lambda i, j: (i, j, 0)),
            pl.BlockSpec((1, tb, dk_all), lambda i, j: (i, j, 1)),
            pl.BlockSpec((1, tb, dv_all), lambda i, j: (i, j, 2 * dk_all // dv_all)),
            pl.BlockSpec((1, tb, dv_all), lambda i, j: (i, j, 2 * dk_all // dv_all + 1)),
            pl.BlockSpec((1, tb, LANES), lambda i, j: (i, j, 0)),
            pl.BlockSpec((LANES, dk_all), lambda i, j: (0, 0)),
            pl.BlockSpec((1, dk_all), lambda i, j: (0, 0)),
            pl.BlockSpec((1, dv), lambda i, j: (0, 0)),
        ],
        out_specs=pl.BlockSpec((1, tb, dv_all), lambda i, j: (i, j, 0)),
        out_shape=jax.ShapeDtypeStruct((b, l, dv_all), BF16),
        scratch_shapes=[pltpu.VMEM((N_HEADS_B, dv, dk), F32)],
        compiler_params=_params("parallel", "arbitrary"),
        name="gla_attention",
    )(proj, proj, proj, proj, g_low, w_gate_b, gate_bias.reshape(1, dk_all), o_norm.reshape(1, dv))


def _dsa_prep_kernel(x_ref, g_ref, win_ref, qln_ref, kvn_ref, wuq_ref, wuk_ref, qan_ref, wiq_ref,
                     ikn_ref, ikb_ref, qabs_ref, qidx_ref, widx_ref, ckv_ref, ckvt_ref, kidx_ref):
    xn = _rms(x_ref[...], g_ref[...]).astype(BF16)
    proj = _nt_dot(xn, win_ref[...])
    o_kv = Q_RANK_C
    o_ki = Q_RANK_C + KV_RANK_C
    o_wi = o_ki + IDX_DIM
    c_q = _rms(proj[:, :o_kv], qln_ref[...]).astype(BF16)
    c_kv = _rms(proj[:, o_kv:o_ki], kvn_ref[...])
    ckv_ref[...] = c_kv.astype(BF16)
    ckvt_ref[...] = c_kv.T.astype(BF16)
    k_idx = proj[:, o_ki:o_wi]
    mu = jnp.mean(k_idx, axis=-1, keepdims=True)
    kc = k_idx - mu
    k_idx = kc * lax.rsqrt(jnp.mean(kc * kc, axis=-1, keepdims=True) + EPS)
    kidx_ref[...] = (k_idx * ikn_ref[...] + ikb_ref[...]).astype(BF16)
    widx_ref[...] = proj[:, o_wi:o_wi + N_IDX_HEADS] * (N_IDX_HEADS ** -0.5 * IDX_DIM ** -0.5)
    q_nope = _dot(c_q, wuq_ref[...])
    q_idx = _dot(c_q, wiq_ref[...])
    qan = qan_ref[...] * (KV_RANK_C ** -0.5 * LOG2E)
    for h in range(N_HEADS_C):
        qh = q_nope[:, h * QK_HEAD_C:(h + 1) * QK_HEAD_C].astype(BF16)
        qa = _rms(_dot(qh, wuk_ref[h]), qan).astype(BF16)
        qi = q_idx[:, h * IDX_DIM:(h + 1) * IDX_DIM].astype(BF16)
        for j in range(qabs_ref.shape[0]):
            qabs_ref[j, h] = qa[j * C_BLOCK:(j + 1) * C_BLOCK]
            qidx_ref[j, h] = qi[j * C_BLOCK:(j + 1) * C_BLOCK]


def dsa_prep(x, g, w_in, q_lat_norm, kv_norm, w_uq, w_uk, q_abs_norm, w_iq, idx_k_norm, idx_k_bias,
             *, tm):
    m, d = x.shape
    nblk = m // C_BLOCK
    bpt = tm // C_BLOCK
    n_in = w_in.shape[0]
    const2 = lambda i: (0, 0)
    return pl.pallas_call(
        _dsa_prep_kernel,
        grid=(m // tm,),
        in_specs=[
            pl.BlockSpec((tm, d), lambda i: (i, 0)),
            pl.BlockSpec((1, d), const2),
            pl.BlockSpec((n_in, d), const2),
            pl.BlockSpec((1, Q_RANK_C), const2),
            pl.BlockSpec((1, KV_RANK_C), const2),
            pl.BlockSpec(w_uq.shape, const2),
            pl.BlockSpec(w_uk.shape, lambda i: (0, 0, 0)),
            pl.BlockSpec((1, KV_RANK_C), const2),
            pl.BlockSpec(w_iq.shape, const2),
            pl.BlockSpec((1, IDX_DIM), const2),
            pl.BlockSpec((1, IDX_DIM), const2),
        ],
        out_specs=[
            pl.BlockSpec((bpt, N_HEADS_C, C_BLOCK, KV_RANK_C), lambda i: (i, 0, 0, 0)),
            pl.BlockSpec((bpt, N_IDX_HEADS, C_BLOCK, IDX_DIM), lambda i: (i, 0, 0, 0)),
            pl.BlockSpec((tm, N_IDX_HEADS), lambda i: (i, 0)),
            pl.BlockSpec((tm, KV_RANK_C), lambda i: (i, 0)),
            pl.BlockSpec((KV_RANK_C, tm), lambda i: (0, i)),
            pl.BlockSpec((tm, IDX_DIM), lambda i: (i, 0)),
        ],
        out_shape=[
            jax.ShapeDtypeStruct((nblk, N_HEADS_C, C_BLOCK, KV_RANK_C), BF16),
            jax.ShapeDtypeStruct((nblk, N_IDX_HEADS, C_BLOCK, IDX_DIM), BF16),
            jax.ShapeDtypeStruct((m, N_IDX_HEADS), F32),
            jax.ShapeDtypeStruct((m, KV_RANK_C), BF16),
            jax.ShapeDtypeStruct((KV_RANK_C, m), BF16),
            jax.ShapeDtypeStruct((m, IDX_DIM), BF16),
        ],
        compiler_params=_params("parallel"),
        name="dsa_prep",
    )(x, g.reshape(1, d), w_in, q_lat_norm.reshape(1, -1), kv_norm.reshape(1, -1), w_uq, w_uk,
      q_abs_norm.reshape(1, -1), w_iq, idx_k_norm.reshape(1, -1), idx_k_bias.reshape(1, -1))


def _sortable_key(s):
    bits = pltpu.bitcast(s, jnp.int32)
    return bits ^ ((bits >> 31) & 0x7FFFFFFF)


def _dsa_attn_kernel(qabs_ref, qidx_ref, widx_ref, ckv_ref, ckvt_ref, kidx_ref, wuv_ref, qan_ref,
                     kvn_ref, o_ref, key_ref, m_ref, acc_ref, *, topk, idx_bits):
    n = pl.program_id(1)
    tk = KEY_TILE_C
    n_tiles = (n * C_BLOCK + C_BLOCK + tk - 1) // tk
    q_row = n * C_BLOCK + lax.broadcasted_iota(jnp.int32, (C_BLOCK, tk), 0)
    k_col = lax.broadcasted_iota(jnp.int32, (C_BLOCK, tk), 1)
    k_row = lax.broadcasted_iota(jnp.int32, (tk, C_BLOCK), 0)
    widx = widx_ref[0, 0]

    def tile(t):
        return pl.ds(pl.multiple_of(t * tk, tk), tk)

    def score_tile(t, carry):
        kt = kidx_ref[0, tile(t), :]
        score = jnp.zeros((C_BLOCK, tk), F32)
        for h in range(N_IDX_HEADS):
            rel = jnp.maximum(_nt_dot(qidx_ref[0, h], kt), 0.0)
            score = score + rel * widx[:, h:h + 1]
        score = jnp.where(t * tk + k_col <= q_row, score, NEG_INF)
        key_ref[tile(t), :] = _sortable_key(score.T)
        return carry

    lax.fori_loop(0, n_tiles, score_tile, 0)

    def count(pred):
        def body(t, cnt):
            hit = pred(key_ref[tile(t), :], t).astype(jnp.int32)
            return cnt + jnp.sum(hit.reshape(tk // 8, 8, C_BLOCK), axis=0)
        cnt = lax.fori_loop(0, n_tiles, body, jnp.zeros((8, C_BLOCK), jnp.int32))
        return jnp.sum(cnt, axis=0, keepdims=True)

    def bisect(i, lo):
        cand = lo + lax.shift_left(jnp.int32(1), 31 - i)
        return jnp.where(count(lambda k, t: k >= cand) >= topk, cand, lo)

    thresh = lax.fori_loop(0, 32, bisect, jnp.full((1, C_BLOCK), INT_MIN, jnp.int32))

    n_ge = count(lambda k, t: k >= thresh)
    excess = jnp.max(jnp.where((n_ge > topk) & (thresh > NEG_KEY), 1, 0))

    @pl.when(excess > 0)
    def _():
        want = topk - count(lambda k, t: k > thresh)

        def index_bisect(i, j0):
            cand = j0 + lax.shift_left(jnp.int32(1), idx_bits - 1 - i)
            before = count(lambda k, t: (k == thresh) & (t * tk + k_row < cand))
            return jnp.where(before < want, cand, j0)

        j0 = lax.fori_loop(0, idx_bits, index_bisect, jnp.zeros((1, C_BLOCK), jnp.int32))

        def demote(t, carry):
            keys = key_ref[tile(t), :]
            drop = (keys == thresh) & (t * tk + k_row > j0)
            key_ref[tile(t), :] = jnp.where(drop, thresh - 1, keys)
            return carry

        lax.fori_loop(0, n_tiles, demote, 0)

    thr = jnp.maximum(thresh, NEG_KEY + 1)
    thr2 = jnp.concatenate([thr, thr], axis=1)
    acc_ref[...] = jnp.zeros_like(acc_ref)
    ones_rows = jnp.ones((ONES_ROWS_C, tk), BF16)
    n_pairs = N_HEADS_C // 2

    def pair_q(p):
        return qabs_ref[0, 2 * p:2 * p + 2].reshape(2 * C_BLOCK, KV_RANK_C)

    bound = (KV_RANK_C * BOUND_SLACK_C * (KV_RANK_C ** -0.5 * LOG2E)
             * jnp.max(jnp.abs(qan_ref[...])) * jnp.max(jnp.abs(kvn_ref[...])))
    small_logits = bound <= MAX_FIXED_OFFSET_C

    def attn_loop(accumulate):
        def attn_tile(t, carry):
            kv = ckv_ref[0, tile(t), :]
            kvt = jnp.concatenate([ckvt_ref[:, tile(t)], ones_rows], axis=0)
            keys = key_ref[tile(t), :]
            sel = jnp.concatenate([keys, keys], axis=1) >= thr2

            def logits(p):
                return jnp.where(sel, _nt_dot(kv, pair_q(p)), NEG_INF)

            pending = [logits(p) for p in range(ATTN_AHEAD_C)]
            for p in range(n_pairs):
                if p + ATTN_AHEAD_C < n_pairs:
                    pending.append(logits(p + ATTN_AHEAD_C))
                accumulate(p, pending.pop(0), kvt)
            return carry

        lax.fori_loop(0, n_tiles, attn_tile, 0)

    def accumulate_fixed(p, s, kvt):
        acc_ref[p] += _dot(kvt, jnp.exp2(s - bound).astype(BF16))

    def accumulate_online(p, s, kvt):
        m_old = m_ref[p]
        m_new = jnp.maximum(m_old, jnp.max(s, axis=0, keepdims=True))
        pe = jnp.exp2(s - m_new)
        acc_ref[p] = jnp.exp2(m_old - m_new) * acc_ref[p] + _dot(kvt, pe.astype(BF16))
        m_ref[p] = m_new

    @pl.when(small_logits)
    def _():
        attn_loop(accumulate_fixed)

    @pl.when(jnp.logical_not(small_logits))
    def _():
        m_ref[...] = jnp.full(m_ref.shape, NEG_INF, F32)
        attn_loop(accumulate_online)

    for p in range(N_HEADS_C // 2):
        denom = acc_ref[p, KV_RANK_C:KV_RANK_C + 1, :]
        o_lat_t = (acc_ref[p, :KV_RANK_C, :] / denom).astype(BF16)
        for e in range(2):
            h = 2 * p + e
            o_ref[0, :, h * V_HEAD_C:(h + 1) * V_HEAD_C] = _tn_dot(
                o_lat_t[:, e * C_BLOCK:(e + 1) * C_BLOCK], wuv_ref[h]).astype(o_ref.dtype)


def dsa_attention(qabs, qidx, widx, ckv, ckvt, kidx, w_uv, q_abs_norm, kv_norm, *, batch, topk):
    nblk = qabs.shape[0]
    nb = nblk // batch
    l = nb * C_BLOCK
    assert l % KEY_TILE_C == 0 and l >= topk
    ckv = ckv.reshape(batch, l, KV_RANK_C)
    kidx = kidx.reshape(batch, l, IDX_DIM)
    widx = widx.reshape(batch, nb, C_BLOCK, N_IDX_HEADS)
    return pl.pallas_call(
        functools.partial(_dsa_attn_kernel, topk=topk, idx_bits=(l - 1).bit_length()),
        grid=(batch, nb),
        in_specs=[
            pl.BlockSpec((1, N_HEADS_C, C_BLOCK, KV_RANK_C), lambda b, j: (b * nb + j, 0, 0, 0)),
            pl.BlockSpec((1, N_IDX_HEADS, C_BLOCK, IDX_DIM), lambda b, j: (b * nb + j, 0, 0, 0)),
            pl.BlockSpec((1, 1, C_BLOCK, N_IDX_HEADS), lambda b, j: (b, j, 0, 0)),
            pl.BlockSpec((1, l, KV_RANK_C), lambda b, j: (b, 0, 0)),
            pl.BlockSpec((KV_RANK_C, l), lambda b, j: (0, b)),
            pl.BlockSpec((1, l, IDX_DIM), lambda b, j: (b, 0, 0)),
            pl.BlockSpec(w_uv.shape, lambda b, j: (0, 0, 0)),
            pl.BlockSpec((1, KV_RANK_C), lambda b, j: (0, 0)),
            pl.BlockSpec((1, KV_RANK_C), lambda b, j: (0, 0)),
        ],
        out_specs=pl.BlockSpec((1, C_BLOCK, N_HEADS_C * V_HEAD_C), lambda b, j: (b, j, 0)),
        out_shape=jax.ShapeDtypeStruct((batch, l, N_HEADS_C * V_HEAD_C), BF16),
        scratch_shapes=[
            pltpu.VMEM((l, C_BLOCK), jnp.int32),
            pltpu.VMEM((N_HEADS_C // 2, 1, 2 * C_BLOCK), F32),
            pltpu.VMEM((N_HEADS_C // 2, KV_RANK_C + ONES_ROWS_C, 2 * C_BLOCK), F32),
        ],
        compiler_params=_params("parallel", "arbitrary"),
        name="dsa_attention",
    )(qabs, qidx, widx, ckv, ckvt, kidx, w_uv, q_abs_norm.reshape(1, -1), kv_norm.reshape(1, -1))


TM = 512
TF = 512


def _swa_layer(h, batch, norm_mix, w_in, q_norm, k_norm, sinks, wo):
    m, d = h.shape
    qkv = norm_matmul(h, norm_mix, w_in.astype(BF16), tm=TM, tn=w_in.shape[1], out_dtype=BF16)
    o = swa_attention(qkv.reshape(batch, m // batch, -1), q_norm, k_norm, sinks)
    return matmul_res(o.reshape(m, -1), wo.astype(BF16), h, tm=TM, tn=d)


def _gla_layer(h, batch, norm_mix, w_in, w_gate_b, gate_bias, o_norm, wo):
    m, d = h.shape
    dk_all = w_gate_b.shape[1]
    dv_all = wo.shape[0]
    n_main = 2 * dk_all + 2 * dv_all
    wgb_pad = jnp.pad(w_gate_b.astype(BF16), ((0, LANES - GATE_RANK), (0, 0)))
    proj, g_low = norm_matmul_gate(h, norm_mix, w_in, n_main, tm=2 * TM, tn=n_main // 8)
    o = gla_attention(proj.reshape(batch, m // batch, -1), g_low.reshape(batch, m // batch, -1),
                      wgb_pad, gate_bias, o_norm,
                      dk=dk_all // N_HEADS_B, dv=dv_all // N_HEADS_B, tb=256)
    return matmul_res(o.reshape(m, -1), wo.astype(BF16), h, tm=TM, tn=d)


def _dsa_layer(h, batch, norm_mix, w_in, q_lat_norm, kv_norm, w_uq, w_uk, q_abs_norm, w_iq,
               idx_k_norm, idx_k_bias, w_uv, wo):
    m, d = h.shape
    l = m // batch
    qabs, qidx, widx, ckv, ckvt, kidx = dsa_prep(
        h, norm_mix, w_in.T.astype(BF16), q_lat_norm, kv_norm, w_uq.astype(BF16), w_uk.astype(BF16),
        q_abs_norm, w_iq.astype(BF16), idx_k_norm, idx_k_bias, tm=TM)
    o = dsa_attention(qabs, qidx, widx, ckv, ckvt, kidx, w_uv.astype(BF16), q_abs_norm, kv_norm, batch=batch,
                      topk=min(TOPK_MAX, l // 4))
    return matmul_res(o.reshape(m, -1), wo.astype(BF16), h, tm=TM, tn=d)


def kernel(x, l0_norm_mix, l0_w_in, l0_q_norm, l0_k_norm, l0_sinks, l0_wo, l0_norm_ffn, l0_w_gate_up, l0_w_down, l1_norm_mix, l1_w_in, l1_w_gate_b, l1_gate_bias, l1_o_norm, l1_wo, l1_norm_ffn, l1_w_gate_up, l1_w_down, l2_norm_mix, l2_w_in, l2_q_lat_norm, l2_kv_norm, l2_w_uq, l2_w_uk, l2_q_abs_norm, l2_w_iq, l2_idx_k_norm, l2_idx_k_bias, l2_w_uv, l2_wo, l2_norm_ffn, l2_w_gate_up, l2_w_down, l3_norm_mix, l3_w_in, l3_q_norm, l3_k_norm, l3_sinks, l3_wo, l3_norm_ffn, l3_w_gate_up, l3_w_down):
    batch, seq, d = x.shape
    h = x.reshape(batch * seq, d)

    def channel_mix(h, norm_ffn, w_gate_up, w_down):
        return ffn(h, norm_ffn, w_gate_up, w_down, tm=2 * TM, tf=TF // 2)

    h = _swa_layer(h, batch, l0_norm_mix, l0_w_in, l0_q_norm, l0_k_norm, l0_sinks, l0_wo)
    h = channel_mix(h, l0_norm_ffn, l0_w_gate_up, l0_w_down)
    h = _gla_layer(h, batch, l1_norm_mix, l1_w_in, l1_w_gate_b, l1_gate_bias, l1_o_norm, l1_wo)
    h = channel_mix(h, l1_norm_ffn, l1_w_gate_up, l1_w_down)
    h = _dsa_layer(h, batch, l2_norm_mix, l2_w_in, l2_q_lat_norm, l2_kv_norm, l2_w_uq, l2_w_uk,
                   l2_q_abs_norm, l2_w_iq, l2_idx_k_norm, l2_idx_k_bias, l2_w_uv, l2_wo)
    h = channel_mix(h, l2_norm_ffn, l2_w_gate_up, l2_w_down)
    h = _swa_layer(h, batch, l3_norm_mix, l3_w_in, l3_q_norm, l3_k_norm, l3_sinks, l3_wo)
    h = channel_mix(h, l3_norm_ffn, l3_w_gate_up, l3_w_down)
    return h.reshape(batch, seq, d)
```

```python
import functools

import jax
import jax.numpy as jnp
import numpy as np
from jax import lax
from jax.experimental import pallas as pl
from jax.experimental.pallas import tpu as pltpu

F32 = jnp.float32
BF16 = jnp.bfloat16
EPS = 1e-6
NEG_INF = -1e30
LANES = 128

HD_A = 64
N_KV_A = 4
GROUP_A = 8
BLOCK_A = 128
N_HEADS_B = 4
GATE_RANK = 16
GATE_TAU = 16.0
CHUNK = 64
N_HEADS_C = 16
QK_HEAD_C = 128
V_HEAD_C = 128
Q_RANK_C = 512
KV_RANK_C = 256
N_IDX_HEADS = 16
IDX_DIM = 64
TOPK_MAX = 256
C_BLOCK = 128
KEY_TILE_C = 512
ATTN_AHEAD_C = 2
ONES_ROWS = 16
BOUND_SLACK_C = 1.01
MAX_FIXED_OFFSET_C = 56.0
INT_MIN = -2 ** 31
LOG2E = 1.4426950408889634
_NEG_BITS = int(np.float32(NEG_INF).view(np.int32))
NEG_KEY = _NEG_BITS ^ ((_NEG_BITS >> 31) & 0x7FFFFFFF)


def _nt_dot(a, b):
    return lax.dot_general(a, b, (((1,), (1,)), ((), ())), preferred_element_type=F32)


def _tn_dot(a, b):
    return lax.dot_general(a, b, (((0,), (0,)), ((), ())), preferred_element_type=F32)


def _dot(a, b):
    return jnp.dot(a, b, preferred_element_type=F32)


def _rms(x, g):
    return x * lax.rsqrt(jnp.mean(x * x, axis=-1, keepdims=True) + EPS) * g


def _params(*sem, vmem_bytes=None):
    return pltpu.CompilerParams(dimension_semantics=sem, vmem_limit_bytes=vmem_bytes)


def _norm_matmul_kernel(x_ref, g_ref, w_ref, o_ref, xn_ref):
    @pl.when(pl.program_id(1) == 0)
    def _():
        xn_ref[...] = _rms(x_ref[...], g_ref[...]).astype(BF16)

    o_ref[...] = _dot(xn_ref[...], w_ref[...].astype(BF16)).astype(o_ref.dtype)


def norm_matmul(x, g, w, *, tm, tn, out_dtype=F32):
    m, d = x.shape
    n = w.shape[1]
    assert m % tm == 0 and n % tn == 0
    return pl.pallas_call(
        _norm_matmul_kernel,
        grid=(m // tm, n // tn),
        in_specs=[
            pl.BlockSpec((tm, d), lambda i, j: (i, 0)),
            pl.BlockSpec((1, d), lambda i, j: (0, 0)),
            pl.BlockSpec((d, tn), lambda i, j: (0, j)),
        ],
        out_specs=pl.BlockSpec((tm, tn), lambda i, j: (i, j)),
        out_shape=jax.ShapeDtypeStruct((m, n), out_dtype),
        scratch_shapes=[pltpu.VMEM((tm, d), BF16)],
        compiler_params=_params("parallel", "arbitrary"),
        name="norm_matmul",
    )(x, g.reshape(1, d), w)


def _norm_matmul_gate_kernel(x_ref, g_ref, w_ref, wg_ref, o_ref, gl_ref, xn_ref):
    @pl.when(pl.program_id(1) == 0)
    def _():
        xn = _rms(x_ref[...], g_ref[...]).astype(BF16)
        xn_ref[...] = xn
        gl_ref[...] = _nt_dot(xn, wg_ref[...])

    o_ref[...] = _nt_dot(xn_ref[...], w_ref[...].astype(BF16)).astype(o_ref.dtype)


def norm_matmul_gate(x, g, w, n_main, *, tm, tn):
    m, d = x.shape
    assert m % tm == 0 and n_main % tn == 0
    w_t = w.T
    w_gate = jnp.pad(w_t[n_main:].astype(BF16), ((0, LANES - (w.shape[1] - n_main)), (0, 0)))
    vmem_bytes = (2 * tm * d * 4 + tm * d * 2 + 2 * d * tn * 4 + d * tn * 2
                  + 2 * tm * tn * 2 + tm * tn * 4 + 2 * tm * LANES * 4 + 2 * d * LANES * 2)
    return pl.pallas_call(
        _norm_matmul_gate_kernel,
        grid=(m // tm, n_main // tn),
        in_specs=[
            pl.BlockSpec((tm, d), lambda i, j: (i, 0)),
            pl.BlockSpec((1, d), lambda i, j: (0, 0)),
            pl.BlockSpec((tn, d), lambda i, j: (j, 0)),
            pl.BlockSpec((LANES, d), lambda i, j: (0, 0)),
        ],
        out_specs=[
            pl.BlockSpec((tm, tn), lambda i, j: (i, j)),
            pl.BlockSpec((tm, LANES), lambda i, j: (i, 0)),
        ],
        out_shape=[
            jax.ShapeDtypeStruct((m, n_main), BF16),
            jax.ShapeDtypeStruct((m, LANES), F32),
        ],
        scratch_shapes=[pltpu.VMEM((tm, d), BF16)],
        compiler_params=_params("parallel", "arbitrary", vmem_bytes=vmem_bytes),
        name="norm_matmul_gate",
    )(x, g.reshape(1, d), w_t, w_gate)


def _matmul_res_kernel(a_ref, w_ref, r_ref, o_ref):
    o_ref[...] = r_ref[...] + _dot(a_ref[...], w_ref[...].astype(BF16))


def matmul_res(a, w, r, *, tm, tn):
    m, k = a.shape
    n = w.shape[1]
    assert m % tm == 0 and n % tn == 0
    return pl.pallas_call(
        _matmul_res_kernel,
        grid=(m // tm, n // tn),
        in_specs=[
            pl.BlockSpec((tm, k), lambda i, j: (i, 0)),
            pl.BlockSpec((k, tn), lambda i, j: (0, j)),
            pl.BlockSpec((tm, tn), lambda i, j: (i, j)),
        ],
        out_specs=pl.BlockSpec((tm, tn), lambda i, j: (i, j)),
        out_shape=jax.ShapeDtypeStruct((m, n), F32),
        compiler_params=_params("parallel", "arbitrary"),
        name="matmul_res",
    )(a, w, r)


def _ffn_kernel(x_ref, g_ref, wg_ref, wu_ref, wd_ref, o_ref, xn_ref):
    @pl.when(pl.program_id(1) == 0)
    def _():
        x = x_ref[...]
        xn_ref[...] = _rms(x, g_ref[...]).astype(BF16)
        o_ref[...] = x

    xn = xn_ref[...]
    gate = _dot(xn, wg_ref[...].astype(BF16))
    up = _dot(xn, wu_ref[...].astype(BF16))
    act = (gate * (1.0 / (1.0 + jnp.exp(-gate))) * up).astype(BF16)
    o_ref[...] += _dot(act, wd_ref[...].astype(BF16))


def ffn(x, g, w_gate_up, w_down, *, tm, tf):
    m, d = x.shape
    ff = w_down.shape[0]
    nf = ff // tf
    assert m % tm == 0 and ff % tf == 0
    w_bytes = w_down.dtype.itemsize
    vmem_bytes = (4 * tm * d * 4 + tm * d * 2 + 6 * d * tf * w_bytes
                  + 3 * tm * tf * 4 + 3 * d * tf * 2)
    return pl.pallas_call(
        _ffn_kernel,
        grid=(m // tm, nf),
        in_specs=[
            pl.BlockSpec((tm, d), lambda i, j: (i, 0)),
            pl.BlockSpec((1, d), lambda i, j: (0, 0)),
            pl.BlockSpec((d, tf), lambda i, j: (0, j)),
            pl.BlockSpec((d, tf), lambda i, j: (0, j + nf)),
            pl.BlockSpec((tf, d), lambda i, j: (j, 0)),
        ],
        out_specs=pl.BlockSpec((tm, d), lambda i, j: (i, 0)),
        out_shape=jax.ShapeDtypeStruct((m, d), F32),
        scratch_shapes=[pltpu.VMEM((tm, d), BF16)],
        compiler_params=_params("parallel", "arbitrary", vmem_bytes=vmem_bytes),
        name="ffn",
    )(x, g.reshape(1, d), w_gate_up, w_gate_up, w_down)


def _swa_kernel(sink_ref, cur_ref, prev_ref, qn_ref, kn_ref, sel_ref, o_ref):
    n = pl.program_id(1)
    nq = N_KV_A * GROUP_A * HD_A
    nkv = N_KV_A * HD_A
    kj = lax.broadcasted_iota(jnp.int32, (2 * BLOCK_A, 2 * BLOCK_A), 0)
    qi = lax.broadcasted_iota(jnp.int32, (2 * BLOCK_A, 2 * BLOCK_A), 1) & (BLOCK_A - 1)
    mask2 = (kj > qi) & (kj <= qi + BLOCK_A) & ((n > 0) | (kj >= BLOCK_A))
    first_half = lax.broadcasted_iota(jnp.int32, (1, 2 * BLOCK_A), 1) < BLOCK_A
    q_bf = cur_ref[0, :, :nq]
    xq = q_bf.astype(F32)
    sq = xq * xq
    sq_hi = sq.astype(BF16)
    sq_lo = (sq - sq_hi.astype(F32)).astype(BF16)
    ssq = _dot(sq_hi, sel_ref[...]) + _dot(sq_lo, sel_ref[...])
    rq_t = (lax.rsqrt(ssq * (1.0 / HD_A) + EPS) * LOG2E).T
    kscale = qn_ref[...] * kn_ref[...] * HD_A ** -0.5
    zeros = jnp.zeros((2 * BLOCK_A, HD_A), F32)
    v_all = jnp.concatenate([prev_ref[0, :, nkv:2 * nkv], cur_ref[0, :, nq + nkv:nq + 2 * nkv]], axis=0)
    v_t = v_all.astype(F32).T.astype(BF16)
    ones_rows = jnp.ones((ONES_ROWS, 2 * BLOCK_A), BF16)
    v_ones = [jnp.concatenate([v_t[h * HD_A:(h + 1) * HD_A, :], ones_rows], axis=0)
              for h in range(N_KV_A)]
    k_pads = []
    for h in range(N_KV_A):
        k = jnp.concatenate([prev_ref[0, :, h * HD_A:(h + 1) * HD_A],
                             cur_ref[0, :, nq + h * HD_A:nq + (h + 1) * HD_A]], axis=0).astype(F32)
        kn = k * lax.rsqrt(jnp.mean(k * k, axis=-1, keepdims=True) + EPS) * kscale
        k_pads.append((jnp.concatenate([kn, zeros], axis=1).astype(BF16),
                       jnp.concatenate([zeros, kn], axis=1).astype(BF16)))

    def logits(i):
        j = 2 * i
        k_pad = k_pads[j // GROUP_A]
        q_tile = q_bf[:, j * HD_A:(j + 2) * HD_A]
        rq = jnp.concatenate([rq_t[j:j + 1, :], rq_t[j + 1:j + 2, :]], axis=1)
        s = jnp.concatenate([_nt_dot(k_pad[0], q_tile), _nt_dot(k_pad[1], q_tile)], axis=1) * rq
        return jnp.where(mask2, s, NEG_INF)

    def attend(i, s):
        j = 2 * i
        sink = jnp.where(first_half, sink_ref[j], sink_ref[j + 1]) * LOG2E
        mx = jnp.maximum(jnp.max(s, axis=0, keepdims=True), sink)
        p = jnp.exp2(s - mx)
        return _dot(v_ones[j // GROUP_A], p.astype(BF16)), jnp.exp2(sink - mx)

    def finish(i, pv, sink_term):
        o_t = pv[:HD_A] / (pv[HD_A:HD_A + 1] + sink_term)
        o_pair = jnp.concatenate([o_t[:, :BLOCK_A], o_t[:, BLOCK_A:]], axis=0)
        o_ref[0, :, 2 * i * HD_A:(2 * i + 2) * HD_A] = o_pair.T.astype(o_ref.dtype)

    n_pairs = N_KV_A * GROUP_A // 2
    ahead = 3
    pending = [logits(i) for i in range(ahead)]
    unfinished = None
    for i in range(n_pairs):
        if i + ahead < n_pairs:
            pending.append(logits(i + ahead))
        out = attend(i, pending.pop(0))
        if unfinished is not None:
            finish(i - 1, *unfinished)
        unfinished = out
    finish(n_pairs - 1, *unfinished)


def swa_attention(qkv, q_norm, k_norm, sinks):
    b, l, n = qkv.shape
    nq = N_KV_A * GROUP_A * HD_A
    kv_w = 2 * N_KV_A * HD_A
    head_sel = (jnp.arange(nq)[:, None] // HD_A == jnp.arange(LANES)[None, :]).astype(BF16)
    return pl.pallas_call(
        _swa_kernel,
        grid=(b, l // BLOCK_A),
        in_specs=[
            pl.BlockSpec(memory_space=pltpu.SMEM),
            pl.BlockSpec((1, BLOCK_A, n), lambda i, j: (i, j, 0)),
            pl.BlockSpec((1, BLOCK_A, kv_w), lambda i, j: (i, jnp.maximum(j - 1, 0), nq // kv_w)),
            pl.BlockSpec((1, HD_A), lambda i, j: (0, 0)),
            pl.BlockSpec((1, HD_A), lambda i, j: (0, 0)),
            pl.BlockSpec((nq, LANES), lambda i, j: (0, 0)),
        ],
        out_specs=pl.BlockSpec((1, BLOCK_A, nq), lambda i, j: (i, j, 0)),
        out_shape=jax.ShapeDtypeStruct((b, l, nq), BF16),
        compiler_params=_params("parallel", "parallel"),
        name="swa_attention",
    )(sinks, qkv, qkv, q_norm.reshape(1, HD_A), k_norm.reshape(1, HD_A), head_sel)


def _split3(x):
    hi = x.astype(BF16)
    r1 = x - hi.astype(F32)
    mid = r1.astype(BF16)
    lo = (r1 - mid.astype(F32)).astype(BF16)
    return hi, mid, lo


def _gla_kernel(q_ref, k_ref, v_ref, r_ref, gl_ref, wgb_ref, gb_ref, on_ref, o_ref, st_ref,
                *, tb, dk, dv):
    @pl.when(pl.program_id(1) == 0)
    def _():
        st_ref[...] = jnp.zeros_like(st_ref)

    ti = lax.broadcasted_iota(jnp.int32, (CHUNK, CHUNK), 0)
    si = lax.broadcasted_iota(jnp.int32, (CHUNK, CHUNK), 1)
    causal = si <= ti
    tri = causal.astype(BF16)
    wgb = wgb_ref[...]
    gb = gb_ref[...]
    on = on_ref[...]

    def decays(c):
        rs = pl.ds(c * CHUNK, CHUNK)
        gate = _dot(gl_ref[0, rs, :].astype(BF16), wgb) + gb
        log_a = (jnp.minimum(gate, 0.0) - jnp.log1p(jnp.exp(-jnp.abs(gate)))) / GATE_TAU
        hi, mid, lo = _split3(log_a)
        bcum = _dot(tri, hi) + _dot(tri, mid) + _dot(tri, lo)
        b_last = bcum[CHUNK - 1:CHUNK, :]
        q = q_ref[0, rs, :].astype(F32)
        k = k_ref[0, rs, :].astype(F32)
        q_dec = (q * dk ** -0.5 * jnp.exp(bcum)).astype(BF16)
        k_inv = (k * jnp.exp(-bcum)).astype(BF16)
        k_end = (k * jnp.exp(b_last - bcum)).astype(BF16)
        return q_dec, k_inv, k_end, jnp.exp(b_last)

    def recur(c, q_dec, k_inv, k_end, decay):
        rs = pl.ds(c * CHUNK, CHUNK)
        for h in range(N_HEADS_B):
            ks = slice(h * dk, (h + 1) * dk)
            vs = slice(h * dv, (h + 1) * dv)
            v = v_ref[0, rs, vs].astype(BF16)
            a = jnp.where(causal, _nt_dot(q_dec[:, ks], k_inv[:, ks]), 0.0)
            st = st_ref[h]
            o = _dot(a.astype(BF16), v) + _nt_dot(q_dec[:, ks], st.astype(BF16))
            st_ref[h] = st * decay[:, ks] + _tn_dot(v, k_end[:, ks])
            o = _rms(o, on)
            r = r_ref[0, rs, vs].astype(F32)
            o_ref[0, rs, vs] = (o * (r * (1.0 / (1.0 + jnp.exp(-r))))).astype(o_ref.dtype)

    n_chunks = tb // CHUNK
    ready = decays(0)
    for c in range(n_chunks):
        cur = ready
        if c + 1 < n_chunks:
            ready = decays(c + 1)
        recur(c, *cur)


def gla_attention(proj, g_low, w_gate_b, gate_bias, o_norm, *, dk, dv, tb):
    b, l, _ = proj.shape
    dk_all, dv_all = N_HEADS_B * dk, N_HEADS_B * dv
    return pl.pallas_call(
        functools.partial(_gla_kernel, tb=tb, dk=dk, dv=dv),
        grid=(b, l // tb),
        in_specs=[
            pl.BlockSpec((1, tb, dk_all), lambda i, j: (i, j, 0)),
            pl.BlockSpec((1, tb, dk_all), lambda i, j: (i, j, 1)),
            pl.BlockSpec((1, tb, dv_all), lambda i, j: (i, j, 2 * dk_all // dv_all)),
            pl.BlockSpec((1, tb, dv_all), lambda i, j: (i, j, 2 * dk_all // dv_all + 1)),
            pl.BlockSpec((1, tb, LANES), lambda i, j: (i, j, 0)),
            pl.BlockSpec((LANES, dk_all), lambda i, j: (0, 0)),
            pl.BlockSpec((1, dk_all), lambda i, j: (0, 0)),
            pl.BlockSpec((1, dv), lambda i, j: (0, 0)),
        ],
        out_specs=pl.BlockSpec((1, tb, dv_all), lambda i, j: (i, j, 0)),
        out_shape=jax.ShapeDtypeStruct((b, l, dv_all), BF16),
        scratch_shapes=[pltpu.VMEM((N_HEADS_B, dv, dk), F32)],
        compiler_params=_params("parallel", "arbitrary"),
        name="gla_attention",
    )(proj, proj, proj, proj, g_low, w_gate_b, gate_bias.reshape(1, dk_all), o_norm.reshape(1, dv))


def _dsa_prep_kernel(x_ref, g_ref, win_ref, qln_ref, kvn_ref, wuq_ref, wuk_ref, qan_ref, wiq_ref,
                     ikn_ref, ikb_ref, qabs_ref, qidx_ref, widx_ref, ckv_ref, ckvt_ref, kidx_ref):
    xn = _rms(x_ref[...], g_ref[...]).astype(BF16)
    proj = _nt_dot(xn, win_ref[...])
    o_kv = Q_RANK_C
    o_ki = Q_RANK_C + KV_RANK_C
    o_wi = o_ki + IDX_DIM
    c_q = _rms(proj[:, :o_kv], qln_ref[...]).astype(BF16)
    c_kv = _rms(proj[:, o_kv:o_ki], kvn_ref[...])
    ckv_ref[...] = c_kv.astype(BF16)
    ckvt_ref[...] = c_kv.T.astype(BF16)
    k_idx = proj[:, o_ki:o_wi]
    mu = jnp.mean(k_idx, axis=-1, keepdims=True)
    kc = k_idx - mu
    k_idx = kc * lax.rsqrt(jnp.mean(kc * kc, axis=-1, keepdims=True) + EPS)
    kidx_ref[...] = (k_idx * ikn_ref[...] + ikb_ref[...]).astype(BF16)
    widx_ref[...] = proj[:, o_wi:o_wi + N_IDX_HEADS] * (N_IDX_HEADS ** -0.5 * IDX_DIM ** -0.5)
    q_nope = _dot(c_q, wuq_ref[...])
    q_idx = _dot(c_q, wiq_ref[...])
    qan = qan_ref[...] * (KV_RANK_C ** -0.5 * LOG2E)
    for h in range(N_HEADS_C):
        qh = q_nope[:, h * QK_HEAD_C:(h + 1) * QK_HEAD_C].astype(BF16)
        qa = _rms(_dot(qh, wuk_ref[h]), qan).astype(BF16)
        qi = q_idx[:, h * IDX_DIM:(h + 1) * IDX_DIM].astype(BF16)
        for j in range(qabs_ref.shape[0]):
            qabs_ref[j, h] = qa[j * C_BLOCK:(j + 1) * C_BLOCK]
            qidx_ref[j, h] = qi[j * C_BLOCK:(j + 1) * C_BLOCK]


def dsa_prep(x, g, w_in, q_lat_norm, kv_norm, w_uq, w_uk, q_abs_norm, w_iq, idx_k_norm, idx_k_bias,
             *, tm):
    m, d = x.shape
    nblk = m // C_BLOCK
    bpt = tm // C_BLOCK
    n_in = w_in.shape[0]
    const2 = lambda i: (0, 0)
    return pl.pallas_call(
        _dsa_prep_kernel,
        grid=(m // tm,),
        in_specs=[
            pl.BlockSpec((tm, d), lambda i: (i, 0)),
            pl.BlockSpec((1, d), const2),
            pl.BlockSpec((n_in, d), const2),
            pl.BlockSpec((1, Q_RANK_C), const2),
            pl.BlockSpec((1, KV_RANK_C), const2),
            pl.BlockSpec(w_uq.shape, const2),
            pl.BlockSpec(w_uk.shape, lambda i: (0, 0, 0)),
            pl.BlockSpec((1, KV_RANK_C), const2),
            pl.BlockSpec(w_iq.shape, const2),
            pl.BlockSpec((1, IDX_DIM), const2),
            pl.BlockSpec((1, IDX_DIM), const2),
        ],
        out_specs=[
            pl.BlockSpec((bpt, N_HEADS_C, C_BLOCK, KV_RANK_C), lambda i: (i, 0, 0, 0)),
            pl.BlockSpec((bpt, N_IDX_HEADS, C_BLOCK, IDX_DIM), lambda i: (i, 0, 0, 0)),
            pl.BlockSpec((tm, N_IDX_HEADS), lambda i: (i, 0)),
            pl.BlockSpec((tm, KV_RANK_C), lambda i: (i, 0)),
            pl.BlockSpec((KV_RANK_C, tm), lambda i: (0, i)),
            pl.BlockSpec((tm, IDX_DIM), lambda i: (i, 0)),
        ],
        out_shape=[
            jax.ShapeDtypeStruct((nblk, N_HEADS_C, C_BLOCK, KV_RANK_C), BF16),
            jax.ShapeDtypeStruct((nblk, N_IDX_HEADS, C_BLOCK, IDX_DIM), BF16),
            jax.ShapeDtypeStruct((m, N_IDX_HEADS), F32),
            jax.ShapeDtypeStruct((m, KV_RANK_C), BF16),
            jax.ShapeDtypeStruct((KV_RANK_C, m), BF16),
            jax.ShapeDtypeStruct((m, IDX_DIM), BF16),
        ],
        compiler_params=_params("parallel"),
        name="dsa_prep",
    )(x, g.reshape(1, d), w_in, q_lat_norm.reshape(1, -1), kv_norm.reshape(1, -1), w_uq, w_uk,
      q_abs_norm.reshape(1, -1), w_iq, idx_k_norm.reshape(1, -1), idx_k_bias.reshape(1, -1))


def _sortable_key(s):
    bits = pltpu.bitcast(s, jnp.int32)
    return bits ^ ((bits >> 31) & 0x7FFFFFFF)


def _dsa_attn_kernel(qabs_ref, qidx_ref, widx_ref, ckv_ref, ckvt_ref, kidx_ref, wuv_ref, qan_ref,
                     kvn_ref, o_ref, key_ref, m_ref, acc_ref, *, topk, idx_bits):
    n = pl.program_id(1)
    tk = KEY_TILE_C
    n_tiles = (n * C_BLOCK + C_BLOCK + tk - 1) // tk
    q_row = n * C_BLOCK + lax.broadcasted_iota(jnp.int32, (C_BLOCK, tk), 0)
    k_col = lax.broadcasted_iota(jnp.int32, (C_BLOCK, tk), 1)
    k_row = lax.broadcasted_iota(jnp.int32, (tk, C_BLOCK), 0)
    widx = widx_ref[0, 0]

    def tile(t):
        return pl.ds(pl.multiple_of(t * tk, tk), tk)

    def score_tile(t, carry):
        kt = kidx_ref[0, tile(t), :]
        score = jnp.zeros((C_BLOCK, tk), F32)
        for h in range(N_IDX_HEADS):
            rel = jnp.maximum(_nt_dot(qidx_ref[0, h], kt), 0.0)
            score = score + rel * widx[:, h:h + 1]
        score = jnp.where(t * tk + k_col <= q_row, score, NEG_INF)
        key_ref[tile(t), :] = _sortable_key(score.T)
        return carry

    lax.fori_loop(0, n_tiles, score_tile, 0)

    def count(pred):
        def body(t, cnt):
            hit = pred(key_ref[tile(t), :], t).astype(jnp.int32)
            return cnt + jnp.sum(hit.reshape(tk // 8, 8, C_BLOCK), axis=0)
        cnt = lax.fori_loop(0, n_tiles, body, jnp.zeros((8, C_BLOCK), jnp.int32))
        return jnp.sum(cnt, axis=0, keepdims=True)

    def bisect(i, carry):
        lo, n_lo = carry
        cand = lo + lax.shift_left(jnp.int32(1), 31 - i)
        n_cand = count(lambda k, t: k >= cand)
        take = n_cand >= topk
        return jnp.where(take, cand, lo), jnp.where(take, n_cand, n_lo)

    thresh, n_ge = lax.fori_loop(
        0, 32, bisect, (jnp.full((1, C_BLOCK), INT_MIN, jnp.int32),
                        jnp.zeros((1, C_BLOCK), jnp.int32) + n_tiles * tk))

    excess = jnp.max(jnp.where((n_ge > topk) & (thresh > NEG_KEY), 1, 0))

    @pl.when(excess > 0)
    def _():
        want = topk - count(lambda k, t: k > thresh)

        def index_bisect(i, j0):
            cand = j0 + lax.shift_left(jnp.int32(1), idx_bits - 1 - i)
            before = count(lambda k, t: (k == thresh) & (t * tk + k_row < cand))
            return jnp.where(before < want, cand, j0)

        j0 = lax.fori_loop(0, idx_bits, index_bisect, jnp.zeros((1, C_BLOCK), jnp.int32))

        def demote(t, carry):
            keys = key_ref[tile(t), :]
            drop = (keys == thresh) & (t * tk + k_row > j0)
            key_ref[tile(t), :] = jnp.where(drop, thresh - 1, keys)
            return carry

        lax.fori_loop(0, n_tiles, demote, 0)

    thr = jnp.maximum(thresh, NEG_KEY + 1)
    thr2 = jnp.concatenate([thr, thr], axis=1)
    acc_ref[...] = jnp.zeros_like(acc_ref)
    ones_rows = jnp.ones((ONES_ROWS, tk), BF16)
    n_pairs = N_HEADS_C // 2

    def pair_q(p):
        return qabs_ref[0, 2 * p:2 * p + 2].reshape(2 * C_BLOCK, KV_RANK_C)

    bound = (KV_RANK_C * BOUND_SLACK_C * (KV_RANK_C ** -0.5 * LOG2E)
             * jnp.max(jnp.abs(qan_ref[...])) * jnp.max(jnp.abs(kvn_ref[...])))
    small_logits = bound <= MAX_FIXED_OFFSET_C

    def attn_loop(accumulate):
        def attn_tile(t, carry):
            kv = ckv_ref[0, tile(t), :]
            kvt = jnp.concatenate([ckvt_ref[:, tile(t)], ones_rows], axis=0)
            keys = key_ref[tile(t), :]
            sel = jnp.concatenate([keys, keys], axis=1) >= thr2

            def logits(p):
                return jnp.where(sel, _nt_dot(kv, pair_q(p)), NEG_INF)

            pending = [logits(p) for p in range(ATTN_AHEAD_C)]
            for p in range(n_pairs):
                if p + ATTN_AHEAD_C < n_pairs:
                    pending.append(logits(p + ATTN_AHEAD_C))
                accumulate(p, pending.pop(0), kvt)
            return carry

        lax.fori_loop(0, n_tiles, attn_tile, 0)

    def accumulate_fixed(p, s, kvt):
        acc_ref[p] += _dot(kvt, jnp.exp2(s - bound).astype(BF16))

    def accumulate_online(p, s, kvt):
        m_old = m_ref[p]
        m_new = jnp.maximum(m_old, jnp.max(s, axis=0, keepdims=True))
        pe = jnp.exp2(s - m_new)
        acc_ref[p] = jnp.exp2(m_old - m_new) * acc_ref[p] + _dot(kvt, pe.astype(BF16))
        m_ref[p] = m_new

    @pl.when(small_logits)
    def _():
        attn_loop(accumulate_fixed)

    @pl.when(jnp.logical_not(small_logits))
    def _():
        m_ref[...] = jnp.full(m_ref.shape, NEG_INF, F32)
        attn_loop(accumulate_online)

    for p in range(N_HEADS_C // 2):
        denom = acc_ref[p, KV_RANK_C:KV_RANK_C + 1, :]
        o_lat_t = (acc_ref[p, :KV_RANK_C, :] / denom).astype(BF16)
        for e in range(2):
            h = 2 * p + e
            o_ref[0, :, h * V_HEAD_C:(h + 1) * V_HEAD_C] = _tn_dot(
                o_lat_t[:, e * C_BLOCK:(e + 1) * C_BLOCK], wuv_ref[h]).astype(o_ref.dtype)


def dsa_attention(qabs, qidx, widx, ckv, ckvt, kidx, w_uv, q_abs_norm, kv_norm, *, batch, topk):
    nblk = qabs.shape[0]
    nb = nblk // batch
    l = nb * C_BLOCK
    assert l % KEY_TILE_C == 0 and l >= topk
    ckv = ckv.reshape(batch, l, KV_RANK_C)
    kidx = kidx.reshape(batch, l, IDX_DIM)
    widx = widx.reshape(batch, nb, C_BLOCK, N_IDX_HEADS)
    return pl.pallas_call(
        functools.partial(_dsa_attn_kernel, topk=topk, idx_bits=(l - 1).bit_length()),
        grid=(batch, nb),
        in_specs=[
            pl.BlockSpec((1, N_HEADS_C, C_BLOCK, KV_RANK_C), lambda b, j: (b * nb + j, 0, 0, 0)),
            pl.BlockSpec((1, N_IDX_HEADS, C_BLOCK, IDX_DIM), lambda b, j: (b * nb + j, 0, 0, 0)),
            pl.BlockSpec((1, 1, C_BLOCK, N_IDX_HEADS), lambda b, j: (b, j, 0, 0)),
            pl.BlockSpec((1, l, KV_RANK_C), lambda b, j: (b, 0, 0)),
            pl.BlockSpec((KV_RANK_C, l), lambda b, j: (0, b)),
            pl.BlockSpec((1, l, IDX_DIM), lambda b, j: (b, 0, 0)),
            pl.BlockSpec(w_uv.shape, lambda b, j: (0, 0, 0)),
            pl.BlockSpec((1, KV_RANK_C), lambda b, j: (0, 0)),
            pl.BlockSpec((1, KV_RANK_C), lambda b, j: (0, 0)),
        ],
        out_specs=pl.BlockSpec((1, C_BLOCK, N_HEADS_C * V_HEAD_C), lambda b, j: (b, j, 0)),
        out_shape=jax.ShapeDtypeStruct((batch, l, N_HEADS_C * V_HEAD_C), BF16),
        scratch_shapes=[
            pltpu.VMEM((l, C_BLOCK), jnp.int32),
            pltpu.VMEM((N_HEADS_C // 2, 1, 2 * C_BLOCK), F32),
            pltpu.VMEM((N_HEADS_C // 2, KV_RANK_C + ONES_ROWS, 2 * C_BLOCK), F32),
        ],
        compiler_params=_params("parallel", "arbitrary"),
        name="dsa_attention",
    )(qabs, qidx, widx, ckv, ckvt, kidx, w_uv, q_abs_norm.reshape(1, -1), kv_norm.reshape(1, -1))


TM = 512
TF = 512


def _swa_layer(h, batch, norm_mix, w_in, q_norm, k_norm, sinks, wo):
    m, d = h.shape
    qkv = norm_matmul(h, norm_mix, w_in.astype(BF16), tm=TM, tn=w_in.shape[1], out_dtype=BF16)
    o = swa_attention(qkv.reshape(batch, m // batch, -1), q_norm, k_norm, sinks)
    return matmul_res(o.reshape(m, -1), wo.astype(BF16), h, tm=TM, tn=d)


def _gla_layer(h, batch, norm_mix, w_in, w_gate_b, gate_bias, o_norm, wo):
    m, d = h.shape
    dk_all = w_gate_b.shape[1]
    dv_all = wo.shape[0]
    n_main = 2 * dk_all + 2 * dv_all
    wgb_pad = jnp.pad(w_gate_b.astype(BF16), ((0, LANES - GATE_RANK), (0, 0)))
    proj, g_low = norm_matmul_gate(h, norm_mix, w_in, n_main, tm=2 * TM, tn=n_main // 8)
    o = gla_attention(proj.reshape(batch, m // batch, -1), g_low.reshape(batch, m // batch, -1),
                      wgb_pad, gate_bias, o_norm,
                      dk=dk_all // N_HEADS_B, dv=dv_all // N_HEADS_B, tb=256)
    return matmul_res(o.reshape(m, -1), wo.astype(BF16), h, tm=TM, tn=d)


def _dsa_layer(h, batch, norm_mix, w_in, q_lat_norm, kv_norm, w_uq, w_uk, q_abs_norm, w_iq,
               idx_k_norm, idx_k_bias, w_uv, wo):
    m, d = h.shape
    l = m // batch
    qabs, qidx, widx, ckv, ckvt, kidx = dsa_prep(
        h, norm_mix, w_in.T.astype(BF16), q_lat_norm, kv_norm, w_uq.astype(BF16), w_uk.astype(BF16),
        q_abs_norm, w_iq.astype(BF16), idx_k_norm, idx_k_bias, tm=TM)
    o = dsa_attention(qabs, qidx, widx, ckv, ckvt, kidx, w_uv.astype(BF16), q_abs_norm, kv_norm, batch=batch,
                      topk=min(TOPK_MAX, l // 4))
    return matmul_res(o.reshape(m, -1), wo.astype(BF16), h, tm=TM, tn=d)


def kernel(x, l0_norm_mix, l0_w_in, l0_q_norm, l0_k_norm, l0_sinks, l0_wo, l0_norm_ffn, l0_w_gate_up, l0_w_down, l1_norm_mix, l1_w_in, l1_w_gate_b, l1_gate_bias, l1_o_norm, l1_wo, l1_norm_ffn, l1_w_gate_up, l1_w_down, l2_norm_mix, l2_w_in, l2_q_lat_norm, l2_kv_norm, l2_w_uq, l2_w_uk, l2_q_abs_norm, l2_w_iq, l2_idx_k_norm, l2_idx_k_bias, l2_w_uv, l2_wo, l2_norm_ffn, l2_w_gate_up, l2_w_down, l3_norm_mix, l3_w_in, l3_q_norm, l3_k_norm, l3_sinks, l3_wo, l3_norm_ffn, l3_w_gate_up, l3_w_down):
    batch, seq, d = x.shape
    h = x.reshape(batch * seq, d)

    def channel_mix(h, norm_ffn, w_gate_up, w_down):
        return ffn(h, norm_ffn, w_gate_up, w_down, tm=2 * TM, tf=TF // 2)

    h = _swa_layer(h, batch, l0_norm_mix, l0_w_in, l0_q_norm, l0_k_norm, l0_sinks, l0_wo)
    h = channel_mix(h, l0_norm_ffn, l0_w_gate_up, l0_w_down)
    h = _gla_layer(h, batch, l1_norm_mix, l1_w_in, l1_w_gate_b, l1_gate_bias, l1_o_norm, l1_wo)
    h = channel_mix(h, l1_norm_ffn, l1_w_gate_up, l1_w_down)
    h = _dsa_layer(h, batch, l2_norm_mix, l2_w_in, l2_q_lat_norm, l2_kv_norm, l2_w_uq, l2_w_uk,
                   l2_q_abs_norm, l2_w_iq, l2_idx_k_norm, l2_idx_k_bias, l2_w_uv, l2_wo)
    h = channel_mix(h, l2_norm_ffn, l2_w_gate_up, l2_w_down)
    h = _swa_layer(h, batch, l3_norm_mix, l3_w_in, l3_q_norm, l3_k_norm, l3_sinks, l3_wo)
    h = channel_mix(h, l3_norm_ffn, l3_w_gate_up, l3_w_down)
    return h.reshape(batch, seq, d)
```

```python
import functools

import jax
import jax.numpy as jnp
import numpy as np
from jax import lax
from jax.experimental import pallas as pl
from jax.experimental.pallas import tpu as pltpu

F32 = jnp.float32
BF16 = jnp.bfloat16
EPS = 1e-6
NEG_INF = -1e30
LANES = 128

HD_A = 64
N_KV_A = 4
GROUP_A = 8
BLOCK_A = 128
N_HEADS_B = 4
GATE_RANK = 16
GATE_TAU = 16.0
CHUNK = 64
N_HEADS_C = 16
QK_HEAD_C = 128
V_HEAD_C = 128
Q_RANK_C = 512
KV_RANK_C = 256
N_IDX_HEADS = 16
IDX_DIM = 64
TOPK_MAX = 256
C_BLOCK = 128
KEY_TILE_C = 512
ATTN_AHEAD_C = 2
ONES_ROWS = 16
BOUND_SLACK_C = 1.01
MAX_FIXED_OFFSET_C = 56.0
INT_MIN = -2 ** 31
LOG2E = 1.4426950408889634
_NEG_BITS = int(np.float32(NEG_INF).view(np.int32))
NEG_KEY = _NEG_BITS ^ ((_NEG_BITS >> 31) & 0x7FFFFFFF)


def _nt_dot(a, b):
    return lax.dot_general(a, b, (((1,), (1,)), ((), ())), preferred_element_type=F32)


def _tn_dot(a, b):
    return lax.dot_general(a, b, (((0,), (0,)), ((), ())), preferred_element_type=F32)


def _dot(a, b):
    return jnp.dot(a, b, preferred_element_type=F32)


def _rms(x, g):
    return x * lax.rsqrt(jnp.mean(x * x, axis=-1, keepdims=True) + EPS) * g


def _params(*sem, vmem_bytes=None):
    return pltpu.CompilerParams(dimension_semantics=sem, vmem_limit_bytes=vmem_bytes)


def _norm_matmul_kernel(x_ref, g_ref, w_ref, o_ref, xn_ref):
    @pl.when(pl.program_id(1) == 0)
    def _():
        xn_ref[...] = _rms(x_ref[...], g_ref[...]).astype(BF16)

    o_ref[...] = _dot(xn_ref[...], w_ref[...].astype(BF16)).astype(o_ref.dtype)


def norm_matmul(x, g, w, *, tm, tn, out_dtype=F32):
    m, d = x.shape
    n = w.shape[1]
    assert m % tm == 0 and n % tn == 0
    return pl.pallas_call(
        _norm_matmul_kernel,
        grid=(m // tm, n // tn),
        in_specs=[
            pl.BlockSpec((tm, d), lambda i, j: (i, 0)),
            pl.BlockSpec((1, d), lambda i, j: (0, 0)),
            pl.BlockSpec((d, tn), lambda i, j: (0, j)),
        ],
        out_specs=pl.BlockSpec((tm, tn), lambda i, j: (i, j)),
        out_shape=jax.ShapeDtypeStruct((m, n), out_dtype),
        scratch_shapes=[pltpu.VMEM((tm, d), BF16)],
        compiler_params=_params("parallel", "arbitrary"),
        name="norm_matmul",
    )(x, g.reshape(1, d), w)


def _norm_matmul_gate_kernel(x_ref, g_ref, w_ref, wg_ref, o_ref, gl_ref, xn_ref):
    @pl.when(pl.program_id(1) == 0)
    def _():
        xn = _rms(x_ref[...], g_ref[...]).astype(BF16)
        xn_ref[...] = xn
        gl_ref[...] = _nt_dot(xn, wg_ref[...])

    o_ref[...] = _nt_dot(xn_ref[...], w_ref[...].astype(BF16)).astype(o_ref.dtype)


def norm_matmul_gate(x, g, w, n_main, *, tm, tn):
    m, d = x.shape
    assert m % tm == 0 and n_main % tn == 0
    w_t = w.T
    w_gate = jnp.pad(w_t[n_main:].astype(BF16), ((0, LANES - (w.shape[1] - n_main)), (0, 0)))
    vmem_bytes = (2 * tm * d * 4 + tm * d * 2 + 2 * d * tn * 4 + d * tn * 2
                  + 2 * tm * tn * 2 + tm * tn * 4 + 2 * tm * LANES * 4 + 2 * d * LANES * 2)
    return pl.pallas_call(
        _norm_matmul_gate_kernel,
        grid=(m // tm, n_main // tn),
        in_specs=[
            pl.BlockSpec((tm, d), lambda i, j: (i, 0)),
            pl.BlockSpec((1, d), lambda i, j: (0, 0)),
            pl.BlockSpec((tn, d), lambda i, j: (j, 0)),
            pl.BlockSpec((LANES, d), lambda i, j: (0, 0)),
        ],
        out_specs=[
            pl.BlockSpec((tm, tn), lambda i, j: (i, j)),
            pl.BlockSpec((tm, LANES), lambda i, j: (i, 0)),
        ],
        out_shape=[
            jax.ShapeDtypeStruct((m, n_main), BF16),
            jax.ShapeDtypeStruct((m, LANES), F32),
        ],
        scratch_shapes=[pltpu.VMEM((tm, d), BF16)],
        compiler_params=_params("parallel", "arbitrary", vmem_bytes=vmem_bytes),
        name="norm_matmul_gate",
    )(x, g.reshape(1, d), w_t, w_gate)


def _matmul_res_kernel(a_ref, w_ref, r_ref, o_ref):
    o_ref[...] = r_ref[...] + _dot(a_ref[...], w_ref[...].astype(BF16))


def matmul_res(a, w, r, *, tm, tn):
    m, k = a.shape
    n = w.shape[1]
    assert m % tm == 0 and n % tn == 0
    return pl.pallas_call(
        _matmul_res_kernel,
        grid=(m // tm, n // tn),
        in_specs=[
            pl.BlockSpec((tm, k), lambda i, j: (i, 0)),
            pl.BlockSpec((k, tn), lambda i, j: (0, j)),
            pl.BlockSpec((tm, tn), lambda i, j: (i, j)),
        ],
        out_specs=pl.BlockSpec((tm, tn), lambda i, j: (i, j)),
        out_shape=jax.ShapeDtypeStruct((m, n), F32),
        compiler_params=_params("parallel", "arbitrary"),
        name="matmul_res",
    )(a, w, r)


def _ffn_kernel(x_ref, g_ref, wg_ref, wu_ref, wd_ref, o_ref, xn_ref, *, row_chunks):
    def weights():
        return wg_ref[...].astype(BF16), wu_ref[...].astype(BF16), wd_ref[...].astype(BF16)

    def swiglu_tile(xn, wg, wu, wd):
        gate = _dot(xn, wg)
        up = _dot(xn, wu)
        act = (gate * (1.0 / (1.0 + jnp.exp(-gate))) * up).astype(BF16)
        return _dot(act, wd)

    @pl.when(pl.program_id(1) == 0)
    def _():
        w = weights()
        rows = x_ref.shape[0] // row_chunks
        for c in range(row_chunks):
            rs = slice(c * rows, (c + 1) * rows)
            x = x_ref[rs, :]
            xn = _rms(x, g_ref[...]).astype(BF16)
            xn_ref[rs, :] = xn
            o_ref[rs, :] = x + swiglu_tile(xn, *w)

    @pl.when(pl.program_id(1) != 0)
    def _():
        o_ref[...] += swiglu_tile(xn_ref[...], *weights())


def ffn(x, g, w_gate_up, w_down, *, tm, tf):
    m, d = x.shape
    ff = w_down.shape[0]
    nf = ff // tf
    assert m % tm == 0 and ff % tf == 0
    w_bytes = w_down.dtype.itemsize
    vmem_bytes = (4 * tm * d * 4 + tm * d * 2 + 6 * d * tf * w_bytes
                  + 3 * tm * tf * 4 + 3 * d * tf * 2)
    return pl.pallas_call(
        functools.partial(_ffn_kernel, row_chunks=max(1, tm // FFN_NORM_ROWS)),
        grid=(m // tm, nf),
        in_specs=[
            pl.BlockSpec((tm, d), lambda i, j: (i, 0)),
            pl.BlockSpec((1, d), lambda i, j: (0, 0)),
            pl.BlockSpec((d, tf), lambda i, j: (0, j)),
            pl.BlockSpec((d, tf), lambda i, j: (0, j + nf)),
            pl.BlockSpec((tf, d), lambda i, j: (j, 0)),
        ],
        out_specs=pl.BlockSpec((tm, d), lambda i, j: (i, 0)),
        out_shape=jax.ShapeDtypeStruct((m, d), F32),
        scratch_shapes=[pltpu.VMEM((tm, d), BF16)],
        compiler_params=_params("parallel", "arbitrary", vmem_bytes=vmem_bytes),
        name="ffn",
    )(x, g.reshape(1, d), w_gate_up, w_gate_up, w_down)


def _swa_kernel(sink_ref, cur_ref, prev_ref, qn_ref, kn_ref, sel_ref, o_ref):
    n = pl.program_id(1)
    nq = N_KV_A * GROUP_A * HD_A
    nkv = N_KV_A * HD_A
    kj = lax.broadcasted_iota(jnp.int32, (2 * BLOCK_A, 2 * BLOCK_A), 0)
    qi = lax.broadcasted_iota(jnp.int32, (2 * BLOCK_A, 2 * BLOCK_A), 1) & (BLOCK_A - 1)
    mask2 = (kj > qi) & (kj <= qi + BLOCK_A) & ((n > 0) | (kj >= BLOCK_A))
    first_half = lax.broadcasted_iota(jnp.int32, (1, 2 * BLOCK_A), 1) < BLOCK_A
    q_bf = cur_ref[0, :, :nq]
    xq = q_bf.astype(F32)
    sq = xq * xq
    sq_hi = sq.astype(BF16)
    sq_lo = (sq - sq_hi.astype(F32)).astype(BF16)
    ssq = _dot(sq_hi, sel_ref[...]) + _dot(sq_lo, sel_ref[...])
    rq_t = (lax.rsqrt(ssq * (1.0 / HD_A) + EPS) * LOG2E).T
    kscale = qn_ref[...] * kn_ref[...] * HD_A ** -0.5
    zeros = jnp.zeros((2 * BLOCK_A, HD_A), F32)
    v_all = jnp.concatenate([prev_ref[0, :, nkv:2 * nkv], cur_ref[0, :, nq + nkv:nq + 2 * nkv]], axis=0)
    v_t = v_all.astype(F32).T.astype(BF16)
    ones_rows = jnp.ones((ONES_ROWS, 2 * BLOCK_A), BF16)
    v_ones = [jnp.concatenate([v_t[h * HD_A:(h + 1) * HD_A, :], ones_rows], axis=0)
              for h in range(N_KV_A)]
    k_pads = []
    for h in range(N_KV_A):
        k = jnp.concatenate([prev_ref[0, :, h * HD_A:(h + 1) * HD_A],
                             cur_ref[0, :, nq + h * HD_A:nq + (h + 1) * HD_A]], axis=0).astype(F32)
        kn = k * lax.rsqrt(jnp.mean(k * k, axis=-1, keepdims=True) + EPS) * kscale
        k_pads.append((jnp.concatenate([kn, zeros], axis=1).astype(BF16),
                       jnp.concatenate([zeros, kn], axis=1).astype(BF16)))

    def logits(i):
        j = 2 * i
        k_pad = k_pads[j // GROUP_A]
        q_tile = q_bf[:, j * HD_A:(j + 2) * HD_A]
        rq = jnp.concatenate([rq_t[j:j + 1, :], rq_t[j + 1:j + 2, :]], axis=1)
        s = jnp.concatenate([_nt_dot(k_pad[0], q_tile), _nt_dot(k_pad[1], q_tile)], axis=1) * rq
        return jnp.where(mask2, s, NEG_INF)

    def attend(i, s):
        j = 2 * i
        sink = jnp.where(first_half, sink_ref[j], sink_ref[j + 1]) * LOG2E
        mx = jnp.maximum(jnp.max(s, axis=0, keepdims=True), sink)
        p = jnp.exp2(s - mx)
        return _dot(v_ones[j // GROUP_A], p.astype(BF16)), jnp.exp2(sink - mx)

    def finish(i, pv, sink_term):
        o_t = pv[:HD_A] / (pv[HD_A:HD_A + 1] + sink_term)
        o_pair = jnp.concatenate([o_t[:, :BLOCK_A], o_t[:, BLOCK_A:]], axis=0)
        o_ref[0, :, 2 * i * HD_A:(2 * i + 2) * HD_A] = o_pair.T.astype(o_ref.dtype)

    n_pairs = N_KV_A * GROUP_A // 2
    ahead = 3
    pending = [logits(i) for i in range(ahead)]
    unfinished = None
    for i in range(n_pairs):
        if i + ahead < n_pairs:
            pending.append(logits(i + ahead))
        out = attend(i, pending.pop(0))
        if unfinished is not None:
            finish(i - 1, *unfinished)
        unfinished = out
    finish(n_pairs - 1, *unfinished)


def swa_attention(qkv, q_norm, k_norm, sinks):
    b, l, n = qkv.shape
    nq = N_KV_A * GROUP_A * HD_A
    kv_w = 2 * N_KV_A * HD_A
    head_sel = (jnp.arange(nq)[:, None] // HD_A == jnp.arange(LANES)[None, :]).astype(BF16)
    return pl.pallas_call(
        _swa_kernel,
        grid=(b, l // BLOCK_A),
        in_specs=[
            pl.BlockSpec(memory_space=pltpu.SMEM),
            pl.BlockSpec((1, BLOCK_A, n), lambda i, j: (i, j, 0)),
            pl.BlockSpec((1, BLOCK_A, kv_w), lambda i, j: (i, jnp.maximum(j - 1, 0), nq // kv_w)),
            pl.BlockSpec((1, HD_A), lambda i, j: (0, 0)),
            pl.BlockSpec((1, HD_A), lambda i, j: (0, 0)),
            pl.BlockSpec((nq, LANES), lambda i, j: (0, 0)),
        ],
        out_specs=pl.BlockSpec((1, BLOCK_A, nq), lambda i, j: (i, j, 0)),
        out_shape=jax.ShapeDtypeStruct((b, l, nq), BF16),
        compiler_params=_params("parallel", "parallel"),
        name="swa_attention",
    )(sinks, qkv, qkv, q_norm.reshape(1, HD_A), k_norm.reshape(1, HD_A), head_sel)


def _split3(x):
    hi = x.astype(BF16)
    r1 = x - hi.astype(F32)
    mid = r1.astype(BF16)
    lo = (r1 - mid.astype(F32)).astype(BF16)
    return hi, mid, lo


def _gla_kernel(q_ref, k_ref, v_ref, r_ref, gl_ref, wgb_ref, gb_ref, on_ref, o_ref, st_ref,
                *, tb, dk, dv):
    @pl.when(pl.program_id(1) == 0)
    def _():
        st_ref[...] = jnp.zeros_like(st_ref)

    ti = lax.broadcasted_iota(jnp.int32, (CHUNK, CHUNK), 0)
    si = lax.broadcasted_iota(jnp.int32, (CHUNK, CHUNK), 1)
    causal = si <= ti
    tri = causal.astype(BF16)
    wgb = wgb_ref[...]
    gb = gb_ref[...]
    on = on_ref[...]

    def decays(c):
        rs = pl.ds(c * CHUNK, CHUNK)
        gate = _dot(gl_ref[0, rs, :].astype(BF16), wgb) + gb
        log_a = (jnp.minimum(gate, 0.0) - jnp.log1p(jnp.exp(-jnp.abs(gate)))) / GATE_TAU
        hi, mid, lo = _split3(log_a)
        bcum = _dot(tri, hi) + _dot(tri, mid) + _dot(tri, lo)
        b_last = bcum[CHUNK - 1:CHUNK, :]
        q = q_ref[0, rs, :].astype(F32)
        k = k_ref[0, rs, :].astype(F32)
        q_dec = (q * dk ** -0.5 * jnp.exp(bcum)).astype(BF16)
        k_inv = (k * jnp.exp(-bcum)).astype(BF16)
        k_end = (k * jnp.exp(b_last - bcum)).astype(BF16)
        return q_dec, k_inv, k_end, jnp.exp(b_last)

    def recur(c, q_dec, k_inv, k_end, decay):
        rs = pl.ds(c * CHUNK, CHUNK)
        for h in range(N_HEADS_B):
            ks = slice(h * dk, (h + 1) * dk)
            vs = slice(h * dv, (h + 1) * dv)
            v = v_ref[0, rs, vs].astype(BF16)
            a = jnp.where(causal, _nt_dot(q_dec[:, ks], k_inv[:, ks]), 0.0)
            st = st_ref[h]
            o = _dot(a.astype(BF16), v) + _nt_dot(q_dec[:, ks], st.astype(BF16))
            st_ref[h] = st * decay[:, ks] + _tn_dot(v, k_end[:, ks])
            o = _rms(o, on)
            r = r_ref[0, rs, vs].astype(F32)
            o_ref[0, rs, vs] = (o * (r * (1.0 / (1.0 + jnp.exp(-r))))).astype(o_ref.dtype)

    n_chunks = tb // CHUNK
    ready = decays(0)
    for c in range(n_chunks):
        cur = ready
        if c + 1 < n_chunks:
            ready = decays(c + 1)
        recur(c, *cur)


def gla_attention(proj, g_low, w_gate_b, gate_bias, o_norm, *, dk, dv, tb):
    b, l, _ = proj.shape
    dk_all, dv_all = N_HEADS_B * dk, N_HEADS_B * dv
    return pl.pallas_call(
        functools.partial(_gla_kernel, tb=tb, dk=dk, dv=dv),
        grid=(b, l // tb),
        in_specs=[
            pl.BlockSpec((1, tb, dk_all), lambda i, j: (i, j, 0)),
            pl.BlockSpec((1, tb, dk_all), lambda i, j: (i, j, 1)),
            pl.BlockSpec((1, tb, dv_all), lambda i, j: (i, j, 2 * dk_all // dv_all)),
            pl.BlockSpec((1, tb, dv_all), lambda i, j: (i, j, 2 * dk_all // dv_all + 1)),
            pl.BlockSpec((1, tb, LANES), lambda i, j: (i, j, 0)),
            pl.BlockSpec((LANES, dk_all), lambda i, j: (0, 0)),
            pl.BlockSpec((1, dk_all), lambda i, j: (0, 0)),
            pl.BlockSpec((1, dv), lambda i, j: (0, 0)),
        ],
        out_specs=pl.BlockSpec((1, tb, dv_all), lambda i, j: (i, j, 0)),
        out_shape=jax.ShapeDtypeStruct((b, l, dv_all), BF16),
        scratch_shapes=[pltpu.VMEM((N_HEADS_B, dv, dk), F32)],
        compiler_params=_params("parallel", "arbitrary"),
        name="gla_attention",
    )(proj, proj, proj, proj, g_low, w_gate_b, gate_bias.reshape(1, dk_all), o_norm.reshape(1, dv))


def _dsa_prep_kernel(x_ref, g_ref, win_ref, qln_ref, kvn_ref, wuq_ref, wuk_ref, qan_ref, wiq_ref,
                     ikn_ref, ikb_ref, qabs_ref, qidx_ref, widx_ref, ckv_ref, ckvt_ref, kidx_ref):
    xn = _rms(x_ref[...], g_ref[...]).astype(BF16)
    proj = _nt_dot(xn, win_ref[...])
    o_kv = Q_RANK_C
    o_ki = Q_RANK_C + KV_RANK_C
    o_wi = o_ki + IDX_DIM
    c_q = _rms(proj[:, :o_kv], qln_ref[...]).astype(BF16)
    c_kv = _rms(proj[:, o_kv:o_ki], kvn_ref[...])
    ckv_ref[...] = c_kv.astype(BF16)
    ckvt_ref[...] = c_kv.T.astype(BF16)
    k_idx = proj[:, o_ki:o_wi]
    mu = jnp.mean(k_idx, axis=-1, keepdims=True)
    kc = k_idx - mu
    k_idx = kc * lax.rsqrt(jnp.mean(kc * kc, axis=-1, keepdims=True) + EPS)
    kidx_ref[...] = (k_idx * ikn_ref[...] + ikb_ref[...]).astype(BF16)
    widx_ref[...] = proj[:, o_wi:o_wi + N_IDX_HEADS] * (N_IDX_HEADS ** -0.5 * IDX_DIM ** -0.5)
    q_nope = _dot(c_q, wuq_ref[...])
    q_idx = _dot(c_q, wiq_ref[...])
    qan = qan_ref[...] * (KV_RANK_C ** -0.5 * LOG2E)
    for h in range(N_HEADS_C):
        qh = q_nope[:, h * QK_HEAD_C:(h + 1) * QK_HEAD_C].astype(BF16)
        qa = _rms(_dot(qh, wuk_ref[h]), qan).astype(BF16)
        qi = q_idx[:, h * IDX_DIM:(h + 1) * IDX_DIM].astype(BF16)
        for j in range(qabs_ref.shape[0]):
            qabs_ref[j, h] = qa[j * C_BLOCK:(j + 1) * C_BLOCK]
            qidx_ref[j, h] = qi[j * C_BLOCK:(j + 1) * C_BLOCK]


def dsa_prep(x, g, w_in, q_lat_norm, kv_norm, w_uq, w_uk, q_abs_norm, w_iq, idx_k_norm, idx_k_bias,
             *, tm):
    m, d = x.shape
    nblk = m // C_BLOCK
    bpt = tm // C_BLOCK
    n_in = w_in.shape[0]
    const2 = lambda i: (0, 0)
    return pl.pallas_call(
        _dsa_prep_kernel,
        grid=(m // tm,),
        in_specs=[
            pl.BlockSpec((tm, d), lambda i: (i, 0)),
            pl.BlockSpec((1, d), const2),
            pl.BlockSpec((n_in, d), const2),
            pl.BlockSpec((1, Q_RANK_C), const2),
            pl.BlockSpec((1, KV_RANK_C), const2),
            pl.BlockSpec(w_uq.shape, const2),
            pl.BlockSpec(w_uk.shape, lambda i: (0, 0, 0)),
            pl.BlockSpec((1, KV_RANK_C), const2),
            pl.BlockSpec(w_iq.shape, const2),
            pl.BlockSpec((1, IDX_DIM), const2),
            pl.BlockSpec((1, IDX_DIM), const2),
        ],
        out_specs=[
            pl.BlockSpec((bpt, N_HEADS_C, C_BLOCK, KV_RANK_C), lambda i: (i, 0, 0, 0)),
            pl.BlockSpec((bpt, N_IDX_HEADS, C_BLOCK, IDX_DIM), lambda i: (i, 0, 0, 0)),
            pl.BlockSpec((tm, N_IDX_HEADS), lambda i: (i, 0)),
            pl.BlockSpec((tm, KV_RANK_C), lambda i: (i, 0)),
            pl.BlockSpec((KV_RANK_C, tm), lambda i: (0, i)),
            pl.BlockSpec((tm, IDX_DIM), lambda i: (i, 0)),
        ],
        out_shape=[
            jax.ShapeDtypeStruct((nblk, N_HEADS_C, C_BLOCK, KV_RANK_C), BF16),
            jax.ShapeDtypeStruct((nblk, N_IDX_HEADS, C_BLOCK, IDX_DIM), BF16),
            jax.ShapeDtypeStruct((m, N_IDX_HEADS), F32),
            jax.ShapeDtypeStruct((m, KV_RANK_C), BF16),
            jax.ShapeDtypeStruct((KV_RANK_C, m), BF16),
            jax.ShapeDtypeStruct((m, IDX_DIM), BF16),
        ],
        compiler_params=_params("parallel"),
        name="dsa_prep",
    )(x, g.reshape(1, d), w_in, q_lat_norm.reshape(1, -1), kv_norm.reshape(1, -1), w_uq, w_uk,
      q_abs_norm.reshape(1, -1), w_iq, idx_k_norm.reshape(1, -1), idx_k_bias.reshape(1, -1))


def _sortable_key(s):
    bits = pltpu.bitcast(s, jnp.int32)
    return bits ^ ((bits >> 31) & 0x7FFFFFFF)


def _dsa_attn_kernel(qabs_ref, qidx_ref, widx_ref, ckv_ref, ckvt_ref, kidx_ref, wuv_ref, qan_ref,
                     kvn_ref, o_ref, key_ref, m_ref, acc_ref, *, topk, idx_bits):
    n = pl.program_id(1)
    tk = KEY_TILE_C
    n_tiles = (n * C_BLOCK + C_BLOCK + tk - 1) // tk
    q_row = n * C_BLOCK + lax.broadcasted_iota(jnp.int32, (C_BLOCK, tk), 0)
    k_col = lax.broadcasted_iota(jnp.int32, (C_BLOCK, tk), 1)
    k_row = lax.broadcasted_iota(jnp.int32, (tk, C_BLOCK), 0)
    widx = widx_ref[0, 0]

    def tile(t):
        return pl.ds(pl.multiple_of(t * tk, tk), tk)

    def score_tile(t, carry):
        kt = kidx_ref[0, tile(t), :]
        score = jnp.zeros((C_BLOCK, tk), F32)
        for h in range(N_IDX_HEADS):
            rel = jnp.maximum(_nt_dot(qidx_ref[0, h], kt), 0.0)
            score = score + rel * widx[:, h:h + 1]
        score = jnp.where(t * tk + k_col <= q_row, score, NEG_INF)
        key_ref[tile(t), :] = _sortable_key(score.T)
        return carry

    lax.fori_loop(0, n_tiles, score_tile, 0)

    def count(pred):
        def body(t, cnt):
            hit = pred(key_ref[tile(t), :], t).astype(jnp.int32)
            return cnt + jnp.sum(hit.reshape(tk // 8, 8, C_BLOCK), axis=0)
        cnt = lax.fori_loop(0, n_tiles, body, jnp.zeros((8, C_BLOCK), jnp.int32))
        return jnp.sum(cnt, axis=0, keepdims=True)

    def bisect(i, carry):
        lo, n_lo = carry
        cand = lo + lax.shift_left(jnp.int32(1), 31 - i)
        n_cand = count(lambda k, t: k >= cand)
        take = n_cand >= topk
        return jnp.where(take, cand, lo), jnp.where(take, n_cand, n_lo)

    thresh, n_ge = lax.fori_loop(
        0, 32, bisect, (jnp.full((1, C_BLOCK), INT_MIN, jnp.int32),
                        jnp.zeros((1, C_BLOCK), jnp.int32) + n_tiles * tk))

    excess = jnp.max(jnp.where((n_ge > topk) & (thresh > NEG_KEY), 1, 0))

    @pl.when(excess > 0)
    def _():
        want = topk - count(lambda k, t: k > thresh)

        def index_bisect(i, j0):
            cand = j0 + lax.shift_left(jnp.int32(1), idx_bits - 1 - i)
            before = count(lambda k, t: (k == thresh) & (t * tk + k_row < cand))
            return jnp.where(before < want, cand, j0)

        j0 = lax.fori_loop(0, idx_bits, index_bisect, jnp.zeros((1, C_BLOCK), jnp.int32))

        def demote(t, carry):
            keys = key_ref[tile(t), :]
            drop = (keys == thresh) & (t * tk + k_row > j0)
            key_ref[tile(t), :] = jnp.where(drop, thresh - 1, keys)
            return carry

        lax.fori_loop(0, n_tiles, demote, 0)

    thr = jnp.maximum(thresh, NEG_KEY + 1)
    thr2 = jnp.concatenate([thr, thr], axis=1)
    acc_ref[...] = jnp.zeros_like(acc_ref)
    ones_rows = jnp.ones((ONES_ROWS, tk), BF16)
    n_pairs = N_HEADS_C // 2

    def pair_q(p):
        return qabs_ref[0, 2 * p:2 * p + 2].reshape(2 * C_BLOCK, KV_RANK_C)

    bound = (KV_RANK_C * BOUND_SLACK_C * (KV_RANK_C ** -0.5 * LOG2E)
             * jnp.max(jnp.abs(qan_ref[...])) * jnp.max(jnp.abs(kvn_ref[...])))
    small_logits = bound <= MAX_FIXED_OFFSET_C

    def attn_loop(accumulate):
        def attn_tile(t, carry):
            kv = ckv_ref[0, tile(t), :]
            kvt = jnp.concatenate([ckvt_ref[:, tile(t)], ones_rows], axis=0)
            keys = key_ref[tile(t), :]
            sel = jnp.concatenate([keys, keys], axis=1) >= thr2

            def logits(p):
                return jnp.where(sel, _nt_dot(kv, pair_q(p)), NEG_INF)

            pending = [logits(p) for p in range(ATTN_AHEAD_C)]
            for p in range(n_pairs):
                if p + ATTN_AHEAD_C < n_pairs:
                    pending.append(logits(p + ATTN_AHEAD_C))
                accumulate(p, pending.pop(0), kvt)
            return carry

        lax.fori_loop(0, n_tiles, attn_tile, 0)

    def accumulate_fixed(p, s, kvt):
        acc_ref[p] += _dot(kvt, jnp.exp2(s - bound).astype(BF16))

    def accumulate_online(p, s, kvt):
        m_old = m_ref[p]
        m_new = jnp.maximum(m_old, jnp.max(s, axis=0, keepdims=True))
        pe = jnp.exp2(s - m_new)
        acc_ref[p] = jnp.exp2(m_old - m_new) * acc_ref[p] + _dot(kvt, pe.astype(BF16))
        m_ref[p] = m_new

    @pl.when(small_logits)
    def _():
        attn_loop(accumulate_fixed)

    @pl.when(jnp.logical_not(small_logits))
    def _():
        m_ref[...] = jnp.full(m_ref.shape, NEG_INF, F32)
        attn_loop(accumulate_online)

    for p in range(N_HEADS_C // 2):
        denom = acc_ref[p, KV_RANK_C:KV_RANK_C + 1, :]
        o_lat_t = (acc_ref[p, :KV_RANK_C, :] / denom).astype(BF16)
        for e in range(2):
            h = 2 * p + e
            o_ref[0, :, h * V_HEAD_C:(h + 1) * V_HEAD_C] = _tn_dot(
                o_lat_t[:, e * C_BLOCK:(e + 1) * C_BLOCK], wuv_ref[h]).astype(o_ref.dtype)


def dsa_attention(qabs, qidx, widx, ckv, ckvt, kidx, w_uv, q_abs_norm, kv_norm, *, batch, topk):
    nblk = qabs.shape[0]
    nb = nblk // batch
    l = nb * C_BLOCK
    assert l % KEY_TILE_C == 0 and l >= topk
    ckv = ckv.reshape(batch, l, KV_RANK_C)
    kidx = kidx.reshape(batch, l, IDX_DIM)
    widx = widx.reshape(batch, nb, C_BLOCK, N_IDX_HEADS)
    return pl.pallas_call(
        functools.partial(_dsa_attn_kernel, topk=topk, idx_bits=(l - 1).bit_length()),
        grid=(batch, nb),
        in_specs=[
            pl.BlockSpec((1, N_HEADS_C, C_BLOCK, KV_RANK_C), lambda b, j: (b * nb + j, 0, 0, 0)),
            pl.BlockSpec((1, N_IDX_HEADS, C_BLOCK, IDX_DIM), lambda b, j: (b * nb + j, 0, 0, 0)),
            pl.BlockSpec((1, 1, C_BLOCK, N_IDX_HEADS), lambda b, j: (b, j, 0, 0)),
            pl.BlockSpec((1, l, KV_RANK_C), lambda b, j: (b, 0, 0)),
            pl.BlockSpec((KV_RANK_C, l), lambda b, j: (0, b)),
            pl.BlockSpec((1, l, IDX_DIM), lambda b, j: (b, 0, 0)),
            pl.BlockSpec(w_uv.shape, lambda b, j: (0, 0, 0)),
            pl.BlockSpec((1, KV_RANK_C), lambda b, j: (0, 0)),
            pl.BlockSpec((1, KV_RANK_C), lambda b, j: (0, 0)),
        ],
        out_specs=pl.BlockSpec((1, C_BLOCK, N_HEADS_C * V_HEAD_C), lambda b, j: (b, j, 0)),
        out_shape=jax.ShapeDtypeStruct((batch, l, N_HEADS_C * V_HEAD_C), BF16),
        scratch_shapes=[
            pltpu.VMEM((l, C_BLOCK), jnp.int32),
            pltpu.VMEM((N_HEADS_C // 2, 1, 2 * C_BLOCK), F32),
            pltpu.VMEM((N_HEADS_C // 2, KV_RANK_C + ONES_ROWS, 2 * C_BLOCK), F32),
        ],
        compiler_params=_params("parallel", "arbitrary"),
        name="dsa_attention",
    )(qabs, qidx, widx, ckv, ckvt, kidx, w_uv, q_abs_norm.reshape(1, -1), kv_norm.reshape(1, -1))


TM = 512
TF = 512
FFN_NORM_ROWS = 512


def _swa_layer(h, batch, norm_mix, w_in, q_norm, k_norm, sinks, wo):
    m, d = h.shape
    qkv = norm_matmul(h, norm_mix, w_in.astype(BF16), tm=TM, tn=w_in.shape[1], out_dtype=BF16)
    o = swa_attention(qkv.reshape(batch, m // batch, -1), q_norm, k_norm, sinks)
    return matmul_res(o.reshape(m, -1), wo.astype(BF16), h, tm=TM, tn=d)


def _gla_layer(h, batch, norm_mix, w_in, w_gate_b, gate_bias, o_norm, wo):
    m, d = h.shape
    dk_all = w_gate_b.shape[1]
    dv_all = wo.shape[0]
    n_main = 2 * dk_all + 2 * dv_all
    wgb_pad = jnp.pad(w_gate_b.astype(BF16), ((0, LANES - GATE_RANK), (0, 0)))
    proj, g_low = norm_matmul_gate(h, norm_mix, w_in, n_main, tm=2 * TM, tn=n_main // 8)
    o = gla_attention(proj.reshape(batch, m // batch, -1), g_low.reshape(batch, m // batch, -1),
                      wgb_pad, gate_bias, o_norm,
                      dk=dk_all // N_HEADS_B, dv=dv_all // N_HEADS_B, tb=256)
    return matmul_res(o.reshape(m, -1), wo.astype(BF16), h, tm=TM, tn=d)


def _dsa_layer(h, batch, norm_mix, w_in, q_lat_norm, kv_norm, w_uq, w_uk, q_abs_norm, w_iq,
               idx_k_norm, idx_k_bias, w_uv, wo):
    m, d = h.shape
    l = m // batch
    qabs, qidx, widx, ckv, ckvt, kidx = dsa_prep(
        h, norm_mix, w_in.T.astype(BF16), q_lat_norm, kv_norm, w_uq.astype(BF16), w_uk.astype(BF16),
        q_abs_norm, w_iq.astype(BF16), idx_k_norm, idx_k_bias, tm=TM)
    o = dsa_attention(qabs, qidx, widx, ckv, ckvt, kidx, w_uv.astype(BF16), q_abs_norm, kv_norm, batch=batch,
                      topk=min(TOPK_MAX, l // 4))
    return matmul_res(o.reshape(m, -1), wo.astype(BF16), h, tm=TM, tn=d)


def kernel(x, l0_norm_mix, l0_w_in, l0_q_norm, l0_k_norm, l0_sinks, l0_wo, l0_norm_ffn, l0_w_gate_up, l0_w_down, l1_norm_mix, l1_w_in, l1_w_gate_b, l1_gate_bias, l1_o_norm, l1_wo, l1_norm_ffn, l1_w_gate_up, l1_w_down, l2_norm_mix, l2_w_in, l2_q_lat_norm, l2_kv_norm, l2_w_uq, l2_w_uk, l2_q_abs_norm, l2_w_iq, l2_idx_k_norm, l2_idx_k_bias, l2_w_uv, l2_wo, l2_norm_ffn, l2_w_gate_up, l2_w_down, l3_norm_mix, l3_w_in, l3_q_norm, l3_k_norm, l3_sinks, l3_wo, l3_norm_ffn, l3_w_gate_up, l3_w_down):
    batch, seq, d = x.shape
    h = x.reshape(batch * seq, d)

    def channel_mix(h, norm_ffn, w_gate_up, w_down):
        return ffn(h, norm_ffn, w_gate_up, w_down, tm=2 * TM, tf=TF // 2)

    h = _swa_layer(h, batch, l0_norm_mix, l0_w_in, l0_q_norm, l0_k_norm, l0_sinks, l0_wo)
    h = channel_mix(h, l0_norm_ffn, l0_w_gate_up, l0_w_down)
    h = _gla_layer(h, batch, l1_norm_mix, l1_w_in, l1_w_gate_b, l1_gate_bias, l1_o_norm, l1_wo)
    h = channel_mix(h, l1_norm_ffn, l1_w_gate_up, l1_w_down)
    h = _dsa_layer(h, batch, l2_norm_mix, l2_w_in, l2_q_lat_norm, l2_kv_norm, l2_w_uq, l2_w_uk,
                   l2_q_abs_norm, l2_w_iq, l2_idx_k_norm, l2_idx_k_bias, l2_w_uv, l2_wo)
    h = channel_mix(h, l2_norm_ffn, l2_w_gate_up, l2_w_down)
    h = _swa_layer(h, batch, l3_norm_mix, l3_w_in, l3_q_norm, l3_k_norm, l3_sinks, l3_wo)
    h = channel_mix(h, l3_norm_ffn, l3_w_gate_up, l3_w_down)
    return h.reshape(batch, seq, d)
```

```python
import functools

import jax
import jax.numpy as jnp
import numpy as np
from jax import lax
from jax.experimental import pallas as pl
from jax.experimental.pallas import tpu as pltpu

F32 = jnp.float32
BF16 = jnp.bfloat16
EPS = 1e-6
NEG_INF = -1e30
LANES = 128

HD_A = 64
N_KV_A = 4
GROUP_A = 8
BLOCK_A = 128
N_HEADS_B = 4
GATE_RANK = 16
GATE_TAU = 16.0
CHUNK = 64
N_HEADS_C = 16
QK_HEAD_C = 128
V_HEAD_C = 128
Q_RANK_C = 512
KV_RANK_C = 256
N_IDX_HEADS = 16
IDX_DIM = 64
TOPK_MAX = 256
C_BLOCK = 128
KEY_TILE_C = 512
ATTN_AHEAD_C = 3
ONES_ROWS = 16
BOUND_SLACK_C = 1.01
MAX_FIXED_OFFSET_C = 56.0
INT_MIN = -2 ** 31
LOG2E = 1.4426950408889634
_NEG_BITS = int(np.float32(NEG_INF).view(np.int32))
NEG_KEY = _NEG_BITS ^ ((_NEG_BITS >> 31) & 0x7FFFFFFF)


def _nt_dot(a, b):
    return lax.dot_general(a, b, (((1,), (1,)), ((), ())), preferred_element_type=F32)


def _tn_dot(a, b):
    return lax.dot_general(a, b, (((0,), (0,)), ((), ())), preferred_element_type=F32)


def _dot(a, b):
    return jnp.dot(a, b, preferred_element_type=F32)


def _rms(x, g):
    return x * lax.rsqrt(jnp.mean(x * x, axis=-1, keepdims=True) + EPS) * g


def _params(*sem, vmem_bytes=None):
    return pltpu.CompilerParams(dimension_semantics=sem, vmem_limit_bytes=vmem_bytes)


def _norm_matmul_kernel(x_ref, g_ref, w_ref, o_ref, xn_ref):
    @pl.when(pl.program_id(1) == 0)
    def _():
        xn_ref[...] = _rms(x_ref[...], g_ref[...]).astype(BF16)

    o_ref[...] = _dot(xn_ref[...], w_ref[...].astype(BF16)).astype(o_ref.dtype)


def norm_matmul(x, g, w, *, tm, tn, out_dtype=F32):
    m, d = x.shape
    n = w.shape[1]
    assert m % tm == 0 and n % tn == 0
    return pl.pallas_call(
        _norm_matmul_kernel,
        grid=(m // tm, n // tn),
        in_specs=[
            pl.BlockSpec((tm, d), lambda i, j: (i, 0)),
            pl.BlockSpec((1, d), lambda i, j: (0, 0)),
            pl.BlockSpec((d, tn), lambda i, j: (0, j)),
        ],
        out_specs=pl.BlockSpec((tm, tn), lambda i, j: (i, j)),
        out_shape=jax.ShapeDtypeStruct((m, n), out_dtype),
        scratch_shapes=[pltpu.VMEM((tm, d), BF16)],
        compiler_params=_params("parallel", "arbitrary"),
        name="norm_matmul",
    )(x, g.reshape(1, d), w)


def _norm_matmul_gate_kernel(x_ref, g_ref, w_ref, wg_ref, o_ref, gl_ref, xn_ref):
    @pl.when(pl.program_id(1) == 0)
    def _():
        xn = _rms(x_ref[...], g_ref[...]).astype(BF16)
        xn_ref[...] = xn
        gl_ref[...] = _nt_dot(xn, wg_ref[...])

    o_ref[...] = _nt_dot(xn_ref[...], w_ref[...].astype(BF16)).astype(o_ref.dtype)


def norm_matmul_gate(x, g, w, n_main, *, tm, tn):
    m, d = x.shape
    assert m % tm == 0 and n_main % tn == 0
    w_t = w.T
    w_gate = jnp.pad(w_t[n_main:].astype(BF16), ((0, LANES - (w.shape[1] - n_main)), (0, 0)))
    vmem_bytes = (2 * tm * d * 4 + tm * d * 2 + 2 * d * tn * 4 + d * tn * 2
                  + 2 * tm * tn * 2 + tm * tn * 4 + 2 * tm * LANES * 4 + 2 * d * LANES * 2)
    return pl.pallas_call(
        _norm_matmul_gate_kernel,
        grid=(m // tm, n_main // tn),
        in_specs=[
            pl.BlockSpec((tm, d), lambda i, j: (i, 0)),
            pl.BlockSpec((1, d), lambda i, j: (0, 0)),
            pl.BlockSpec((tn, d), lambda i, j: (j, 0)),
            pl.BlockSpec((LANES, d), lambda i, j: (0, 0)),
        ],
        out_specs=[
            pl.BlockSpec((tm, tn), lambda i, j: (i, j)),
            pl.BlockSpec((tm, LANES), lambda i, j: (i, 0)),
        ],
        out_shape=[
            jax.ShapeDtypeStruct((m, n_main), BF16),
            jax.ShapeDtypeStruct((m, LANES), F32),
        ],
        scratch_shapes=[pltpu.VMEM((tm, d), BF16)],
        compiler_params=_params("parallel", "arbitrary", vmem_bytes=vmem_bytes),
        name="norm_matmul_gate",
    )(x, g.reshape(1, d), w_t, w_gate)


def _matmul_res_kernel(a_ref, w_ref, r_ref, o_ref):
    o_ref[...] = r_ref[...] + _dot(a_ref[...], w_ref[...].astype(BF16))


def matmul_res(a, w, r, *, tm, tn):
    m, k = a.shape
    n = w.shape[1]
    assert m % tm == 0 and n % tn == 0
    return pl.pallas_call(
        _matmul_res_kernel,
        grid=(m // tm, n // tn),
        in_specs=[
            pl.BlockSpec((tm, k), lambda i, j: (i, 0)),
            pl.BlockSpec((k, tn), lambda i, j: (0, j)),
            pl.BlockSpec((tm, tn), lambda i, j: (i, j)),
        ],
        out_specs=pl.BlockSpec((tm, tn), lambda i, j: (i, j)),
        out_shape=jax.ShapeDtypeStruct((m, n), F32),
        compiler_params=_params("parallel", "arbitrary"),
        name="matmul_res",
    )(a, w, r)


def _ffn_kernel(x_ref, g_ref, wg_ref, wu_ref, wd_ref, o_ref, xn_ref, *, row_chunks):
    def weights():
        return wg_ref[...].astype(BF16), wu_ref[...].astype(BF16), wd_ref[...].astype(BF16)

    def swiglu_tile(xn, wg, wu, wd):
        gate = _dot(xn, wg)
        up = _dot(xn, wu)
        act = (gate * (1.0 / (1.0 + jnp.exp(-gate))) * up).astype(BF16)
        return _dot(act, wd)

    @pl.when(pl.program_id(1) == 0)
    def _():
        w = weights()
        rows = x_ref.shape[0] // row_chunks
        for c in range(row_chunks):
            rs = slice(c * rows, (c + 1) * rows)
            x = x_ref[rs, :]
            xn = _rms(x, g_ref[...]).astype(BF16)
            xn_ref[rs, :] = xn
            o_ref[rs, :] = x + swiglu_tile(xn, *w)

    @pl.when(pl.program_id(1) != 0)
    def _():
        o_ref[...] += swiglu_tile(xn_ref[...], *weights())


def ffn(x, g, w_gate_up, w_down, *, tm, tf):
    m, d = x.shape
    ff = w_down.shape[0]
    nf = ff // tf
    assert m % tm == 0 and ff % tf == 0
    w_bytes = w_down.dtype.itemsize
    vmem_bytes = (4 * tm * d * 4 + tm * d * 2 + 6 * d * tf * w_bytes
                  + 3 * tm * tf * 4 + 3 * d * tf * 2)
    return pl.pallas_call(
        functools.partial(_ffn_kernel, row_chunks=max(1, tm // FFN_NORM_ROWS)),
        grid=(m // tm, nf),
        in_specs=[
            pl.BlockSpec((tm, d), lambda i, j: (i, 0)),
            pl.BlockSpec((1, d), lambda i, j: (0, 0)),
            pl.BlockSpec((d, tf), lambda i, j: (0, j)),
            pl.BlockSpec((d, tf), lambda i, j: (0, j + nf)),
            pl.BlockSpec((tf, d), lambda i, j: (j, 0)),
        ],
        out_specs=pl.BlockSpec((tm, d), lambda i, j: (i, 0)),
        out_shape=jax.ShapeDtypeStruct((m, d), F32),
        scratch_shapes=[pltpu.VMEM((tm, d), BF16)],
        compiler_params=_params("parallel", "arbitrary", vmem_bytes=vmem_bytes),
        name="ffn",
    )(x, g.reshape(1, d), w_gate_up, w_gate_up, w_down)


def _swa_kernel(sink_ref, cur_ref, prev_ref, qn_ref, kn_ref, sel_ref, o_ref):
    n = pl.program_id(1)
    nq = N_KV_A * GROUP_A * HD_A
    nkv = N_KV_A * HD_A
    kj = lax.broadcasted_iota(jnp.int32, (2 * BLOCK_A, 2 * BLOCK_A), 0)
    qi = lax.broadcasted_iota(jnp.int32, (2 * BLOCK_A, 2 * BLOCK_A), 1) & (BLOCK_A - 1)
    mask2 = (kj > qi) & (kj <= qi + BLOCK_A) & ((n > 0) | (kj >= BLOCK_A))
    mask_bias = jnp.where(mask2, 0.0, NEG_INF)
    first_half = lax.broadcasted_iota(jnp.int32, (1, 2 * BLOCK_A), 1) < BLOCK_A
    q_bf = cur_ref[0, :, :nq]
    xq = q_bf.astype(F32)
    sq = xq * xq
    sq_hi = sq.astype(BF16)
    sq_lo = (sq - sq_hi.astype(F32)).astype(BF16)
    ssq = _dot(sq_hi, sel_ref[...]) + _dot(sq_lo, sel_ref[...])
    rq_t = (lax.rsqrt(ssq * (1.0 / HD_A) + EPS) * LOG2E).T
    kscale = qn_ref[...] * kn_ref[...] * HD_A ** -0.5
    zeros = jnp.zeros((2 * BLOCK_A, HD_A), F32)
    v_all = jnp.concatenate([prev_ref[0, :, nkv:2 * nkv], cur_ref[0, :, nq + nkv:nq + 2 * nkv]], axis=0)
    v_t = v_all.astype(F32).T.astype(BF16)
    ones_rows = jnp.ones((ONES_ROWS, 2 * BLOCK_A), BF16)
    v_ones = [jnp.concatenate([v_t[h * HD_A:(h + 1) * HD_A, :], ones_rows], axis=0)
              for h in range(N_KV_A)]
    k_pads = []
    for h in range(N_KV_A):
        k = jnp.concatenate([prev_ref[0, :, h * HD_A:(h + 1) * HD_A],
                             cur_ref[0, :, nq + h * HD_A:nq + (h + 1) * HD_A]], axis=0).astype(F32)
        kn = k * lax.rsqrt(jnp.mean(k * k, axis=-1, keepdims=True) + EPS) * kscale
        k_pads.append((jnp.concatenate([kn, zeros], axis=1).astype(BF16),
                       jnp.concatenate([zeros, kn], axis=1).astype(BF16)))

    def logits(i):
        j = 2 * i
        k_pad = k_pads[j // GROUP_A]
        q_tile = q_bf[:, j * HD_A:(j + 2) * HD_A]
        rq = jnp.concatenate([rq_t[j:j + 1, :], rq_t[j + 1:j + 2, :]], axis=1)
        s = jnp.concatenate([_nt_dot(k_pad[0], q_tile), _nt_dot(k_pad[1], q_tile)], axis=1) * rq
        return s + mask_bias

    def attend(i, s):
        j = 2 * i
        sink = jnp.where(first_half, sink_ref[j], sink_ref[j + 1]) * LOG2E
        mx = jnp.maximum(jnp.max(s, axis=0, keepdims=True), sink)
        p = jnp.exp2(s - mx)
        return _dot(v_ones[j // GROUP_A], p.astype(BF16)), jnp.exp2(sink - mx)

    def finish(i, pv, sink_term):
        o_t = pv[:HD_A] / (pv[HD_A:HD_A + 1] + sink_term)
        o_pair = jnp.concatenate([o_t[:, :BLOCK_A], o_t[:, BLOCK_A:]], axis=0)
        o_ref[0, :, 2 * i * HD_A:(2 * i + 2) * HD_A] = o_pair.T.astype(o_ref.dtype)

    n_pairs = N_KV_A * GROUP_A // 2
    ahead = 4
    pending = [logits(i) for i in range(ahead)]
    unfinished = None
    for i in range(n_pairs):
        if i + ahead < n_pairs:
            pending.append(logits(i + ahead))
        out = attend(i, pending.pop(0))
        if unfinished is not None:
            finish(i - 1, *unfinished)
        unfinished = out
    finish(n_pairs - 1, *unfinished)


def swa_attention(qkv, q_norm, k_norm, sinks):
    b, l, n = qkv.shape
    nq = N_KV_A * GROUP_A * HD_A
    kv_w = 2 * N_KV_A * HD_A
    head_sel = (jnp.arange(nq)[:, None] // HD_A == jnp.arange(LANES)[None, :]).astype(BF16)
    return pl.pallas_call(
        _swa_kernel,
        grid=(b, l // BLOCK_A),
        in_specs=[
            pl.BlockSpec(memory_space=pltpu.SMEM),
            pl.BlockSpec((1, BLOCK_A, n), lambda i, j: (i, j, 0)),
            pl.BlockSpec((1, BLOCK_A, kv_w), lambda i, j: (i, jnp.maximum(j - 1, 0), nq // kv_w)),
            pl.BlockSpec((1, HD_A), lambda i, j: (0, 0)),
            pl.BlockSpec((1, HD_A), lambda i, j: (0, 0)),
            pl.BlockSpec((nq, LANES), lambda i, j: (0, 0)),
        ],
        out_specs=pl.BlockSpec((1, BLOCK_A, nq), lambda i, j: (i, j, 0)),
        out_shape=jax.ShapeDtypeStruct((b, l, nq), BF16),
        compiler_params=_params("parallel", "parallel"),
        name="swa_attention",
    )(sinks, qkv, qkv, q_norm.reshape(1, HD_A), k_norm.reshape(1, HD_A), head_sel)


def _split3(x):
    hi = x.astype(BF16)
    r1 = x - hi.astype(F32)
    mid = r1.astype(BF16)
    lo = (r1 - mid.astype(F32)).astype(BF16)
    return hi, mid, lo


def _gla_kernel(q_ref, k_ref, v_ref, r_ref, gl_ref, wgb_ref, gb_ref, on_ref, o_ref, st_ref,
                *, tb, dk, dv):
    @pl.when(pl.program_id(1) == 0)
    def _():
        st_ref[...] = jnp.zeros_like(st_ref)

    ti = lax.broadcasted_iota(jnp.int32, (CHUNK, CHUNK), 0)
    si = lax.broadcasted_iota(jnp.int32, (CHUNK, CHUNK), 1)
    causal = si <= ti
    tri = causal.astype(BF16)
    wgb = wgb_ref[...]
    gb = gb_ref[...]
    on = on_ref[...]

    def decays(c):
        rs = pl.ds(c * CHUNK, CHUNK)
        gate = _dot(gl_ref[0, rs, :].astype(BF16), wgb) + gb
        log_a = (jnp.minimum(gate, 0.0) - jnp.log1p(jnp.exp(-jnp.abs(gate)))) / GATE_TAU
        hi, mid, lo = _split3(log_a)
        bcum = _dot(tri, hi) + _dot(tri, mid) + _dot(tri, lo)
        b_last = bcum[CHUNK - 1:CHUNK, :]
        q = q_ref[0, rs, :].astype(F32)
        k = k_ref[0, rs, :].astype(F32)
        q_dec = (q * dk ** -0.5 * jnp.exp(bcum)).astype(BF16)
        k_inv = (k * jnp.exp(-bcum)).astype(BF16)
        k_end = (k * jnp.exp(b_last - bcum)).astype(BF16)
        return q_dec, k_inv, k_end, jnp.exp(b_last)

    def recur(c, q_dec, k_inv, k_end, decay):
        rs = pl.ds(c * CHUNK, CHUNK)
        for h in range(N_HEADS_B):
            ks = slice(h * dk, (h + 1) * dk)
            vs = slice(h * dv, (h + 1) * dv)
            v = v_ref[0, rs, vs].astype(BF16)
            a = jnp.where(causal, _nt_dot(q_dec[:, ks], k_inv[:, ks]), 0.0)
            st = st_ref[h]
            o = _dot(a.astype(BF16), v) + _nt_dot(q_dec[:, ks], st.astype(BF16))
            st_ref[h] = st * decay[:, ks] + _tn_dot(v, k_end[:, ks])
            o = _rms(o, on)
            r = r_ref[0, rs, vs].astype(F32)
            o_ref[0, rs, vs] = (o * (r * (1.0 / (1.0 + jnp.exp(-r))))).astype(o_ref.dtype)

    n_chunks = tb // CHUNK
    ready = decays(0)
    for c in range(n_chunks):
        cur = ready
        if c + 1 < n_chunks:
            ready = decays(c + 1)
        recur(c, *cur)


def gla_attention(proj, g_low, w_gate_b, gate_bias, o_norm, *, dk, dv, tb):
    b, l, _ = proj.shape
    dk_all, dv_all = N_HEADS_B * dk, N_HEADS_B * dv
    return pl.pallas_call(
        functools.partial(_gla_kernel, tb=tb, dk=dk, dv=dv),
        grid=(b, l // tb),
        in_specs=[
            pl.BlockSpec((1, tb, dk_all), lambda i, j: (i, j, 0)),
            pl.BlockSpec((1, tb, dk_all), lambda i, j: (i, j, 1)),
            pl.BlockSpec((1, tb, dv_all), lambda i, j: (i, j, 2 * dk_all // dv_all)),
            pl.BlockSpec((1, tb, dv_all), lambda i, j: (i, j, 2 * dk_all // dv_all + 1)),
            pl.BlockSpec((1, tb, LANES), lambda i, j: (i, j, 0)),
            pl.BlockSpec((LANES, dk_all), lambda i, j: (0, 0)),
            pl.BlockSpec((1, dk_all), lambda i, j: (0, 0)),
            pl.BlockSpec((1, dv), lambda i, j: (0, 0)),
        ],
        out_specs=pl.BlockSpec((1, tb, dv_all), lambda i, j: (i, j, 0)),
        out_shape=jax.ShapeDtypeStruct((b, l, dv_all), BF16),
        scratch_shapes=[pltpu.VMEM((N_HEADS_B, dv, dk), F32)],
        compiler_params=_params("parallel", "arbitrary"),
        name="gla_attention",
    )(proj, proj, proj, proj, g_low, w_gate_b, gate_bias.reshape(1, dk_all), o_norm.reshape(1, dv))


def _dsa_prep_kernel(x_ref, g_ref, win_ref, qln_ref, kvn_ref, wuq_ref, wuk_ref, qan_ref, wiq_ref,
                     ikn_ref, ikb_ref, qabs_ref, qidx_ref, widx_ref, ckv_ref, ckvt_ref, kidx_ref):
    xn = _rms(x_ref[...], g_ref[...]).astype(BF16)
    proj = _nt_dot(xn, win_ref[...])
    o_kv = Q_RANK_C
    o_ki = Q_RANK_C + KV_RANK_C
    o_wi = o_ki + IDX_DIM
    c_q = _rms(proj[:, :o_kv], qln_ref[...]).astype(BF16)
    c_kv = _rms(proj[:, o_kv:o_ki], kvn_ref[...])
    ckv_ref[...] = c_kv.astype(BF16)
    ckvt_ref[...] = c_kv.T.astype(BF16)
    k_idx = proj[:, o_ki:o_wi]
    mu = jnp.mean(k_idx, axis=-1, keepdims=True)
    kc = k_idx - mu
    k_idx = kc * lax.rsqrt(jnp.mean(kc * kc, axis=-1, keepdims=True) + EPS)
    kidx_ref[...] = (k_idx * ikn_ref[...] + ikb_ref[...]).astype(BF16)
    widx_ref[...] = proj[:, o_wi:o_wi + N_IDX_HEADS] * (N_IDX_HEADS ** -0.5 * IDX_DIM ** -0.5)
    q_nope = _dot(c_q, wuq_ref[...])
    q_idx = _dot(c_q, wiq_ref[...])
    qan = qan_ref[...] * (KV_RANK_C ** -0.5 * LOG2E)
    for h in range(N_HEADS_C):
        qh = q_nope[:, h * QK_HEAD_C:(h + 1) * QK_HEAD_C].astype(BF16)
        qa = _rms(_dot(qh, wuk_ref[h]), qan).astype(BF16)
        qi = q_idx[:, h * IDX_DIM:(h + 1) * IDX_DIM].astype(BF16)
        for j in range(qabs_ref.shape[0]):
            qabs_ref[j, h] = qa[j * C_BLOCK:(j + 1) * C_BLOCK]
            qidx_ref[j, h] = qi[j * C_BLOCK:(j + 1) * C_BLOCK]


def dsa_prep(x, g, w_in, q_lat_norm, kv_norm, w_uq, w_uk, q_abs_norm, w_iq, idx_k_norm, idx_k_bias,
             *, tm):
    m, d = x.shape
    nblk = m // C_BLOCK
    bpt = tm // C_BLOCK
    n_in = w_in.shape[0]
    const2 = lambda i: (0, 0)
    return pl.pallas_call(
        _dsa_prep_kernel,
        grid=(m // tm,),
        in_specs=[
            pl.BlockSpec((tm, d), lambda i: (i, 0)),
            pl.BlockSpec((1, d), const2),
            pl.BlockSpec((n_in, d), const2),
            pl.BlockSpec((1, Q_RANK_C), const2),
            pl.BlockSpec((1, KV_RANK_C), const2),
            pl.BlockSpec(w_uq.shape, const2),
            pl.BlockSpec(w_uk.shape, lambda i: (0, 0, 0)),
            pl.BlockSpec((1, KV_RANK_C), const2),
            pl.BlockSpec(w_iq.shape, const2),
            pl.BlockSpec((1, IDX_DIM), const2),
            pl.BlockSpec((1, IDX_DIM), const2),
        ],
        out_specs=[
            pl.BlockSpec((bpt, N_HEADS_C, C_BLOCK, KV_RANK_C), lambda i: (i, 0, 0, 0)),
            pl.BlockSpec((bpt, N_IDX_HEADS, C_BLOCK, IDX_DIM), lambda i: (i, 0, 0, 0)),
            pl.BlockSpec((tm, N_IDX_HEADS), lambda i: (i, 0)),
            pl.BlockSpec((tm, KV_RANK_C), lambda i: (i, 0)),
            pl.BlockSpec((KV_RANK_C, tm), lambda i: (0, i)),
            pl.BlockSpec((tm, IDX_DIM), lambda i: (i, 0)),
        ],
        out_shape=[
            jax.ShapeDtypeStruct((nblk, N_HEADS_C, C_BLOCK, KV_RANK_C), BF16),
            jax.ShapeDtypeStruct((nblk, N_IDX_HEADS, C_BLOCK, IDX_DIM), BF16),
            jax.ShapeDtypeStruct((m, N_IDX_HEADS), F32),
            jax.ShapeDtypeStruct((m, KV_RANK_C), BF16),
            jax.ShapeDtypeStruct((KV_RANK_C, m), BF16),
            jax.ShapeDtypeStruct((m, IDX_DIM), BF16),
        ],
        compiler_params=_params("parallel"),
        name="dsa_prep",
    )(x, g.reshape(1, d), w_in, q_lat_norm.reshape(1, -1), kv_norm.reshape(1, -1), w_uq, w_uk,
      q_abs_norm.reshape(1, -1), w_iq, idx_k_norm.reshape(1, -1), idx_k_bias.reshape(1, -1))


def _sortable_key(s):
    bits = pltpu.bitcast(s, jnp.int32)
    return bits ^ ((bits >> 31) & 0x7FFFFFFF)


def _dsa_attn_kernel(qabs_ref, qidx_ref, widx_ref, ckv_ref, ckvt_ref, kidx_ref, wuv_ref, qan_ref,
                     kvn_ref, o_ref, key_ref, m_ref, acc_ref, *, topk, idx_bits):
    n = pl.program_id(1)
    tk = KEY_TILE_C
    n_tiles = (n * C_BLOCK + C_BLOCK + tk - 1) // tk
    q_row = n * C_BLOCK + lax.broadcasted_iota(jnp.int32, (C_BLOCK, tk), 0)
    k_col = lax.broadcasted_iota(jnp.int32, (C_BLOCK, tk), 1)
    k_row = lax.broadcasted_iota(jnp.int32, (tk, C_BLOCK), 0)
    widx = widx_ref[0, 0]

    def tile(t):
        return pl.ds(pl.multiple_of(t * tk, tk), tk)

    def score_tile(t, carry):
        kt = kidx_ref[0, tile(t), :]
        score = jnp.zeros((C_BLOCK, tk), F32)
        for h in range(N_IDX_HEADS):
            rel = jnp.maximum(_nt_dot(qidx_ref[0, h], kt), 0.0)
            score = score + rel * widx[:, h:h + 1]
        score = jnp.where(t * tk + k_col <= q_row, score, NEG_INF)
        key_ref[tile(t), :] = _sortable_key(score.T)
        return carry

    lax.fori_loop(0, n_tiles, score_tile, 0)

    def count(pred):
        def body(t, cnt):
            hit = pred(key_ref[tile(t), :], t).astype(jnp.int32)
            return cnt + jnp.sum(hit.reshape(tk // 8, 8, C_BLOCK), axis=0)
        cnt = lax.fori_loop(0, n_tiles, body, jnp.zeros((8, C_BLOCK), jnp.int32))
        return jnp.sum(cnt, axis=0, keepdims=True)

    def bisect(i, carry):
        lo, n_lo = carry
        cand = lo + lax.shift_left(jnp.int32(1), 31 - i)
        n_cand = count(lambda k, t: k >= cand)
        take = n_cand >= topk
        return jnp.where(take, cand, lo), jnp.where(take, n_cand, n_lo)

    thresh, n_ge = lax.fori_loop(
        0, 32, bisect, (jnp.full((1, C_BLOCK), INT_MIN, jnp.int32),
                        jnp.zeros((1, C_BLOCK), jnp.int32) + n_tiles * tk))

    excess = jnp.max(jnp.where((n_ge > topk) & (thresh > NEG_KEY), 1, 0))

    @pl.when(excess > 0)
    def _():
        want = topk - count(lambda k, t: k > thresh)

        def index_bisect(i, j0):
            cand = j0 + lax.shift_left(jnp.int32(1), idx_bits - 1 - i)
            before = count(lambda k, t: (k == thresh) & (t * tk + k_row < cand))
            return jnp.where(before < want, cand, j0)

        j0 = lax.fori_loop(0, idx_bits, index_bisect, jnp.zeros((1, C_BLOCK), jnp.int32))

        def demote(t, carry):
            keys = key_ref[tile(t), :]
            drop = (keys == thresh) & (t * tk + k_row > j0)
            key_ref[tile(t), :] = jnp.where(drop, thresh - 1, keys)
            return carry

        lax.fori_loop(0, n_tiles, demote, 0)

    thr = jnp.maximum(thresh, NEG_KEY + 1)
    thr2 = jnp.concatenate([thr, thr], axis=1)
    acc_ref[...] = jnp.zeros_like(acc_ref)
    ones_rows = jnp.ones((ONES_ROWS, tk), BF16)
    n_pairs = N_HEADS_C // 2

    def pair_q(p):
        return qabs_ref[0, 2 * p:2 * p + 2].reshape(2 * C_BLOCK, KV_RANK_C)

    bound = (KV_RANK_C * BOUND_SLACK_C * (KV_RANK_C ** -0.5 * LOG2E)
             * jnp.max(jnp.abs(qan_ref[...])) * jnp.max(jnp.abs(kvn_ref[...])))
    small_logits = bound <= MAX_FIXED_OFFSET_C

    def attn_loop(accumulate, offset):
        def attn_tile(t, carry):
            kv = ckv_ref[0, tile(t), :]
            kvt = jnp.concatenate([ckvt_ref[:, tile(t)], ones_rows], axis=0)
            keys = key_ref[tile(t), :]
            sel = jnp.concatenate([keys, keys], axis=1) >= thr2
            bias = jnp.where(sel, -offset, NEG_INF)

            def logits(p):
                return _nt_dot(kv, pair_q(p)) + bias

            pending = [logits(p) for p in range(ATTN_AHEAD_C)]
            for p in range(n_pairs):
                if p + ATTN_AHEAD_C < n_pairs:
                    pending.append(logits(p + ATTN_AHEAD_C))
                accumulate(p, pending.pop(0), kvt)
            return carry

        lax.fori_loop(0, n_tiles, attn_tile, 0)

    def accumulate_fixed(p, s, kvt):
        acc_ref[p] += _dot(kvt, jnp.exp2(s).astype(BF16))

    def accumulate_online(p, s, kvt):
        m_old = m_ref[p]
        m_new = jnp.maximum(m_old, jnp.max(s, axis=0, keepdims=True))
        pe = jnp.exp2(s - m_new)
        acc_ref[p] = jnp.exp2(m_old - m_new) * acc_ref[p] + _dot(kvt, pe.astype(BF16))
        m_ref[p] = m_new

    @pl.when(small_logits)
    def _():
        attn_loop(accumulate_fixed, bound)

    @pl.when(jnp.logical_not(small_logits))
    def _():
        m_ref[...] = jnp.full(m_ref.shape, NEG_INF, F32)
        attn_loop(accumulate_online, 0.0)

    for p in range(N_HEADS_C // 2):
        denom = acc_ref[p, KV_RANK_C:KV_RANK_C + 1, :]
        o_lat_t = (acc_ref[p, :KV_RANK_C, :] / denom).astype(BF16)
        for e in range(2):
            h = 2 * p + e
            o_ref[0, :, h * V_HEAD_C:(h + 1) * V_HEAD_C] = _tn_dot(
                o_lat_t[:, e * C_BLOCK:(e + 1) * C_BLOCK], wuv_ref[h]).astype(o_ref.dtype)


def dsa_attention(qabs, qidx, widx, ckv, ckvt, kidx, w_uv, q_abs_norm, kv_norm, *, batch, topk):
    nblk = qabs.shape[0]
    nb = nblk // batch
    l = nb * C_BLOCK
    assert l % KEY_TILE_C == 0 and l >= topk
    ckv = ckv.reshape(batch, l, KV_RANK_C)
    kidx = kidx.reshape(batch, l, IDX_DIM)
    widx = widx.reshape(batch, nb, C_BLOCK, N_IDX_HEADS)
    return pl.pallas_call(
        functools.partial(_dsa_attn_kernel, topk=topk, idx_bits=(l - 1).bit_length()),
        grid=(batch, nb),
        in_specs=[
            pl.BlockSpec((1, N_HEADS_C, C_BLOCK, KV_RANK_C), lambda b, j: (b * nb + j, 0, 0, 0)),
            pl.BlockSpec((1, N_IDX_HEADS, C_BLOCK, IDX_DIM), lambda b, j: (b * nb + j, 0, 0, 0)),
            pl.BlockSpec((1, 1, C_BLOCK, N_IDX_HEADS), lambda b, j: (b, j, 0, 0)),
            pl.BlockSpec((1, l, KV_RANK_C), lambda b, j: (b, 0, 0)),
            pl.BlockSpec((KV_RANK_C, l), lambda b, j: (0, b)),
            pl.BlockSpec((1, l, IDX_DIM), lambda b, j: (b, 0, 0)),
            pl.BlockSpec(w_uv.shape, lambda b, j: (0, 0, 0)),
            pl.BlockSpec((1, KV_RANK_C), lambda b, j: (0, 0)),
            pl.BlockSpec((1, KV_RANK_C), lambda b, j: (0, 0)),
        ],
        out_specs=pl.BlockSpec((1, C_BLOCK, N_HEADS_C * V_HEAD_C), lambda b, j: (b, j, 0)),
        out_shape=jax.ShapeDtypeStruct((batch, l, N_HEADS_C * V_HEAD_C), BF16),
        scratch_shapes=[
            pltpu.VMEM((l, C_BLOCK), jnp.int32),
            pltpu.VMEM((N_HEADS_C // 2, 1, 2 * C_BLOCK), F32),
            pltpu.VMEM((N_HEADS_C // 2, KV_RANK_C + ONES_ROWS, 2 * C_BLOCK), F32),
        ],
        compiler_params=_params("parallel", "arbitrary"),
        name="dsa_attention",
    )(qabs, qidx, widx, ckv, ckvt, kidx, w_uv, q_abs_norm.reshape(1, -1), kv_norm.reshape(1, -1))


TM = 512
TF = 512
FFN_NORM_ROWS = 512


def _swa_layer(h, batch, norm_mix, w_in, q_norm, k_norm, sinks, wo):
    m, d = h.shape
    qkv = norm_matmul(h, norm_mix, w_in.astype(BF16), tm=TM, tn=w_in.shape[1], out_dtype=BF16)
    o = swa_attention(qkv.reshape(batch, m // batch, -1), q_norm, k_norm, sinks)
    return matmul_res(o.reshape(m, -1), wo.astype(BF16), h, tm=TM, tn=d)


def _gla_layer(h, batch, norm_mix, w_in, w_gate_b, gate_bias, o_norm, wo):
    m, d = h.shape
    dk_all = w_gate_b.shape[1]
    dv_all = wo.shape[0]
    n_main = 2 * dk_all + 2 * dv_all
    wgb_pad = jnp.pad(w_gate_b.astype(BF16), ((0, LANES - GATE_RANK), (0, 0)))
    proj, g_low = norm_matmul_gate(h, norm_mix, w_in, n_main, tm=2 * TM, tn=n_main // 8)
    o = gla_attention(proj.reshape(batch, m // batch, -1), g_low.reshape(batch, m // batch, -1),
                      wgb_pad, gate_bias, o_norm,
                      dk=dk_all // N_HEADS_B, dv=dv_all // N_HEADS_B, tb=256)
    return matmul_res(o.reshape(m, -1), wo.astype(BF16), h, tm=TM, tn=d)


def _dsa_layer(h, batch, norm_mix, w_in, q_lat_norm, kv_norm, w_uq, w_uk, q_abs_norm, w_iq,
               idx_k_norm, idx_k_bias, w_uv, wo):
    m, d = h.shape
    l = m // batch
    qabs, qidx, widx, ckv, ckvt, kidx = dsa_prep(
        h, norm_mix, w_in.T.astype(BF16), q_lat_norm, kv_norm, w_uq.astype(BF16), w_uk.astype(BF16),
        q_abs_norm, w_iq.astype(BF16), idx_k_norm, idx_k_bias, tm=TM)
    o = dsa_attention(qabs, qidx, widx, ckv, ckvt, kidx, w_uv.astype(BF16), q_abs_norm, kv_norm, batch=batch,
                      topk=min(TOPK_MAX, l // 4))
    return matmul_res(o.reshape(m, -1), wo.astype(BF16), h, tm=TM, tn=d)


def kernel(x, l0_norm_mix, l0_w_in, l0_q_norm, l0_k_norm, l0_sinks, l0_wo, l0_norm_ffn, l0_w_gate_up, l0_w_down, l1_norm_mix, l1_w_in, l1_w_gate_b, l1_gate_bias, l1_o_norm, l1_wo, l1_norm_ffn, l1_w_gate_up, l1_w_down, l2_norm_mix, l2_w_in, l2_q_lat_norm, l2_kv_norm, l2_w_uq, l2_w_uk, l2_q_abs_norm, l2_w_iq, l2_idx_k_norm, l2_idx_k_bias, l2_w_uv, l2_wo, l2_norm_ffn, l2_w_gate_up, l2_w_down, l3_norm_mix, l3_w_in, l3_q_norm, l3_k_norm, l3_sinks, l3_wo, l3_norm_ffn, l3_w_gate_up, l3_w_down):
    batch, seq, d = x.shape
    h = x.reshape(batch * seq, d)

    def channel_mix(h, norm_ffn, w_gate_up, w_down):
        return ffn(h, norm_ffn, w_gate_up, w_down, tm=2 * TM, tf=TF // 2)

    h = _swa_layer(h, batch, l0_norm_mix, l0_w_in, l0_q_norm, l0_k_norm, l0_sinks, l0_wo)
    h = channel_mix(h, l0_norm_ffn, l0_w_gate_up, l0_w_down)
    h = _gla_layer(h, batch, l1_norm_mix, l1_w_in, l1_w_gate_b, l1_gate_bias, l1_o_norm, l1_wo)
    h = channel_mix(h, l1_norm_ffn, l1_w_gate_up, l1_w_down)
    h = _dsa_layer(h, batch, l2_norm_mix, l2_w_in, l2_q_lat_norm, l2_kv_norm, l2_w_uq, l2_w_uk,
                   l2_q_abs_norm, l2_w_iq, l2_idx_k_norm, l2_idx_k_bias, l2_w_uv, l2_wo)
    h = channel_mix(h, l2_norm_ffn, l2_w_gate_up, l2_w_down)
    h = _swa_layer(h, batch, l3_norm_mix, l3_w_in, l3_q_norm, l3_k_norm, l3_sinks, l3_wo)
    h = channel_mix(h, l3_norm_ffn, l3_w_gate_up, l3_w_down)
    return h.reshape(batch, seq, d)
```

```python
import functools

import jax
import jax.numpy as jnp
import numpy as np
from jax import lax
from jax.experimental import pallas as pl
from jax.experimental.pallas import tpu as pltpu

F32 = jnp.float32
BF16 = jnp.bfloat16
EPS = 1e-6
NEG_INF = -1e30
LANES = 128

HD_A = 64
N_KV_A = 4
GROUP_A = 8
BLOCK_A = 128
N_HEADS_B = 4
GATE_RANK = 16
GATE_TAU = 16.0
CHUNK = 64
N_HEADS_C = 16
QK_HEAD_C = 128
V_HEAD_C = 128
Q_RANK_C = 512
KV_RANK_C = 256
N_IDX_HEADS = 16
IDX_DIM = 64
TOPK_MAX = 256
C_BLOCK = 128
KEY_TILE_C = 512
ATTN_AHEAD_C = 3
ONES_ROWS = 16
BOUND_SLACK_C = 1.01
MAX_FIXED_OFFSET_C = 56.0
INT_MIN = -2 ** 31
LOG2E = 1.4426950408889634
_NEG_BITS = int(np.float32(NEG_INF).view(np.int32))
NEG_KEY = _NEG_BITS ^ ((_NEG_BITS >> 31) & 0x7FFFFFFF)


def _nt_dot(a, b):
    return lax.dot_general(a, b, (((1,), (1,)), ((), ())), preferred_element_type=F32)


def _tn_dot(a, b):
    return lax.dot_general(a, b, (((0,), (0,)), ((), ())), preferred_element_type=F32)


def _dot(a, b):
    return jnp.dot(a, b, preferred_element_type=F32)


def _rms(x, g):
    return x * lax.rsqrt(jnp.mean(x * x, axis=-1, keepdims=True) + EPS) * g


def _params(*sem, vmem_bytes=None):
    return pltpu.CompilerParams(dimension_semantics=sem, vmem_limit_bytes=vmem_bytes)


def _norm_matmul_kernel(x_ref, g_ref, w_ref, o_ref, xn_ref):
    @pl.when(pl.program_id(1) == 0)
    def _():
        xn_ref[...] = _rms(x_ref[...], g_ref[...]).astype(BF16)

    o_ref[...] = _dot(xn_ref[...], w_ref[...].astype(BF16)).astype(o_ref.dtype)


def norm_matmul(x, g, w, *, tm, tn, out_dtype=F32):
    m, d = x.shape
    n = w.shape[1]
    assert m % tm == 0 and n % tn == 0
    return pl.pallas_call(
        _norm_matmul_kernel,
        grid=(m // tm, n // tn),
        in_specs=[
            pl.BlockSpec((tm, d), lambda i, j: (i, 0)),
            pl.BlockSpec((1, d), lambda i, j: (0, 0)),
            pl.BlockSpec((d, tn), lambda i, j: (0, j)),
        ],
        out_specs=pl.BlockSpec((tm, tn), lambda i, j: (i, j)),
        out_shape=jax.ShapeDtypeStruct((m, n), out_dtype),
        scratch_shapes=[pltpu.VMEM((tm, d), BF16)],
        compiler_params=_params("parallel", "arbitrary"),
        name="norm_matmul",
    )(x, g.reshape(1, d), w)


def _norm_matmul_gate_kernel(x_ref, g_ref, w_ref, wg_ref, o_ref, gl_ref, xn_ref):
    @pl.when(pl.program_id(1) == 0)
    def _():
        xn = _rms(x_ref[...], g_ref[...]).astype(BF16)
        xn_ref[...] = xn
        gl_ref[...] = _nt_dot(xn, wg_ref[...])

    o_ref[...] = _nt_dot(xn_ref[...], w_ref[...].astype(BF16)).astype(o_ref.dtype)


def norm_matmul_gate(x, g, w, n_main, *, tm, tn):
    m, d = x.shape
    assert m % tm == 0 and n_main % tn == 0
    w_t = w.T
    w_gate = jnp.pad(w_t[n_main:].astype(BF16), ((0, LANES - (w.shape[1] - n_main)), (0, 0)))
    vmem_bytes = (2 * tm * d * 4 + tm * d * 2 + 2 * d * tn * 4 + d * tn * 2
                  + 2 * tm * tn * 2 + tm * tn * 4 + 2 * tm * LANES * 4 + 2 * d * LANES * 2)
    return pl.pallas_call(
        _norm_matmul_gate_kernel,
        grid=(m // tm, n_main // tn),
        in_specs=[
            pl.BlockSpec((tm, d), lambda i, j: (i, 0)),
            pl.BlockSpec((1, d), lambda i, j: (0, 0)),
            pl.BlockSpec((tn, d), lambda i, j: (j, 0)),
            pl.BlockSpec((LANES, d), lambda i, j: (0, 0)),
        ],
        out_specs=[
            pl.BlockSpec((tm, tn), lambda i, j: (i, j)),
            pl.BlockSpec((tm, LANES), lambda i, j: (i, 0)),
        ],
        out_shape=[
            jax.ShapeDtypeStruct((m, n_main), BF16),
            jax.ShapeDtypeStruct((m, LANES), F32),
        ],
        scratch_shapes=[pltpu.VMEM((tm, d), BF16)],
        compiler_params=_params("parallel", "arbitrary", vmem_bytes=vmem_bytes),
        name="norm_matmul_gate",
    )(x, g.reshape(1, d), w_t, w_gate)


def _matmul_res_kernel(a_ref, w_ref, r_ref, o_ref):
    o_ref[...] = r_ref[...] + _dot(a_ref[...], w_ref[...].astype(BF16))


def matmul_res(a, w, r, *, tm, tn):
    m, k = a.shape
    n = w.shape[1]
    assert m % tm == 0 and n % tn == 0
    return pl.pallas_call(
        _matmul_res_kernel,
        grid=(m // tm, n // tn),
        in_specs=[
            pl.BlockSpec((tm, k), lambda i, j: (i, 0)),
            pl.BlockSpec((k, tn), lambda i, j: (0, j)),
            pl.BlockSpec((tm, tn), lambda i, j: (i, j)),
        ],
        out_specs=pl.BlockSpec((tm, tn), lambda i, j: (i, j)),
        out_shape=jax.ShapeDtypeStruct((m, n), F32),
        compiler_params=_params("parallel", "arbitrary"),
        name="matmul_res",
    )(a, w, r)


def _ffn_kernel(x_ref, g_ref, wg_ref, wu_ref, wd_ref, o_ref, xn_ref, *, row_chunks):
    def weights():
        return wg_ref[...].astype(BF16), wu_ref[...].astype(BF16), wd_ref[...].astype(BF16)

    def swiglu_tile(xn, wg, wu, wd):
        gate = _dot(xn, wg)
        up = _dot(xn, wu)
        act = (gate * (1.0 / (1.0 + jnp.exp(-gate))) * up).astype(BF16)
        return _dot(act, wd)

    @pl.when(pl.program_id(1) == 0)
    def _():
        w = weights()
        rows = x_ref.shape[0] // row_chunks
        for c in range(row_chunks):
            rs = slice(c * rows, (c + 1) * rows)
            x = x_ref[rs, :]
            xn = _rms(x, g_ref[...]).astype(BF16)
            xn_ref[rs, :] = xn
            o_ref[rs, :] = x + swiglu_tile(xn, *w)

    @pl.when(pl.program_id(1) != 0)
    def _():
        o_ref[...] += swiglu_tile(xn_ref[...], *weights())


def ffn(x, g, w_gate_up, w_down, *, tm, tf):
    m, d = x.shape
    ff = w_down.shape[0]
    nf = ff // tf
    assert m % tm == 0 and ff % tf == 0
    w_bytes = w_down.dtype.itemsize
    vmem_bytes = (4 * tm * d * 4 + tm * d * 2 + 6 * d * tf * w_bytes
                  + 3 * tm * tf * 4 + 3 * d * tf * 2)
    return pl.pallas_call(
        functools.partial(_ffn_kernel, row_chunks=max(1, tm // FFN_NORM_ROWS)),
        grid=(m // tm, nf),
        in_specs=[
            pl.BlockSpec((tm, d), lambda i, j: (i, 0)),
            pl.BlockSpec((1, d), lambda i, j: (0, 0)),
            pl.BlockSpec((d, tf), lambda i, j: (0, j)),
            pl.BlockSpec((d, tf), lambda i, j: (0, j + nf)),
            pl.BlockSpec((tf, d), lambda i, j: (j, 0)),
        ],
        out_specs=pl.BlockSpec((tm, d), lambda i, j: (i, 0)),
        out_shape=jax.ShapeDtypeStruct((m, d), F32),
        scratch_shapes=[pltpu.VMEM((tm, d), BF16)],
        compiler_params=_params("parallel", "arbitrary", vmem_bytes=vmem_bytes),
        name="ffn",
    )(x, g.reshape(1, d), w_gate_up, w_gate_up, w_down)


def _swa_kernel(sink_ref, cur_ref, prev_ref, qn_ref, kn_ref, sel_ref, o_ref):
    n = pl.program_id(1)
    nq = N_KV_A * GROUP_A * HD_A
    nkv = N_KV_A * HD_A
    kj = lax.broadcasted_iota(jnp.int32, (2 * BLOCK_A, 2 * BLOCK_A), 0)
    qi = lax.broadcasted_iota(jnp.int32, (2 * BLOCK_A, 2 * BLOCK_A), 1) & (BLOCK_A - 1)
    mask2 = (kj > qi) & (kj <= qi + BLOCK_A) & ((n > 0) | (kj >= BLOCK_A))
    mask_bias = jnp.where(mask2, 0.0, NEG_INF)
    first_half = lax.broadcasted_iota(jnp.int32, (1, 2 * BLOCK_A), 1) < BLOCK_A
    q_bf = cur_ref[0, :, :nq]
    xq = q_bf.astype(F32)
    sq = xq * xq
    sq_hi = sq.astype(BF16)
    sq_lo = (sq - sq_hi.astype(F32)).astype(BF16)
    ssq = _dot(sq_hi, sel_ref[...]) + _dot(sq_lo, sel_ref[...])
    rq_t = (lax.rsqrt(ssq * (1.0 / HD_A) + EPS) * LOG2E).T
    kscale = qn_ref[...] * kn_ref[...] * HD_A ** -0.5
    zeros = jnp.zeros((2 * BLOCK_A, HD_A), F32)
    v_all = jnp.concatenate([prev_ref[0, :, nkv:2 * nkv], cur_ref[0, :, nq + nkv:nq + 2 * nkv]], axis=0)
    v_t = v_all.astype(F32).T.astype(BF16)
    ones_rows = jnp.ones((ONES_ROWS, 2 * BLOCK_A), BF16)
    v_ones = [jnp.concatenate([v_t[h * HD_A:(h + 1) * HD_A, :], ones_rows], axis=0)
              for h in range(N_KV_A)]
    k_pads = []
    for h in range(N_KV_A):
        k = jnp.concatenate([prev_ref[0, :, h * HD_A:(h + 1) * HD_A],
                             cur_ref[0, :, nq + h * HD_A:nq + (h + 1) * HD_A]], axis=0).astype(F32)
        kn = k * lax.rsqrt(jnp.mean(k * k, axis=-1, keepdims=True) + EPS) * kscale
        k_pads.append((jnp.concatenate([kn, zeros], axis=1).astype(BF16),
                       jnp.concatenate([zeros, kn], axis=1).astype(BF16)))

    def logits(i):
        j = 2 * i
        k_pad = k_pads[j // GROUP_A]
        q_tile = q_bf[:, j * HD_A:(j + 2) * HD_A]
        rq = jnp.concatenate([rq_t[j:j + 1, :], rq_t[j + 1:j + 2, :]], axis=1)
        s = jnp.concatenate([_nt_dot(k_pad[0], q_tile), _nt_dot(k_pad[1], q_tile)], axis=1) * rq
        return s + mask_bias

    def attend(i, s):
        j = 2 * i
        sink = jnp.where(first_half, sink_ref[j], sink_ref[j + 1]) * LOG2E
        mx = jnp.maximum(jnp.max(s, axis=0, keepdims=True), sink)
        p = jnp.exp2(s - mx)
        return _dot(v_ones[j // GROUP_A], p.astype(BF16)), jnp.exp2(sink - mx)

    def finish(i, pv, sink_term):
        o_t = pv[:HD_A] / (pv[HD_A:HD_A + 1] + sink_term)
        o_pair = jnp.concatenate([o_t[:, :BLOCK_A], o_t[:, BLOCK_A:]], axis=0)
        o_ref[0, :, 2 * i * HD_A:(2 * i + 2) * HD_A] = o_pair.T.astype(o_ref.dtype)

    n_pairs = N_KV_A * GROUP_A // 2
    ahead = 4
    pending = [logits(i) for i in range(ahead)]
    unfinished = None
    for i in range(n_pairs):
        if i + ahead < n_pairs:
            pending.append(logits(i + ahead))
        out = attend(i, pending.pop(0))
        if unfinished is not None:
            finish(i - 1, *unfinished)
        unfinished = out
    finish(n_pairs - 1, *unfinished)


def swa_attention(qkv, q_norm, k_norm, sinks):
    b, l, n = qkv.shape
    nq = N_KV_A * GROUP_A * HD_A
    kv_w = 2 * N_KV_A * HD_A
    head_sel = (jnp.arange(nq)[:, None] // HD_A == jnp.arange(LANES)[None, :]).astype(BF16)
    return pl.pallas_call(
        _swa_kernel,
        grid=(b, l // BLOCK_A),
        in_specs=[
            pl.BlockSpec(memory_space=pltpu.SMEM),
            pl.BlockSpec((1, BLOCK_A, n), lambda i, j: (i, j, 0)),
            pl.BlockSpec((1, BLOCK_A, kv_w), lambda i, j: (i, jnp.maximum(j - 1, 0), nq // kv_w)),
            pl.BlockSpec((1, HD_A), lambda i, j: (0, 0)),
            pl.BlockSpec((1, HD_A), lambda i, j: (0, 0)),
            pl.BlockSpec((nq, LANES), lambda i, j: (0, 0)),
        ],
        out_specs=pl.BlockSpec((1, BLOCK_A, nq), lambda i, j: (i, j, 0)),
        out_shape=jax.ShapeDtypeStruct((b, l, nq), BF16),
        compiler_params=_params("parallel", "parallel"),
        name="swa_attention",
    )(sinks, qkv, qkv, q_norm.reshape(1, HD_A), k_norm.reshape(1, HD_A), head_sel)


def _split3(x):
    hi = x.astype(BF16)
    r1 = x - hi.astype(F32)
    mid = r1.astype(BF16)
    lo = (r1 - mid.astype(F32)).astype(BF16)
    return hi, mid, lo


def _gla_kernel(q_ref, k_ref, v_ref, r_ref, gl_ref, wgb_ref, gb_ref, on_ref, o_ref, st_ref,
                *, tb, dk, dv):
    @pl.when(pl.program_id(1) == 0)
    def _():
        st_ref[...] = jnp.zeros_like(st_ref)

    ti = lax.broadcasted_iota(jnp.int32, (CHUNK, CHUNK), 0)
    si = lax.broadcasted_iota(jnp.int32, (CHUNK, CHUNK), 1)
    causal = si <= ti
    tri = causal.astype(BF16)
    wgb = wgb_ref[...]
    gb = gb_ref[...]
    on = on_ref[...]

    def decays(c):
        rs = pl.ds(c * CHUNK, CHUNK)
        gate = _dot(gl_ref[0, rs, :].astype(BF16), wgb) + gb
        log_a = (jnp.minimum(gate, 0.0) - jnp.log(1.0 + jnp.exp(-jnp.abs(gate)))) * (1.0 / GATE_TAU)
        hi, mid, lo = _split3(log_a)
        bcum = _dot(tri, hi) + _dot(tri, mid) + _dot(tri, lo)
        b_last = bcum[CHUNK - 1:CHUNK, :]
        q = q_ref[0, rs, :].astype(F32)
        k = k_ref[0, rs, :].astype(F32)
        q_dec = (q * dk ** -0.5 * jnp.exp(bcum)).astype(BF16)
        k_inv = (k * jnp.exp(-bcum)).astype(BF16)
        k_end = (k * jnp.exp(b_last - bcum)).astype(BF16)
        return q_dec, k_inv, k_end, jnp.exp(b_last)

    def recur(c, q_dec, k_inv, k_end, decay):
        rs = pl.ds(c * CHUNK, CHUNK)
        for h in range(N_HEADS_B):
            ks = slice(h * dk, (h + 1) * dk)
            vs = slice(h * dv, (h + 1) * dv)
            v = v_ref[0, rs, vs].astype(BF16)
            a = jnp.where(causal, _nt_dot(q_dec[:, ks], k_inv[:, ks]), 0.0)
            st = st_ref[h]
            o = _dot(a.astype(BF16), v) + _nt_dot(q_dec[:, ks], st.astype(BF16))
            st_ref[h] = st * decay[:, ks] + _tn_dot(v, k_end[:, ks])
            o = _rms(o, on)
            r = r_ref[0, rs, vs].astype(F32)
            o_ref[0, rs, vs] = (o * (r * (1.0 / (1.0 + jnp.exp(-r))))).astype(o_ref.dtype)

    n_chunks = tb // CHUNK
    ready = decays(0)
    for c in range(n_chunks):
        cur = ready
        if c + 1 < n_chunks:
            ready = decays(c + 1)
        recur(c, *cur)


def gla_attention(proj, g_low, w_gate_b, gate_bias, o_norm, *, dk, dv, tb):
    b, l, _ = proj.shape
    dk_all, dv_all = N_HEADS_B * dk, N_HEADS_B * dv
    return pl.pallas_call(
        functools.partial(_gla_kernel, tb=tb, dk=dk, dv=dv),
        grid=(b, l // tb),
        in_specs=[
            pl.BlockSpec((1, tb, dk_all), lambda i, j: (i, j, 0)),
            pl.BlockSpec((1, tb, dk_all), lambda i, j: (i, j, 1)),
            pl.BlockSpec((1, tb, dv_all), lambda i, j: (i, j, 2 * dk_all // dv_all)),
            pl.BlockSpec((1, tb, dv_all), lambda i, j: (i, j, 2 * dk_all // dv_all + 1)),
            pl.BlockSpec((1, tb, LANES), lambda i, j: (i, j, 0)),
            pl.BlockSpec((LANES, dk_all), lambda i, j: (0, 0)),
            pl.BlockSpec((1, dk_all), lambda i, j: (0, 0)),
            pl.BlockSpec((1, dv), lambda i, j: (0, 0)),
        ],
        out_specs=pl.BlockSpec((1, tb, dv_all), lambda i, j: (i, j, 0)),
        out_shape=jax.ShapeDtypeStruct((b, l, dv_all), BF16),
        scratch_shapes=[pltpu.VMEM((N_HEADS_B, dv, dk), F32)],
        compiler_params=_params("parallel", "arbitrary"),
        name="gla_attention",
    )(proj, proj, proj, proj, g_low, w_gate_b, gate_bias.reshape(1, dk_all), o_norm.reshape(1, dv))


def _dsa_prep_kernel(x_ref, g_ref, win_ref, qln_ref, kvn_ref, wuq_ref, wuk_ref, qan_ref, wiq_ref,
                     ikn_ref, ikb_ref, qabs_ref, qidx_ref, widx_ref, ckv_ref, ckvt_ref, kidx_ref):
    xn = _rms(x_ref[...], g_ref[...]).astype(BF16)
    proj = _nt_dot(xn, win_ref[...])
    o_kv = Q_RANK_C
    o_ki = Q_RANK_C + KV_RANK_C
    o_wi = o_ki + IDX_DIM
    c_q = _rms(proj[:, :o_kv], qln_ref[...]).astype(BF16)
    c_kv = _rms(proj[:, o_kv:o_ki], kvn_ref[...])
    ckv_ref[...] = c_kv.astype(BF16)
    ckvt_ref[...] = c_kv.T.astype(BF16)
    k_idx = proj[:, o_ki:o_wi]
    mu = jnp.mean(k_idx, axis=-1, keepdims=True)
    kc = k_idx - mu
    k_idx = kc * lax.rsqrt(jnp.mean(kc * kc, axis=-1, keepdims=True) + EPS)
    kidx_ref[...] = (k_idx * ikn_ref[...] + ikb_ref[...]).astype(BF16)
    widx_ref[...] = proj[:, o_wi:o_wi + N_IDX_HEADS] * (N_IDX_HEADS ** -0.5 * IDX_DIM ** -0.5)
    q_nope = _dot(c_q, wuq_ref[...])
    q_idx = _dot(c_q, wiq_ref[...])
    qan = qan_ref[...] * (KV_RANK_C ** -0.5 * LOG2E)
    for h in range(N_HEADS_C):
        qh = q_nope[:, h * QK_HEAD_C:(h + 1) * QK_HEAD_C].astype(BF16)
        qa = _rms(_dot(qh, wuk_ref[h]), qan).astype(BF16)
        qi = q_idx[:, h * IDX_DIM:(h + 1) * IDX_DIM].astype(BF16)
        for j in range(qabs_ref.shape[0]):
            qabs_ref[j, h] = qa[j * C_BLOCK:(j + 1) * C_BLOCK]
            qidx_ref[j, h] = qi[j * C_BLOCK:(j + 1) * C_BLOCK]


def dsa_prep(x, g, w_in, q_lat_norm, kv_norm, w_uq, w_uk, q_abs_norm, w_iq, idx_k_norm, idx_k_bias,
             *, tm):
    m, d = x.shape
    nblk = m // C_BLOCK
    bpt = tm // C_BLOCK
    n_in = w_in.shape[0]
    const2 = lambda i: (0, 0)
    return pl.pallas_call(
        _dsa_prep_kernel,
        grid=(m // tm,),
        in_specs=[
            pl.BlockSpec((tm, d), lambda i: (i, 0)),
            pl.BlockSpec((1, d), const2),
            pl.BlockSpec((n_in, d), const2),
            pl.BlockSpec((1, Q_RANK_C), const2),
            pl.BlockSpec((1, KV_RANK_C), const2),
            pl.BlockSpec(w_uq.shape, const2),
            pl.BlockSpec(w_uk.shape, lambda i: (0, 0, 0)),
            pl.BlockSpec((1, KV_RANK_C), const2),
            pl.BlockSpec(w_iq.shape, const2),
            pl.BlockSpec((1, IDX_DIM), const2),
            pl.BlockSpec((1, IDX_DIM), const2),
        ],
        out_specs=[
            pl.BlockSpec((bpt, N_HEADS_C, C_BLOCK, KV_RANK_C), lambda i: (i, 0, 0, 0)),
            pl.BlockSpec((bpt, N_IDX_HEADS, C_BLOCK, IDX_DIM), lambda i: (i, 0, 0, 0)),
            pl.BlockSpec((tm, N_IDX_HEADS), lambda i: (i, 0)),
            pl.BlockSpec((tm, KV_RANK_C), lambda i: (i, 0)),
            pl.BlockSpec((KV_RANK_C, tm), lambda i: (0, i)),
            pl.BlockSpec((tm, IDX_DIM), lambda i: (i, 0)),
        ],
        out_shape=[
            jax.ShapeDtypeStruct((nblk, N_HEADS_C, C_BLOCK, KV_RANK_C), BF16),
            jax.ShapeDtypeStruct((nblk, N_IDX_HEADS, C_BLOCK, IDX_DIM), BF16),
            jax.ShapeDtypeStruct((m, N_IDX_HEADS), F32),
            jax.ShapeDtypeStruct((m, KV_RANK_C), BF16),
            jax.ShapeDtypeStruct((KV_RANK_C, m), BF16),
            jax.ShapeDtypeStruct((m, IDX_DIM), BF16),
        ],
        compiler_params=_params("parallel"),
        name="dsa_prep",
    )(x, g.reshape(1, d), w_in, q_lat_norm.reshape(1, -1), kv_norm.reshape(1, -1), w_uq, w_uk,
      q_abs_norm.reshape(1, -1), w_iq, idx_k_norm.reshape(1, -1), idx_k_bias.reshape(1, -1))


def _sortable_key(s):
    bits = pltpu.bitcast(s, jnp.int32)
    return bits ^ ((bits >> 31) & 0x7FFFFFFF)


def _dsa_attn_kernel(qabs_ref, qidx_ref, widx_ref, ckv_ref, ckvt_ref, kidx_ref, wuv_ref, qan_ref,
                     kvn_ref, o_ref, key_ref, m_ref, acc_ref, *, topk, idx_bits):
    n = pl.program_id(1)
    tk = KEY_TILE_C
    n_tiles = (n * C_BLOCK + C_BLOCK + tk - 1) // tk
    q_lane = n * C_BLOCK + lax.broadcasted_iota(jnp.int32, (tk, C_BLOCK), 1)
    k_row = lax.broadcasted_iota(jnp.int32, (tk, C_BLOCK), 0)
    widx = widx_ref[0, 0]

    def tile(t):
        return pl.ds(pl.multiple_of(t * tk, tk), tk)

    def score_tile(t, carry):
        kt = kidx_ref[0, tile(t), :]
        score = jnp.zeros((tk, 2 * C_BLOCK), F32)
        for p in range(N_IDX_HEADS // 2):
            q = qidx_ref[0, 2 * p:2 * p + 2].reshape(2 * C_BLOCK, IDX_DIM)
            w = jnp.concatenate([widx[2 * p:2 * p + 1, :], widx[2 * p + 1:2 * p + 2, :]], axis=1)
            score = score + jnp.maximum(_nt_dot(kt, q), 0.0) * w
        score = score[:, :C_BLOCK] + score[:, C_BLOCK:]
        score = jnp.where(t * tk + k_row <= q_lane, score, NEG_INF)
        key_ref[tile(t), :] = _sortable_key(score)
        return carry

    lax.fori_loop(0, n_tiles, score_tile, 0)

    def count(pred):
        def body(t, cnt):
            hit = pred(key_ref[tile(t), :], t).astype(jnp.int32)
            return cnt + jnp.sum(hit.reshape(tk // 8, 8, C_BLOCK), axis=0)
        cnt = lax.fori_loop(0, n_tiles, body, jnp.zeros((8, C_BLOCK), jnp.int32))
        return jnp.sum(cnt, axis=0, keepdims=True)

    def bisect(i, carry):
        lo, n_lo = carry
        cand = lo + lax.shift_left(jnp.int32(1), 31 - i)
        n_cand = count(lambda k, t: k >= cand)
        take = n_cand >= topk
        return jnp.where(take, cand, lo), jnp.where(take, n_cand, n_lo)

    thresh, n_ge = lax.fori_loop(
        0, 32, bisect, (jnp.full((1, C_BLOCK), INT_MIN, jnp.int32),
                        jnp.zeros((1, C_BLOCK), jnp.int32) + n_tiles * tk))

    excess = jnp.max(jnp.where((n_ge > topk) & (thresh > NEG_KEY), 1, 0))

    @pl.when(excess > 0)
    def _():
        want = topk - count(lambda k, t: k > thresh)

        def index_bisect(i, j0):
            cand = j0 + lax.shift_left(jnp.int32(1), idx_bits - 1 - i)
            before = count(lambda k, t: (k == thresh) & (t * tk + k_row < cand))
            return jnp.where(before < want, cand, j0)

        j0 = lax.fori_loop(0, idx_bits, index_bisect, jnp.zeros((1, C_BLOCK), jnp.int32))

        def demote(t, carry):
            keys = key_ref[tile(t), :]
            drop = (keys == thresh) & (t * tk + k_row > j0)
            key_ref[tile(t), :] = jnp.where(drop, thresh - 1, keys)
            return carry

        lax.fori_loop(0, n_tiles, demote, 0)

    thr = jnp.maximum(thresh, NEG_KEY + 1)
    thr2 = jnp.concatenate([thr, thr], axis=1)
    acc_ref[...] = jnp.zeros_like(acc_ref)
    ones_rows = jnp.ones((ONES_ROWS, tk), BF16)
    n_pairs = N_HEADS_C // 2

    def pair_q(p):
        return qabs_ref[0, 2 * p:2 * p + 2].reshape(2 * C_BLOCK, KV_RANK_C)

    bound = (KV_RANK_C * BOUND_SLACK_C * (KV_RANK_C ** -0.5 * LOG2E)
             * jnp.max(jnp.abs(qan_ref[...])) * jnp.max(jnp.abs(kvn_ref[...])))
    small_logits = bound <= MAX_FIXED_OFFSET_C

    def attn_loop(accumulate, offset):
        def attn_tile(t, carry):
            kv = ckv_ref[0, tile(t), :]
            kvt = jnp.concatenate([ckvt_ref[:, tile(t)], ones_rows], axis=0)
            keys = key_ref[tile(t), :]
            sel = jnp.concatenate([keys, keys], axis=1) >= thr2
            bias = jnp.where(sel, -offset, NEG_INF)

            def logits(p):
                return _nt_dot(kv, pair_q(p)) + bias

            pending = [logits(p) for p in range(ATTN_AHEAD_C)]
            for p in range(n_pairs):
                if p + ATTN_AHEAD_C < n_pairs:
                    pending.append(logits(p + ATTN_AHEAD_C))
                accumulate(p, pending.pop(0), kvt)
            return carry

        lax.fori_loop(0, n_tiles, attn_tile, 0)

    def accumulate_fixed(p, s, kvt):
        acc_ref[p] += _dot(kvt, jnp.exp2(s).astype(BF16))

    def accumulate_online(p, s, kvt):
        m_old = m_ref[p]
        m_new = jnp.maximum(m_old, jnp.max(s, axis=0, keepdims=True))
        pe = jnp.exp2(s - m_new)
        acc_ref[p] = jnp.exp2(m_old - m_new) * acc_ref[p] + _dot(kvt, pe.astype(BF16))
        m_ref[p] = m_new

    @pl.when(small_logits)
    def _():
        attn_loop(accumulate_fixed, bound)

    @pl.when(jnp.logical_not(small_logits))
    def _():
        m_ref[...] = jnp.full(m_ref.shape, NEG_INF, F32)
        attn_loop(accumulate_online, 0.0)

    for p in range(N_HEADS_C // 2):
        denom = acc_ref[p, KV_RANK_C:KV_RANK_C + 1, :]
        o_lat_t = (acc_ref[p, :KV_RANK_C, :] / denom).astype(BF16)
        for e in range(2):
            h = 2 * p + e
            o_ref[0, :, h * V_HEAD_C:(h + 1) * V_HEAD_C] = _tn_dot(
                o_lat_t[:, e * C_BLOCK:(e + 1) * C_BLOCK], wuv_ref[h]).astype(o_ref.dtype)


def dsa_attention(qabs, qidx, widx, ckv, ckvt, kidx, w_uv, q_abs_norm, kv_norm, *, batch, topk):
    nblk = qabs.shape[0]
    nb = nblk // batch
    l = nb * C_BLOCK
    assert l % KEY_TILE_C == 0 and l >= topk
    ckv = ckv.reshape(batch, l, KV_RANK_C)
    kidx = kidx.reshape(batch, l, IDX_DIM)
    widx = widx.reshape(batch, nb, C_BLOCK, N_IDX_HEADS).transpose(0, 1, 3, 2)
    return pl.pallas_call(
        functools.partial(_dsa_attn_kernel, topk=topk, idx_bits=(l - 1).bit_length()),
        grid=(batch, nb),
        in_specs=[
            pl.BlockSpec((1, N_HEADS_C, C_BLOCK, KV_RANK_C), lambda b, j: (b * nb + j, 0, 0, 0)),
            pl.BlockSpec((1, N_IDX_HEADS, C_BLOCK, IDX_DIM), lambda b, j: (b * nb + j, 0, 0, 0)),
            pl.BlockSpec((1, 1, N_IDX_HEADS, C_BLOCK), lambda b, j: (b, j, 0, 0)),
            pl.BlockSpec((1, l, KV_RANK_C), lambda b, j: (b, 0, 0)),
            pl.BlockSpec((KV_RANK_C, l), lambda b, j: (0, b)),
            pl.BlockSpec((1, l, IDX_DIM), lambda b, j: (b, 0, 0)),
            pl.BlockSpec(w_uv.shape, lambda b, j: (0, 0, 0)),
            pl.BlockSpec((1, KV_RANK_C), lambda b, j: (0, 0)),
            pl.BlockSpec((1, KV_RANK_C), lambda b, j: (0, 0)),
        ],
        out_specs=pl.BlockSpec((1, C_BLOCK, N_HEADS_C * V_HEAD_C), lambda b, j: (b, j, 0)),
        out_shape=jax.ShapeDtypeStruct((batch, l, N_HEADS_C * V_HEAD_C), BF16),
        scratch_shapes=[
            pltpu.VMEM((l, C_BLOCK), jnp.int32),
            pltpu.VMEM((N_HEADS_C // 2, 1, 2 * C_BLOCK), F32),
            pltpu.VMEM((N_HEADS_C // 2, KV_RANK_C + ONES_ROWS, 2 * C_BLOCK), F32),
        ],
        compiler_params=_params("parallel", "arbitrary"),
        name="dsa_attention",
    )(qabs, qidx, widx, ckv, ckvt, kidx, w_uv, q_abs_norm.reshape(1, -1), kv_norm.reshape(1, -1))


TM = 512
TF = 512
FFN_NORM_ROWS = 512


def _swa_layer(h, batch, norm_mix, w_in, q_norm, k_norm, sinks, wo):
    m, d = h.shape
    qkv = norm_matmul(h, norm_mix, w_in.astype(BF16), tm=TM, tn=w_in.shape[1], out_dtype=BF16)
    o = swa_attention(qkv.reshape(batch, m // batch, -1), q_norm, k_norm, sinks)
    return matmul_res(o.reshape(m, -1), wo.astype(BF16), h, tm=TM, tn=d)


def _gla_layer(h, batch, norm_mix, w_in, w_gate_b, gate_bias, o_norm, wo):
    m, d = h.shape
    dk_all = w_gate_b.shape[1]
    dv_all = wo.shape[0]
    n_main = 2 * dk_all + 2 * dv_all
    wgb_pad = jnp.pad(w_gate_b.astype(BF16), ((0, LANES - GATE_RANK), (0, 0)))
    proj, g_low = norm_matmul_gate(h, norm_mix, w_in, n_main, tm=2 * TM, tn=n_main // 8)
    o = gla_attention(proj.reshape(batch, m // batch, -1), g_low.reshape(batch, m // batch, -1),
                      wgb_pad, gate_bias, o_norm,
                      dk=dk_all // N_HEADS_B, dv=dv_all // N_HEADS_B, tb=256)
    return matmul_res(o.reshape(m, -1), wo.astype(BF16), h, tm=TM, tn=d)


def _dsa_layer(h, batch, norm_mix, w_in, q_lat_norm, kv_norm, w_uq, w_uk, q_abs_norm, w_iq,
               idx_k_norm, idx_k_bias, w_uv, wo):
    m, d = h.shape
    l = m // batch
    qabs, qidx, widx, ckv, ckvt, kidx = dsa_prep(
        h, norm_mix, w_in.T.astype(BF16), q_lat_norm, kv_norm, w_uq.astype(BF16), w_uk.astype(BF16),
        q_abs_norm, w_iq.astype(BF16), idx_k_norm, idx_k_bias, tm=TM)
    o = dsa_attention(qabs, qidx, widx, ckv, ckvt, kidx, w_uv.astype(BF16), q_abs_norm, kv_norm, batch=batch,
                      topk=min(TOPK_MAX, l // 4))
    return matmul_res(o.reshape(m, -1), wo.astype(BF16), h, tm=TM, tn=d)


def kernel(x, l0_norm_mix, l0_w_in, l0_q_norm, l0_k_norm, l0_sinks, l0_wo, l0_norm_ffn, l0_w_gate_up, l0_w_down, l1_norm_mix, l1_w_in, l1_w_gate_b, l1_gate_bias, l1_o_norm, l1_wo, l1_norm_ffn, l1_w_gate_up, l1_w_down, l2_norm_mix, l2_w_in, l2_q_lat_norm, l2_kv_norm, l2_w_uq, l2_w_uk, l2_q_abs_norm, l2_w_iq, l2_idx_k_norm, l2_idx_k_bias, l2_w_uv, l2_wo, l2_norm_ffn, l2_w_gate_up, l2_w_down, l3_norm_mix, l3_w_in, l3_q_norm, l3_k_norm, l3_sinks, l3_wo, l3_norm_ffn, l3_w_gate_up, l3_w_down):
    batch, seq, d = x.shape
    h = x.reshape(batch * seq, d)

    def channel_mix(h, norm_ffn, w_gate_up, w_down):
        return ffn(h, norm_ffn, w_gate_up, w_down, tm=2 * TM, tf=TF // 2)

    h = _swa_layer(h, batch, l0_norm_mix, l0_w_in, l0_q_norm, l0_k_norm, l0_sinks, l0_wo)
    h = channel_mix(h, l0_norm_ffn, l0_w_gate_up, l0_w_down)
    h = _gla_layer(h, batch, l1_norm_mix, l1_w_in, l1_w_gate_b, l1_gate_bias, l1_o_norm, l1_wo)
    h = channel_mix(h, l1_norm_ffn, l1_w_gate_up, l1_w_down)
    h = _dsa_layer(h, batch, l2_norm_mix, l2_w_in, l2_q_lat_norm, l2_kv_norm, l2_w_uq, l2_w_uk,
                   l2_q_abs_norm, l2_w_iq, l2_idx_k_norm, l2_idx_k_bias, l2_w_uv, l2_wo)
    h = channel_mix(h, l2_norm_ffn, l2_w_gate_up, l2_w_down)
    h = _swa_layer(h, batch, l3_norm_mix, l3_w_in, l3_q_norm, l3_k_norm, l3_sinks, l3_wo)
    h = channel_mix(h, l3_norm_ffn, l3_w_gate_up, l3_w_down)
    return h.reshape(batch, seq, d)
```
